```python
import jax, jax.numpy as jnp
from jax import lax
import numpy as np

D_MODEL = 2048
BATCH = 1
SEQ = 16384
DEPTH = 1

D_PLE = 256
D_MIX = D_MODEL
RW_HEAD_DIM = 64
RW_WIDTH = D_MIX // 2
RW_HEADS = RW_WIDTH // RW_HEAD_DIM
DECAY_LORA = 64
AAA_LORA = 64
FX_HEAD_DIM = 64
FX_WIDTH = D_MIX - RW_WIDTH
FX_HEADS = FX_WIDTH // FX_HEAD_DIM
Q_BLOCK = 128
RMS_EPS = 1e-6
GN_EPS = 64e-5
_SPLIT_SIZES = (RW_WIDTH, RW_WIDTH, RW_WIDTH, RW_WIDTH, DECAY_LORA, AAA_LORA,
                FX_WIDTH, FX_WIDTH, FX_WIDTH, FX_WIDTH, FX_HEADS)
IN_COLS = sum(_SPLIT_SIZES)

kernel_name = "hybrid_rwkv7_fox_parallel_heads"


def _split_offsets():
    offs, acc = [], 0
    for s in _SPLIT_SIZES[:-1]:
        acc += s
        offs.append(acc)
    return offs


def rms_norm(x, g, eps=RMS_EPS):
    xf = x.astype(jnp.float32)
    y = xf * lax.rsqrt(jnp.mean(xf * xf, axis=-1, keepdims=True) + eps)
    return (y * g.astype(jnp.float32)).astype(x.dtype)


def token_shift(z, mu):
    z_prev = jnp.pad(z, ((0, 0), (1, 0), (0, 0)))[:, :-1]
    return z + (z_prev - z) * mu


def rwkv7_scan(r, decay, k, v, a_vec, b_vec):
    B, T, H, N = r.shape

    def step(S, inp):
        r_t, w_t, k_t, v_t, a_t, b_t = inp
        sa = jnp.einsum('bhvk,bhk->bhv', S, a_t)
        S = (S * w_t[:, :, None, :] + sa[..., None] * b_t[:, :, None, :]
             + v_t[..., None] * k_t[:, :, None, :])
        y = jnp.einsum('bhvk,bhk->bhv', S, r_t)
        return S, y

    xs = tuple(jnp.moveaxis(t, 1, 0) for t in (r, decay, k, v, a_vec, b_vec))
    S0 = jnp.zeros((B, H, N, N), jnp.float32)
    _, ys = lax.scan(step, S0, xs)
    return jnp.moveaxis(ys, 0, 1)


def rwkv7_branch(zr, zk, zv, zw, za, mu_r, mu_k, mu_v, mu_w, mu_a,
                 w0, w2, a0, a2, k_k, k_a, r_k, ln_w, ln_b):
    B, T, C = zr.shape
    H, N = RW_HEADS, RW_HEAD_DIM
    f32 = jnp.float32
    r = token_shift(zr, mu_r).astype(f32)
    k = token_shift(zk, mu_k).astype(f32)
    v = token_shift(zv, mu_v).astype(f32)
    dw = token_shift(zw, mu_w)
    da = token_shift(za, mu_a)
    w_log = -jax.nn.softplus(-(w0 + jnp.tanh(dw) @ w2).astype(f32)) - 0.5
    decay = jnp.exp(-jnp.exp(w_log))
    a = jax.nn.sigmoid((a0 + da @ a2).astype(f32))
    kk = (k * k_k.astype(f32)).reshape(B, T, H, N)
    kk = kk / jnp.maximum(jnp.linalg.norm(kk, axis=-1, keepdims=True), 1e-12)
    k = k * (1.0 + (a - 1.0) * k_a.astype(f32))
    hv = lambda t: t.reshape(B, T, H, N)
    r4, k4, v4, a4, d4 = hv(r), hv(k), hv(v), hv(a), hv(decay)
    y = rwkv7_scan(r4, d4, k4, v4, -kk, kk * a4)
    mean = jnp.mean(y, axis=-1, keepdims=True)
    var = jnp.var(y, axis=-1, keepdims=True)
    y = ((y - mean) * lax.rsqrt(var + GN_EPS)).reshape(B, T, C)
    y = y * ln_w.astype(f32) + ln_b.astype(f32)
    bonus = jnp.sum(r4 * k4 * r_k.astype(f32), axis=-1, keepdims=True) * v4
    return (y + bonus.reshape(B, T, C)).astype(zr.dtype)


def forgetting_attention(q, k, v, f_logit, b_f, q_g, k_g):
    B, T, _ = q.shape
    H, Dh = FX_HEADS, FX_HEAD_DIM
    f32 = jnp.float32
    q = rms_norm(q.reshape(B, T, H, Dh), q_g).astype(f32).transpose(0, 2, 1, 3)
    k = rms_norm(k.reshape(B, T, H, Dh), k_g).astype(f32).transpose(0, 2, 1, 3)
    v = v.reshape(B, T, H, Dh).astype(f32).transpose(0, 2, 1, 3)
    log_f = jax.nn.log_sigmoid((f_logit + b_f).astype(f32))
    F = jnp.cumsum(log_f, axis=1).transpose(0, 2, 1)
    scale = Dh ** -0.5
    k_pos = jnp.arange(T)

    def block(i):
        start = i * Q_BLOCK
        qb = lax.dynamic_slice_in_dim(q, start, Q_BLOCK, axis=2)
        Fq = lax.dynamic_slice_in_dim(F, start, Q_BLOCK, axis=2)
        s = jnp.einsum('bhqd,bhkd->bhqk', qb, k) * scale
        s = s + Fq[..., :, None] - F[..., None, :]
        q_pos = start + jnp.arange(Q_BLOCK)
        s = jnp.where(k_pos[None, :] <= q_pos[:, None], s, -jnp.inf)
        pr = jax.nn.softmax(s, axis=-1)
        return jnp.einsum('bhqk,bhkd->bhqd', pr, v)

    out = lax.map(block, jnp.arange(T // Q_BLOCK))
    out = out.transpose(1, 0, 3, 2, 4).reshape(B, T, H * Dh)
    return out.astype(f_logit.dtype)


def setup_inputs(seed: int = 0) -> dict:
    key = jax.random.key(seed)
    ks = iter(jax.random.split(key, 32))
    f32 = jnp.float32
    nrm = lambda shape, s: jax.random.normal(next(ks), shape, f32) * s
    gain = lambda shape: 1.0 + 0.02 * jax.random.normal(next(ks), shape, f32)
    uni = lambda shape, lo, hi: jax.random.uniform(next(ks), shape, f32, lo, hi)
    L = DEPTH
    return {
        "x": nrm((BATCH, SEQ, D_MODEL), 1.0),
        "p": nrm((DEPTH, BATCH, SEQ, D_PLE), 1.0),
        "pre_norm_g": gain((L, D_MODEL)),
        "w_in": nrm((L, D_MODEL, IN_COLS), D_MODEL ** -0.5),
        "rw_mu_r": uni((L, RW_WIDTH), 0.0, 1.0),
        "rw_mu_k": uni((L, RW_WIDTH), 0.0, 1.0),
        "rw_mu_v": uni((L, RW_WIDTH), 0.0, 1.0),
        "rw_mu_w": uni((L, DECAY_LORA), 0.0, 1.0),
        "rw_mu_a": uni((L, AAA_LORA), 0.0, 1.0),
        "rw_w0": uni((L, RW_WIDTH), -3.0, 0.5),
        "rw_w2": nrm((L, DECAY_LORA, RW_WIDTH), 0.5 * DECAY_LORA ** -0.5),
        "rw_a0": nrm((L, RW_WIDTH), 0.1),
        "rw_a2": nrm((L, AAA_LORA, RW_WIDTH), AAA_LORA ** -0.5),
        "rw_k_k": 0.85 + nrm((L, RW_WIDTH), 0.02),
        "rw_k_a": gain((L, RW_WIDTH)),
        "rw_r_k": nrm((L, RW_HEADS, RW_HEAD_DIM), 0.1),
        "rw_ln_w": gain((L, RW_WIDTH)),
        "rw_ln_b": nrm((L, RW_WIDTH), 0.01),
        "fx_b_f": uni((L, FX_HEADS), 1.0, 5.0),
        "fx_q_g": gain((L, FX_HEAD_DIM)),
        "fx_k_g": gain((L, FX_HEAD_DIM)),
        "w_out": nrm((L, D_MIX, D_MODEL), D_MIX ** -0.5),
        "post_norm_g": gain((L, D_MODEL)),
        "ple_norm_g": gain((L, D_MODEL)),
        "w_ple_gate": nrm((L, D_MODEL, D_MODEL), D_MODEL ** -0.5),
        "w_ple": nrm((L, D_PLE, D_MODEL), D_PLE ** -0.5),
    }


def reference(x, p, pre_norm_g, w_in, rw_mu_r, rw_mu_k, rw_mu_v, rw_mu_w, rw_mu_a,
              rw_w0, rw_w2, rw_a0, rw_a2, rw_k_k, rw_k_a, rw_r_k, rw_ln_w, rw_ln_b,
              fx_b_f, fx_q_g, fx_k_g, w_out, post_norm_g, ple_norm_g, w_ple_gate, w_ple):
    offs = _split_offsets()
    for i in range(DEPTH):
        h = rms_norm(x, pre_norm_g[i])
        z = h @ w_in[i]
        (rr, rk, rv, rg, rw, ra, fq, fk, fv, fg, ff) = jnp.split(z, offs, axis=-1)
        y_rw = rwkv7_branch(rr, rk, rv, rw, ra, rw_mu_r[i], rw_mu_k[i], rw_mu_v[i],
                            rw_mu_w[i], rw_mu_a[i], rw_w0[i], rw_w2[i], rw_a0[i], rw_a2[i],
                            rw_k_k[i], rw_k_a[i], rw_r_k[i], rw_ln_w[i], rw_ln_b[i])
        y_fx = forgetting_attention(fq, fk, fv, ff, fx_b_f[i], fx_q_g[i], fx_k_g[i])
        y = jnp.concatenate([(y_rw * jax.nn.silu(rg)).astype(h.dtype),
                             (y_fx * jax.nn.silu(fg)).astype(h.dtype)], axis=-1)
        m = y @ w_out[i]
        x = x + rms_norm(m, post_norm_g[i])
        gate = jax.nn.sigmoid(rms_norm(x, ple_norm_g[i]) @ w_ple_gate[i])
        x = x + gate * (p[i] @ w_ple[i])
    return x
```

```python
import functools

import numpy as np
import jax
import jax.numpy as jnp
from jax import lax
from jax.experimental import pallas as pl
from jax.experimental.pallas import tpu as pltpu

f32 = jnp.float32
bf16 = jnp.bfloat16

D_MODEL = 2048
D_PLE = 256
WIDTH = 1024
HEAD = 64
NHEADS = 16
NPAIRS = NHEADS // 2
LORA = 64
LANES = 128
RMS_EPS = 1e-6
GN_EPS = 64e-5
CHUNK = 64
LOG2E = 1.4426950408889634
EXP_NEG_HALF = 0.6065306597126334
NEG_BIG = -1e30

Z_RW = 0
Z_FX = 4096
Z_LORA = 8192
Z_FF = 8320
Z_COLS = 8448
VT_ROWS = 80

VMEM_LIMIT = 56 * 1024 * 1024


def _dot(a, b):
    return jnp.dot(a, b, preferred_element_type=f32)


def _dot_nt(a, b):
    return lax.dot_general(a, b, (((1,), (1,)), ((), ())), preferred_element_type=f32)


def _dot_tn(a, b):
    return lax.dot_general(a, b, (((0,), (0,)), ((), ())), preferred_element_type=f32)


def _split2(x):
    hi = x.astype(bf16)
    lo = (x - hi.astype(f32)).astype(bf16)
    return hi, lo


def _split3(x):
    hi = x.astype(bf16)
    r = x - hi.astype(f32)
    mid = r.astype(bf16)
    lo = (r - mid.astype(f32)).astype(bf16)
    return hi, mid, lo


def _dot_sel(x, sel):
    hi, mid, lo = _split3(x)
    return _dot(hi, sel) + _dot(mid, sel) + _dot(lo, sel)


def _sel_dot(sel, x):
    hi, mid, lo = _split3(x)
    return _dot(sel, hi) + _dot(sel, mid) + _dot(sel, lo)


def _dot_x3(a, b):
    ah, al = _split2(a)
    bh, bl = _split2(b)
    return _dot(ah, bh) + _dot(al, bh) + _dot(ah, bl)


def _sigmoid(x):
    return 1.0 / (1.0 + jnp.exp(-x))


def _in_proj_kernel(x_ref, g_ref, w_ref, z_ref, h_ref):
    @pl.when(pl.program_id(1) == 0)
    def _():
        x = x_ref[...]
        ms = jnp.mean(x * x, axis=-1, keepdims=True)
        h_ref[...] = (x * lax.rsqrt(ms + RMS_EPS) * g_ref[...]).astype(bf16)

    z_ref[...] = _dot(h_ref[...], w_ref[...])


def _in_proj(x, g, w, *, tm, tn):
    T = x.shape[0]
    return pl.pallas_call(
        _in_proj_kernel,
        grid=(T // tm, Z_COLS // tn),
        in_specs=[
            pl.BlockSpec((tm, D_MODEL), lambda i, j: (i, 0)),
            pl.BlockSpec((1, D_MODEL), lambda i, j: (0, 0)),
            pl.BlockSpec((D_MODEL, tn), lambda i, j: (0, j)),
        ],
        out_specs=pl.BlockSpec((tm, tn), lambda i, j: (i, j)),
        out_shape=jax.ShapeDtypeStruct((T, Z_COLS), f32),
        scratch_shapes=[pltpu.VMEM((tm, D_MODEL), bf16)],
        compiler_params=pltpu.CompilerParams(
            dimension_semantics=("parallel", "arbitrary"), vmem_limit_bytes=VMEM_LIMIT),
        name="in_proj",
    )(x, g, w)


def _fox_prep_kernel(fq_ref, fk_ref, fv_ref, ff_ref, gq_ref, gk_ref, bf_ref,
                     qa_ref, ka_ref, vt_ref, carry_ref, *, tb):
    @pl.when(pl.program_id(0) == 0)
    def _():
        carry_ref[...] = jnp.zeros_like(carry_ref)

    ci = lax.broadcasted_iota(jnp.int32, (WIDTH, LANES), 0)
    hi_ = lax.broadcasted_iota(jnp.int32, (WIDTH, LANES), 1)
    ind = jnp.where((ci >> 6) == hi_, 1.0, 0.0).astype(bf16)
    hj = lax.broadcasted_iota(jnp.int32, (LANES, WIDTH), 0)
    cj = lax.broadcasted_iota(jnp.int32, (LANES, WIDTH), 1)
    ind_t = jnp.where((cj >> 6) == hj, 1.0, 0.0).astype(bf16)

    def head_rms(x, gain):
        ssq = _dot_sel(x * x, ind)
        rinv = lax.rsqrt(ssq * (1.0 / HEAD) + RMS_EPS)
        return x * _dot_sel(rinv, ind_t) * gain

    qn = head_rms(fq_ref[...], gq_ref[...]) * (HEAD ** -0.5 * LOG2E)
    kn = head_rms(fk_ref[...], gk_ref[...])

    xf = ff_ref[...] + bf_ref[...]
    logf = jnp.minimum(xf, 0.0) - jnp.log(1.0 + jnp.exp(-jnp.abs(xf)))
    ri = lax.broadcasted_iota(jnp.int32, (tb, tb), 0)
    rj = lax.broadcasted_iota(jnp.int32, (tb, tb), 1)
    tri = jnp.where(rj <= ri, 1.0, 0.0).astype(bf16)
    cum = _sel_dot(tri, logf) + carry_ref[0:1, :]
    carry_ref[...] = jnp.broadcast_to(cum[tb - 1:tb, :], carry_ref.shape)
    cum2 = cum * LOG2E
    c_hi, c_mid, c_lo = (t.astype(f32) for t in _split3(cum2))

    lane = lax.broadcasted_iota(jnp.int32, (tb, LANES), 1)
    low = lane < HEAD
    vt = fv_ref[...].T
    vpad = jnp.concatenate(
        [jnp.ones((1, tb), f32), jnp.zeros((VT_ROWS - HEAD - 1, tb), f32)], axis=0)
    for h in range(NHEADS):
        col = h // 2
        qc = qn[:, col * LANES:(col + 1) * LANES]
        kc = kn[:, col * LANES:(col + 1) * LANES]
        if h % 2 == 1:
            qc = pltpu.roll(qc, HEAD, axis=1)
            kc = pltpu.roll(kc, HEAD, axis=1)
        fh, fm, fl = c_hi[:, h:h + 1], c_mid[:, h:h + 1], c_lo[:, h:h + 1]
        eq = jnp.where(lane == HEAD, fh, jnp.where(lane == HEAD + 1, fm, jnp.where(lane == HEAD + 2, fl,
             jnp.where(lane < HEAD + 6, 1.0, 0.0))))
        ek = jnp.where(lane == HEAD + 3, -fh, jnp.where(lane == HEAD + 4, -fm, jnp.where(lane == HEAD + 5, -fl,
             jnp.where(lane < HEAD + 3, 1.0, 0.0))))
        qa_ref[h] = jnp.where(low, qc, eq).astype(bf16)
        ka_ref[h] = jnp.where(low, kc, ek).astype(bf16)
        vt_ref[h] = jnp.concatenate([vt[h * HEAD:(h + 1) * HEAD, :], vpad], axis=0).astype(bf16)


def _fox_prep(z, gq, gk, bfp, *, tb):
    T = z.shape[0]
    wb = Z_FX // WIDTH
    return pl.pallas_call(
        functools.partial(_fox_prep_kernel, tb=tb),
        grid=(T // tb,),
        in_specs=[
            pl.BlockSpec((tb, WIDTH), lambda i: (i, wb)),
            pl.BlockSpec((tb, WIDTH), lambda i: (i, wb + 1)),
            pl.BlockSpec((tb, WIDTH), lambda i: (i, wb + 2)),
            pl.BlockSpec((tb, LANES), lambda i: (i, Z_FF // LANES)),
            pl.BlockSpec((1, WIDTH), lambda i: (0, 0)),
            pl.BlockSpec((1, WIDTH), lambda i: (0, 0)),
            pl.BlockSpec((1, LANES), lambda i: (0, 0)),
        ],
        out_specs=[
            pl.BlockSpec((NHEADS, tb, LANES), lambda i: (0, i, 0)),
            pl.BlockSpec((NHEADS, tb, LANES), lambda i: (0, i, 0)),
            pl.BlockSpec((NHEADS, VT_ROWS, tb), lambda i: (0, 0, i)),
        ],
        out_shape=[
            jax.ShapeDtypeStruct((NHEADS, T, LANES), bf16),
            jax.ShapeDtypeStruct((NHEADS, T, LANES), bf16),
            jax.ShapeDtypeStruct((NHEADS, VT_ROWS, T), bf16),
        ],
        scratch_shapes=[pltpu.VMEM((8, LANES), f32)],
        compiler_params=pltpu.CompilerParams(
            dimension_semantics=("arbitrary",), vmem_limit_bytes=VMEM_LIMIT),
        name="fox_prep",
    )(z, z, z, z, gq, gk, bfp)


def _attn_kernel(qi_ref, kj_ref, qa_ref, ka_ref, vt_ref, fg_ref, o_ref, m_ref, acc_ref, *, tq):
    s = pl.program_id(1)
    i = qi_ref[s]
    j = kj_ref[s]

    @pl.when(j == 0)
    def _():
        m_ref[...] = jnp.full(m_ref.shape, NEG_BIG, f32)
        acc_ref[...] = jnp.zeros_like(acc_ref)

    def update(masked):
        for hh in range(2):
            st = _dot_nt(ka_ref[hh], qa_ref[hh])
            if masked:
                kpos = lax.broadcasted_iota(jnp.int32, st.shape, 0)
                qpos = lax.broadcasted_iota(jnp.int32, st.shape, 1)
                st = jnp.where(kpos <= qpos, st, NEG_BIG)
            m_old = m_ref[hh]
            m_new = jnp.maximum(m_old, jnp.max(st, axis=0, keepdims=True))
            alpha = jnp.exp2(m_old - m_new)
            p = jnp.exp2(st - m_new).astype(bf16)
            acc_ref[hh] = alpha * acc_ref[hh] + _dot(vt_ref[hh], p)
            m_ref[hh] = m_new

    @pl.when(j < i)
    def _():
        update(False)

    @pl.when(j == i)
    def _():
        update(True)
        outs = []
        for hh in range(2):
            acc = acc_ref[hh]
            outs.append(acc[0:HEAD, :] / acc[HEAD:HEAD + 1, :])
        y = jnp.concatenate(outs, axis=0).T
        g = fg_ref[...]
        o_ref[...] = (y * g * _sigmoid(g)).astype(o_ref.dtype)


def _attention(qa, ka, vt, z, *, tq):
    T = qa.shape[1]
    nq = T // tq
    qi = np.concatenate([np.full(i + 1, i, np.int32) for i in range(nq)])
    kj = np.concatenate([np.arange(i + 1, dtype=np.int32) for i in range(nq)])
    fg_col = (Z_FX + 3 * WIDTH) // LANES
    grid_spec = pltpu.PrefetchScalarGridSpec(
        num_scalar_prefetch=2,
        grid=(NPAIRS, len(qi)),
        in_specs=[
            pl.BlockSpec((2, tq, LANES), lambda g, s, qi, kj: (g, qi[s], 0)),
            pl.BlockSpec((2, tq, LANES), lambda g, s, qi, kj: (g, kj[s], 0)),
            pl.BlockSpec((2, VT_ROWS, tq), lambda g, s, qi, kj: (g, 0, kj[s])),
            pl.BlockSpec((tq, LANES), lambda g, s, qi, kj: (qi[s], fg_col + g)),
        ],
        out_specs=pl.BlockSpec((tq, LANES), lambda g, s, qi, kj: (qi[s], g)),
        scratch_shapes=[pltpu.VMEM((2, 1, tq), f32), pltpu.VMEM((2, VT_ROWS, tq), f32)],
    )
    return pl.pallas_call(
        functools.partial(_attn_kernel, tq=tq),
        grid_spec=grid_spec,
        out_shape=jax.ShapeDtypeStruct((T, WIDTH), bf16),
        compiler_params=pltpu.CompilerParams(
            dimension_semantics=("parallel", "arbitrary"), vmem_limit_bytes=VMEM_LIMIT),
        name="fox_attention",
    )(jnp.asarray(qi), jnp.asarray(kj), qa, ka, vt, z)


P_MU_R, P_MU_K, P_MU_V, P_W0, P_A0, P_KK, P_KA, P_RK, P_LNW, P_LNB = range(10)


def _rwkv_kernel(zr_ref, zk_ref, zv_ref, zg_ref, zl_ref, par_ref, mul_ref, w2_ref, a2_ref,
                 o_ref, s_ref, prev_ref, *, tb):
    @pl.when(pl.program_id(1) == 0)
    def _():
        s_ref[...] = jnp.zeros_like(s_ref)
        prev_ref[...] = jnp.zeros_like(prev_ref)

    par = par_ref[...]
    prow = lambda k: par[k:k + 1, :]
    row = lax.broadcasted_iota(jnp.int32, (tb, LANES), 0)

    def shift(x, mu, slot):
        xp = pltpu.roll(x, 1, axis=0)
        xp = jnp.where(row == 0, prev_ref[slot, 7:8, :], xp)
        prev_ref[slot] = x[tb - 8:tb, :]
        return x + (xp - x) * mu

    r = shift(zr_ref[...], prow(P_MU_R), 0)
    k = shift(zk_ref[...], prow(P_MU_K), 1)
    v = shift(zv_ref[...], prow(P_MU_V), 2)
    sl = shift(zl_ref[...], mul_ref[0:1, :], 3)

    bi = lax.broadcasted_iota(jnp.int32, (LANES, LANES), 0)
    bj = lax.broadcasted_iota(jnp.int32, (LANES, LANES), 1)
    same_head = jnp.where((bi >> 6) == (bj >> 6), 1.0, 0.0).astype(bf16)
    seg = lambda x: _dot_sel(x, same_head)

    u = prow(P_W0) + _dot_x3(jnp.tanh(sl), w2_ref[...])
    ld = -EXP_NEG_HALF * _sigmoid(u)
    av = _sigmoid(prow(P_A0) + _dot_x3(sl, a2_ref[...]))

    kk = k * prow(P_KK)
    kk = kk * lax.rsqrt(jnp.maximum(seg(kk * kk), 1e-24))
    k2 = k * (1.0 + (av - 1.0) * prow(P_KA))

    ti = lax.broadcasted_iota(jnp.int32, (tb, tb), 0)
    tj = lax.broadcasted_iota(jnp.int32, (tb, tb), 1)
    tri = jnp.where(((ti >> 6) == (tj >> 6)) & (tj <= ti), 1.0, 0.0).astype(bf16)
    cum = _sel_dot(tri, ld)

    e_pos = jnp.exp(cum)
    e_neg = jnp.exp(-cum)
    a_t = -kk * jnp.exp(cum - ld)
    b_t = kk * av * e_neg
    k_t = k2 * e_neg
    r_t = r * e_pos

    ri = lax.broadcasted_iota(jnp.int32, (LANES, LANES), 0)
    rj = lax.broadcasted_iota(jnp.int32, (LANES, LANES), 1)
    strict_lower = rj < ri
    eye = jnp.where(ri == rj, 1.0, 0.0)
    qi = lax.broadcasted_iota(jnp.int32, (CHUNK, LANES), 0)
    qj = lax.broadcasted_iota(jnp.int32, (CHUNK, LANES), 1)
    incl_lower = (qj & (CHUNK - 1)) <= qi
    chunk_head0 = qj < HEAD

    def stack(x):
        return jnp.concatenate([jnp.where(chunk_head0, x, 0.0), jnp.where(chunk_head0, 0.0, x)], axis=0)

    def level_mask(b):
        sh = b.bit_length()
        return ((ri >> sh) == (rj >> sh)) & ((ri & (2 * b - 1)) >= b) & ((rj & (2 * b - 1)) < b)

    s_state = s_ref[...]
    ys = []
    for c in range(tb // CHUNK):
        rows = slice(c * CHUNK, (c + 1) * CHUNK)
        cend = cum[(c + 1) * CHUNK - 1:(c + 1) * CHUNK, :]
        dec = jnp.exp(cend - cum[rows])
        am = stack(a_t[rows])
        vm = stack(v[rows])
        x = _dot_nt(jnp.concatenate([am, r_t[rows]], axis=0).astype(bf16),
                    jnp.concatenate([stack(b_t[rows]), stack(k_t[rows])], axis=0).astype(bf16))
        l_ab = jnp.where(strict_lower, x[0:LANES, 0:LANES], 0.0)
        l_ak = jnp.where(strict_lower, x[0:LANES, LANES:], 0.0)
        l_rb = jnp.where(incl_lower, x[LANES:, 0:LANES], 0.0)
        l_rk = jnp.where(incl_lower, x[LANES:, LANES:], 0.0)

        inv = eye + jnp.where(level_mask(1), l_ab, 0.0)
        b = 2
        while b < CHUNK:
            y_ = _dot(jnp.where(level_mask(b), l_ab, 0.0).astype(bf16), inv.astype(bf16))
            inv = inv + _dot(inv.astype(bf16), y_.astype(bf16))
            b *= 2

        g = jnp.concatenate([am, _dot(l_ak.astype(bf16), vm.astype(bf16))], axis=1)
        tg = _dot(inv.astype(bf16), g.astype(bf16))
        w_m, u_v = tg[:, 0:LANES], tg[:, LANES:]

        s_bf = s_state.astype(bf16)
        u_s = _dot_nt(w_m.astype(bf16), s_bf) + u_v
        uv = jnp.concatenate([u_s, vm], axis=0).astype(bf16)
        ys.append(_dot_nt(r_t[rows].astype(bf16), s_bf)
                  + _dot(jnp.concatenate([l_rb, l_rk], axis=1).astype(bf16), uv))
        bk = jnp.concatenate([stack(kk[rows] * av[rows] * dec), stack(k2[rows] * dec)], axis=0).astype(bf16)
        s_state = s_state * jnp.exp(cend) + _dot_tn(uv, bk)
    s_ref[...] = s_state

    y = jnp.concatenate(ys, axis=0)
    mean = seg(y) * (1.0 / HEAD)
    d = y - mean
    var = seg(d * d) * (1.0 / HEAD)
    yn = d * lax.rsqrt(var + GN_EPS) * prow(P_LNW) + prow(P_LNB)
    bonus = seg(r * k2 * prow(P_RK)) * v
    g = zg_ref[...]
    o_ref[...] = ((yn + bonus) * g * _sigmoid(g)).astype(o_ref.dtype)


def _rwkv(z, par, mul, w2p, a2p, *, tb):
    T = z.shape[0]
    nb = WIDTH // LANES
    zspec = lambda off: pl.BlockSpec((tb, LANES), lambda g, t: (t, off + g))
    return pl.pallas_call(
        functools.partial(_rwkv_kernel, tb=tb),
        grid=(NPAIRS, T // tb),
        in_specs=[
            zspec(0), zspec(nb), zspec(2 * nb), zspec(3 * nb),
            pl.BlockSpec((tb, LANES), lambda g, t: (t, Z_LORA // LANES)),
            pl.BlockSpec((16, LANES), lambda g, t: (0, g)),
            pl.BlockSpec((8, LANES), lambda g, t: (0, 0)),
            pl.BlockSpec((LANES, LANES), lambda g, t: (0, g)),
            pl.BlockSpec((LANES, LANES), lambda g, t: (0, g)),
        ],
        out_specs=pl.BlockSpec((tb, LANES), lambda g, t: (t, g)),
        out_shape=jax.ShapeDtypeStruct((T, WIDTH), bf16),
        scratch_shapes=[pltpu.VMEM((LANES, LANES), f32), pltpu.VMEM((4, 8, LANES), f32)],
        compiler_params=pltpu.CompilerParams(
            dimension_semantics=("parallel", "arbitrary"), vmem_limit_bytes=VMEM_LIMIT),
        name="rwkv7",
    )(z, z, z, z, z, par, mul, w2p, a2p)


def _out_kernel(x_ref, yr_ref, yf_ref, p_ref, wo_ref, wg_ref, wp_ref, g1_ref, g2_ref, o_ref):
    def rms(t, g):
        return t * lax.rsqrt(jnp.mean(t * t, axis=-1, keepdims=True) + RMS_EPS) * g

    m = _dot(yr_ref[...], wo_ref[0:WIDTH, :]) + _dot(yf_ref[...], wo_ref[WIDTH:, :])
    x1 = x_ref[...] + rms(m, g1_ref[...])
    gate = _sigmoid(_dot(rms(x1, g2_ref[...]).astype(bf16), wg_ref[...]))
    o_ref[...] = x1 + gate * _dot(p_ref[...].astype(bf16), wp_ref[...])


def _out_proj(x, yr, yf, p, wo, wg, wp, g1, g2, *, tm):
    T = x.shape[0]
    const = lambda shape: pl.BlockSpec(shape, lambda i: (0, 0))
    return pl.pallas_call(
        _out_kernel,
        grid=(T // tm,),
        in_specs=[
            pl.BlockSpec((tm, D_MODEL), lambda i: (i, 0)),
            pl.BlockSpec((tm, WIDTH), lambda i: (i, 0)),
            pl.BlockSpec((tm, WIDTH), lambda i: (i, 0)),
            pl.BlockSpec((tm, D_PLE), lambda i: (i, 0)),
            const((D_MODEL, D_MODEL)), const((D_MODEL, D_MODEL)), const((D_PLE, D_MODEL)),
            const((1, D_MODEL)), const((1, D_MODEL)),
        ],
        out_specs=pl.BlockSpec((tm, D_MODEL), lambda i: (i, 0)),
        out_shape=jax.ShapeDtypeStruct((T, D_MODEL), f32),
        compiler_params=pltpu.CompilerParams(
            dimension_semantics=("parallel",), vmem_limit_bytes=VMEM_LIMIT),
        name="out_proj",
    )(x, yr, yf, p, wo, wg, wp, g1, g2)


def _layer(x, p, pre_g, w_in, mu_r, mu_k, mu_v, mu_w, mu_a, w0, w2, a0, a2, k_k, k_a, r_k, ln_w, ln_b,
           b_f, q_g, k_g, w_out, post_g, ple_g, w_gate, w_ple):
    T = x.shape[0]
    rw_end = 4 * WIDTH
    lora_end = rw_end + 2 * LORA
    fx_end = lora_end + 4 * WIDTH
    w = jnp.concatenate([
        w_in[:, :rw_end], w_in[:, lora_end:fx_end], w_in[:, rw_end:lora_end], w_in[:, fx_end:],
        jnp.zeros((D_MODEL, Z_COLS - w_in.shape[1]), w_in.dtype)], axis=1).astype(bf16)
    z = _in_proj(x, pre_g.reshape(1, D_MODEL), w, tm=min(1024, T), tn=1408)

    gq = jnp.tile(q_g, NHEADS).reshape(1, WIDTH)
    gk = jnp.tile(k_g, NHEADS).reshape(1, WIDTH)
    bfp = jnp.pad(b_f, (0, LANES - NHEADS)).reshape(1, LANES)
    qa, ka, vt = _fox_prep(z, gq, gk, bfp, tb=min(256, T))
    y_fx = _attention(qa, ka, vt, z, tq=min(512, T))

    par = jnp.stack([mu_r, mu_k, mu_v, w0, a0, k_k, k_a, r_k.reshape(WIDTH), ln_w, ln_b])
    par = jnp.pad(par, ((0, 16 - par.shape[0]), (0, 0)))
    mul = jnp.broadcast_to(jnp.concatenate([mu_w, mu_a]).reshape(1, LANES), (8, LANES))
    zeros = jnp.zeros((LORA, WIDTH), f32)
    w2p = jnp.concatenate([w2, zeros], axis=0)
    a2p = jnp.concatenate([zeros, a2], axis=0)
    y_rw = _rwkv(z, par, mul, w2p, a2p, tb=min(256, T))

    return _out_proj(x, y_rw, y_fx, p, w_out.astype(bf16), w_gate.astype(bf16), w_ple.astype(bf16),
                     post_g.reshape(1, D_MODEL), ple_g.reshape(1, D_MODEL), tm=min(256, T))


def kernel(x, p, pre_norm_g, w_in, rw_mu_r, rw_mu_k, rw_mu_v, rw_mu_w, rw_mu_a, rw_w0, rw_w2, rw_a0, rw_a2,
           rw_k_k, rw_k_a, rw_r_k, rw_ln_w, rw_ln_b, fx_b_f, fx_q_g, fx_k_g, w_out, post_norm_g, ple_norm_g,
           w_ple_gate, w_ple):
    B = x.shape[0]
    outs = []
    for b in range(B):
        xb = x[b]
        for i in range(p.shape[0]):
            xb = _layer(xb, p[i, b], pre_norm_g[i], w_in[i], rw_mu_r[i], rw_mu_k[i], rw_mu_v[i], rw_mu_w[i],
                        rw_mu_a[i], rw_w0[i], rw_w2[i], rw_a0[i], rw_a2[i], rw_k_k[i], rw_k_a[i], rw_r_k[i],
                        rw_ln_w[i], rw_ln_b[i], fx_b_f[i], fx_q_g[i], fx_k_g[i], w_out[i], post_norm_g[i],
                        ple_norm_g[i], w_ple_gate[i], w_ple[i])
        outs.append(xb)
    return jnp.stack(outs)
```

```python
import functools

import numpy as np
import jax
import jax.numpy as jnp
from jax import lax
from jax.experimental import pallas as pl
from jax.experimental.pallas import tpu as pltpu

f32 = jnp.float32
bf16 = jnp.bfloat16

D_MODEL = 2048
D_PLE = 256
WIDTH = 1024
HEAD = 64
NHEADS = 16
NPAIRS = NHEADS // 2
LORA = 64
LANES = 128
RMS_EPS = 1e-6
GN_EPS = 64e-5
CHUNK = 64
LOG2E = 1.4426950408889634
EXP_NEG_HALF = 0.6065306597126334
NEG_BIG = -1e30

Z_RW = 0
Z_FX = 4096
Z_LORA = 8192
Z_FF = 8320
Z_COLS = 8448
VT_ROWS = 80
Q_COL = 256

VMEM_LIMIT = 56 * 1024 * 1024


def _dot(a, b):
    return jnp.dot(a, b, preferred_element_type=f32)


def _dot_nt(a, b):
    return lax.dot_general(a, b, (((1,), (1,)), ((), ())), preferred_element_type=f32)


def _dot_tn(a, b):
    return lax.dot_general(a, b, (((0,), (0,)), ((), ())), preferred_element_type=f32)


def _split2(x):
    hi = x.astype(bf16)
    lo = (x - hi.astype(f32)).astype(bf16)
    return hi, lo


def _split3(x):
    hi = x.astype(bf16)
    r = x - hi.astype(f32)
    mid = r.astype(bf16)
    lo = (r - mid.astype(f32)).astype(bf16)
    return hi, mid, lo


def _dot_sel(x, sel):
    hi, mid, lo = _split3(x)
    return _dot(hi, sel) + _dot(mid, sel) + _dot(lo, sel)


def _sel_dot(sel, x):
    hi, mid, lo = _split3(x)
    return _dot(sel, hi) + _dot(sel, mid) + _dot(sel, lo)


def _dot_x3(a, b):
    ah, al = _split2(a)
    bh, bl = _split2(b)
    return _dot(ah, bh) + _dot(al, bh) + _dot(ah, bl)


def _sigmoid(x):
    return 1.0 / (1.0 + jnp.exp(-x))


def _in_proj_kernel(x_ref, g_ref, w_ref, z_ref, h_ref):
    @pl.when(pl.program_id(1) == 0)
    def _():
        x = x_ref[...]
        ms = jnp.mean(x * x, axis=-1, keepdims=True)
        h_ref[...] = (x * lax.rsqrt(ms + RMS_EPS) * g_ref[...]).astype(bf16)

    z_ref[...] = _dot(h_ref[...], w_ref[...])


def _in_proj(x, g, w, *, tm, tn):
    T = x.shape[0]
    return pl.pallas_call(
        _in_proj_kernel,
        grid=(T // tm, Z_COLS // tn),
        in_specs=[
            pl.BlockSpec((tm, D_MODEL), lambda i, j: (i, 0)),
            pl.BlockSpec((1, D_MODEL), lambda i, j: (0, 0)),
            pl.BlockSpec((D_MODEL, tn), lambda i, j: (0, j)),
        ],
        out_specs=pl.BlockSpec((tm, tn), lambda i, j: (i, j)),
        out_shape=jax.ShapeDtypeStruct((T, Z_COLS), f32),
        scratch_shapes=[pltpu.VMEM((tm, D_MODEL), bf16)],
        compiler_params=pltpu.CompilerParams(
            dimension_semantics=("parallel", "arbitrary"), vmem_limit_bytes=VMEM_LIMIT),
        name="in_proj",
    )(x, g, w)


def _fox_prep_kernel(fq_ref, fk_ref, fv_ref, ff_ref, gq_ref, gk_ref, bf_ref,
                     qt_ref, ka_ref, vt_ref, carry_ref, *, tb):
    @pl.when(pl.program_id(0) == 0)
    def _():
        carry_ref[...] = jnp.zeros_like(carry_ref)

    ci = lax.broadcasted_iota(jnp.int32, (WIDTH, LANES), 0)
    hi_ = lax.broadcasted_iota(jnp.int32, (WIDTH, LANES), 1)
    ind = jnp.where((ci >> 6) == hi_, 1.0, 0.0).astype(bf16)
    hj = lax.broadcasted_iota(jnp.int32, (LANES, WIDTH), 0)
    cj = lax.broadcasted_iota(jnp.int32, (LANES, WIDTH), 1)
    ind_t = jnp.where((cj >> 6) == hj, 1.0, 0.0).astype(bf16)

    def head_rms(x, gain):
        ssq = _dot_sel(x * x, ind)
        rinv = lax.rsqrt(ssq * (1.0 / HEAD) + RMS_EPS)
        return x * _dot_sel(rinv, ind_t) * gain

    qn = head_rms(fq_ref[...], gq_ref[...]) * (HEAD ** -0.5 * LOG2E)
    kn = head_rms(fk_ref[...], gk_ref[...])

    xf = ff_ref[...] + bf_ref[...]
    logf = jnp.minimum(xf, 0.0) - jnp.log(1.0 + jnp.exp(-jnp.abs(xf)))
    ri = lax.broadcasted_iota(jnp.int32, (tb, tb), 0)
    rj = lax.broadcasted_iota(jnp.int32, (tb, tb), 1)
    tri = jnp.where(rj <= ri, 1.0, 0.0).astype(bf16)
    cum = _sel_dot(tri, logf) + carry_ref[0:1, :]
    carry_ref[...] = jnp.broadcast_to(cum[tb - 1:tb, :], carry_ref.shape)
    cum2 = cum * LOG2E
    c_hi, c_mid, c_lo = (t.astype(f32) for t in _split3(cum2))

    lane = lax.broadcasted_iota(jnp.int32, (tb, LANES), 1)
    low = lane < HEAD
    vt = fv_ref[...].T
    vpad = jnp.concatenate(
        [jnp.ones((1, tb), f32), jnp.zeros((VT_ROWS - HEAD - 1, tb), f32)], axis=0)
    for h in range(NHEADS):
        col = slice((h // 2) * LANES, (h // 2 + 1) * LANES)
        qc, kc = qn[:, col], kn[:, col]
        if h % 2 == 1:
            qc, kc = pltpu.roll(qc, HEAD, axis=1), pltpu.roll(kc, HEAD, axis=1)
        fh, fm, fl = c_hi[:, h:h + 1], c_mid[:, h:h + 1], c_lo[:, h:h + 1]
        eq = jnp.where(lane == HEAD, fh, jnp.where(lane == HEAD + 1, fm, jnp.where(lane == HEAD + 2, fl,
             jnp.where(lane < HEAD + 6, 1.0, 0.0))))
        ek = jnp.where(lane == HEAD + 3, -fh, jnp.where(lane == HEAD + 4, -fm, jnp.where(lane == HEAD + 5, -fl,
             jnp.where(lane < HEAD + 3, 1.0, 0.0))))
        qt_ref[h] = jnp.where(low, qc, eq).T.astype(bf16)
        ka_ref[h] = jnp.where(low, kc, ek).astype(bf16)
        vt_ref[h] = jnp.concatenate([vt[h * HEAD:(h + 1) * HEAD, :], vpad], axis=0).astype(bf16)


def _fox_prep(z, gq, gk, bfp, *, tb):
    T = z.shape[0]
    wb = Z_FX // WIDTH
    return pl.pallas_call(
        functools.partial(_fox_prep_kernel, tb=tb),
        grid=(T // tb,),
        in_specs=[
            pl.BlockSpec((tb, WIDTH), lambda i: (i, wb)),
            pl.BlockSpec((tb, WIDTH), lambda i: (i, wb + 1)),
            pl.BlockSpec((tb, WIDTH), lambda i: (i, wb + 2)),
            pl.BlockSpec((tb, LANES), lambda i: (i, Z_FF // LANES)),
            pl.BlockSpec((1, WIDTH), lambda i: (0, 0)),
            pl.BlockSpec((1, WIDTH), lambda i: (0, 0)),
            pl.BlockSpec((1, LANES), lambda i: (0, 0)),
        ],
        out_specs=[
            pl.BlockSpec((NHEADS, LANES, tb), lambda i: (0, 0, i)),
            pl.BlockSpec((NHEADS, tb, LANES), lambda i: (0, i, 0)),
            pl.BlockSpec((NHEADS, VT_ROWS, tb), lambda i: (0, 0, i)),
        ],
        out_shape=[
            jax.ShapeDtypeStruct((NHEADS, LANES, T), bf16),
            jax.ShapeDtypeStruct((NHEADS, T, LANES), bf16),
            jax.ShapeDtypeStruct((NHEADS, VT_ROWS, T), bf16),
        ],
        scratch_shapes=[pltpu.VMEM((8, LANES), f32)],
        compiler_params=pltpu.CompilerParams(
            dimension_semantics=("arbitrary",), vmem_limit_bytes=VMEM_LIMIT),
        name="fox_prep",
    )(z, z, z, z, gq, gk, bfp)


def _attn_kernel(qi_ref, kj_ref, qt_ref, ka_ref, vt_ref, fg_ref, o_ref, m_ref, acc_ref, *, tq):
    s = pl.program_id(1)
    i = qi_ref[s]
    j = kj_ref[s]

    @pl.when(j == 0)
    def _():
        m_ref[...] = jnp.full(m_ref.shape, NEG_BIG, f32)
        acc_ref[...] = jnp.zeros_like(acc_ref)

    def update(diag):
        chains = [(hh, qs * Q_COL) for qs in range(tq // Q_COL) for hh in range(2)]

        def n_keys(q_lo):
            return min(tq, q_lo + Q_COL) if diag else tq

        def scores(hh, q_lo):
            st = _dot(ka_ref[hh, 0:n_keys(q_lo), :], qt_ref[hh, :, q_lo:q_lo + Q_COL])
            if diag:
                kpos = lax.broadcasted_iota(jnp.int32, st.shape, 0)
                qpos = q_lo + lax.broadcasted_iota(jnp.int32, st.shape, 1)
                st = jnp.where(kpos <= qpos, st, NEG_BIG)
            return st

        def consume(hh, q_lo, st):
            qsl = slice(q_lo, q_lo + Q_COL)
            m_old = m_ref[hh, :, qsl]
            m_new = jnp.maximum(m_old, jnp.max(st, axis=0, keepdims=True))
            m_ref[hh, :, qsl] = m_new
            p = jnp.exp2(st - m_new).astype(bf16)
            acc_ref[hh, :, qsl] = (jnp.exp2(m_old - m_new) * acc_ref[hh, :, qsl]
                                   + _dot(vt_ref[hh, :, 0:n_keys(q_lo)], p))

        st_next = scores(*chains[0])
        for n, chain in enumerate(chains):
            st = st_next
            if n + 1 < len(chains):
                st_next = scores(*chains[n + 1])
            consume(*chain, st)

    @pl.when(j < i)
    def _():
        update(False)

    @pl.when(j == i)
    def _():
        update(True)
        outs = []
        for hh in range(2):
            acc = acc_ref[hh]
            outs.append(acc[0:HEAD, :] / acc[HEAD:HEAD + 1, :])
        y = jnp.concatenate(outs, axis=0).T
        g = fg_ref[...]
        o_ref[...] = (y * g * _sigmoid(g)).astype(o_ref.dtype)


def _attention(qt, ka, vt, z, *, tq):
    T = ka.shape[1]
    nq = T // tq
    qi = np.concatenate([np.full(i + 1, i, np.int32) for i in range(nq)])
    kj = np.concatenate([np.arange(i + 1, dtype=np.int32) for i in range(nq)])
    fg_col = (Z_FX + 3 * WIDTH) // LANES
    grid_spec = pltpu.PrefetchScalarGridSpec(
        num_scalar_prefetch=2,
        grid=(NPAIRS, len(qi)),
        in_specs=[
            pl.BlockSpec((2, LANES, tq), lambda g, s, qi, kj: (g, 0, qi[s])),
            pl.BlockSpec((2, tq, LANES), lambda g, s, qi, kj: (g, kj[s], 0)),
            pl.BlockSpec((2, VT_ROWS, tq), lambda g, s, qi, kj: (g, 0, kj[s])),
            pl.BlockSpec((tq, LANES), lambda g, s, qi, kj: (qi[s], fg_col + g)),
        ],
        out_specs=pl.BlockSpec((tq, LANES), lambda g, s, qi, kj: (qi[s], g)),
        scratch_shapes=[pltpu.VMEM((2, 1, tq), f32), pltpu.VMEM((2, VT_ROWS, tq), f32)],
    )
    return pl.pallas_call(
        functools.partial(_attn_kernel, tq=tq),
        grid_spec=grid_spec,
        out_shape=jax.ShapeDtypeStruct((T, WIDTH), bf16),
        compiler_params=pltpu.CompilerParams(
            dimension_semantics=("parallel", "arbitrary"), vmem_limit_bytes=VMEM_LIMIT),
        name="fox_attention",
    )(jnp.asarray(qi), jnp.asarray(kj), qt, ka, vt, z)


P_MU_R, P_MU_K, P_MU_V, P_W0, P_A0, P_KK, P_KA, P_RK, P_LNW, P_LNB = range(10)


def _rwkv_kernel(zr_ref, zk_ref, zv_ref, zg_ref, zl_ref, par_ref, mul_ref, w2_ref, a2_ref,
                 o_ref, s_ref, prev_ref, prevl_ref, *, tb, npair):
    width = npair * LANES

    @pl.when(pl.program_id(1) == 0)
    def _():
        s_ref[...] = jnp.zeros_like(s_ref)
        prev_ref[...] = jnp.zeros_like(prev_ref)
        prevl_ref[...] = jnp.zeros_like(prevl_ref)

    par = par_ref[...]
    prow = lambda k: par[k:k + 1, :]

    def shift(x, mu, last_ref):
        row = lax.broadcasted_iota(jnp.int32, x.shape, 0)
        xp = pltpu.roll(x, 1, axis=0)
        xp = jnp.where(row == 0, last_ref[7:8, :], xp)
        last_ref[...] = x[tb - 8:tb, :]
        return x + (xp - x) * mu

    r = shift(zr_ref[...], prow(P_MU_R), prev_ref.at[0])
    k = shift(zk_ref[...], prow(P_MU_K), prev_ref.at[1])
    v = shift(zv_ref[...], prow(P_MU_V), prev_ref.at[2])
    sl = shift(zl_ref[...], mul_ref[0:1, :], prevl_ref)

    bi = lax.broadcasted_iota(jnp.int32, (width, width), 0)
    bj = lax.broadcasted_iota(jnp.int32, (width, width), 1)
    same_head = jnp.where((bi >> 6) == (bj >> 6), 1.0, 0.0).astype(bf16)
    seg = lambda x: _dot_sel(x, same_head)

    u = prow(P_W0) + _dot_x3(jnp.tanh(sl), w2_ref[...])
    ld = -EXP_NEG_HALF * _sigmoid(u)
    av = _sigmoid(prow(P_A0) + _dot_x3(sl, a2_ref[...]))

    kk = k * prow(P_KK)
    kk = kk * lax.rsqrt(jnp.maximum(seg(kk * kk), 1e-24))
    k2 = k * (1.0 + (av - 1.0) * prow(P_KA))

    ti = lax.broadcasted_iota(jnp.int32, (tb, tb), 0)
    tj = lax.broadcasted_iota(jnp.int32, (tb, tb), 1)
    tri = jnp.where(((ti >> 6) == (tj >> 6)) & (tj <= ti), 1.0, 0.0).astype(bf16)
    cum = _sel_dot(tri, ld)

    e_pos = jnp.exp(cum)
    e_neg = jnp.exp(-cum)
    a_t = -kk * jnp.exp(cum - ld)
    b_t = kk * av * e_neg
    k_t = k2 * e_neg
    r_t = r * e_pos

    ri = lax.broadcasted_iota(jnp.int32, (LANES, LANES), 0)
    rj = lax.broadcasted_iota(jnp.int32, (LANES, LANES), 1)
    strict_lower = rj < ri
    eye = jnp.where(ri == rj, 1.0, 0.0)
    qi = lax.broadcasted_iota(jnp.int32, (CHUNK, LANES), 0)
    qj = lax.broadcasted_iota(jnp.int32, (CHUNK, LANES), 1)
    incl_lower = (qj & (CHUNK - 1)) <= qi
    chunk_head0 = qj < HEAD

    def stack(x):
        return jnp.concatenate([jnp.where(chunk_head0, x, 0.0), jnp.where(chunk_head0, 0.0, x)], axis=0)

    def level_mask(b):
        sh = b.bit_length()
        return ((ri >> sh) == (rj >> sh)) & ((ri & (2 * b - 1)) >= b) & ((rj & (2 * b - 1)) < b)

    nchunk = tb // CHUNK
    units = [(pr, c) for pr in range(npair) for c in range(nchunk)]
    rows = lambda c: slice(c * CHUNK, (c + 1) * CHUNK)
    lanes = lambda pr: slice(pr * LANES, (pr + 1) * LANES)
    cut = lambda x, u: x[rows(u[1]), lanes(u[0])]
    cend = {u: cum[(u[1] + 1) * CHUNK - 1:(u[1] + 1) * CHUNK, lanes(u[0])] for u in units}

    am = {u: stack(cut(a_t, u)) for u in units}
    vm = {u: stack(cut(v, u)) for u in units}
    x = {u: _dot_nt(jnp.concatenate([am[u], cut(r_t, u)], axis=0).astype(bf16),
                    jnp.concatenate([stack(cut(b_t, u)), stack(cut(k_t, u))], axis=0).astype(bf16))
         for u in units}
    l_ab = {u: jnp.where(strict_lower, x[u][0:LANES, 0:LANES], 0.0) for u in units}
    l_ak = {u: jnp.where(strict_lower, x[u][0:LANES, LANES:], 0.0) for u in units}
    l_r = {u: jnp.where(jnp.concatenate([incl_lower, incl_lower], axis=1), x[u][LANES:, :], 0.0).astype(bf16)
           for u in units}

    inv = {u: eye + jnp.where(level_mask(1), l_ab[u], 0.0) for u in units}
    b = 2
    while b < CHUNK:
        mask = level_mask(b)
        y_ = {u: _dot(jnp.where(mask, l_ab[u], 0.0).astype(bf16), inv[u].astype(bf16)) for u in units}
        inv = {u: inv[u] + _dot(inv[u].astype(bf16), y_[u].astype(bf16)) for u in units}
        b *= 2

    lv = {u: _dot(l_ak[u].astype(bf16), vm[u].astype(bf16)) for u in units}
    tg = {u: _dot(inv[u].astype(bf16), jnp.concatenate([am[u], lv[u]], axis=1).astype(bf16)) for u in units}
    zeros = jnp.zeros((LANES, LANES), f32)
    ly = {u: _dot(l_r[u], jnp.concatenate(
              [tg[u], jnp.concatenate([zeros, vm[u]], axis=1)], axis=0).astype(bf16)) for u in units}
    bh = {u: stack(cut(kk, u) * cut(av, u) * jnp.exp(cend[u] - cut(cum, u))).astype(bf16) for u in units}
    kh = {u: stack(cut(k2, u) * jnp.exp(cend[u] - cut(cum, u))).astype(bf16) for u in units}
    ag = {u: _dot_tn(tg[u].astype(bf16), bh[u]) for u in units}
    vk = {u: _dot_tn(vm[u].astype(bf16), kh[u]) for u in units}
    q_eff = {u: (cut(r_t, u) + ly[u][:, 0:LANES]).astype(bf16) for u in units}
    a_eff = {u: ag[u][0:LANES, :].astype(bf16) for u in units}
    g_eff = {u: ag[u][LANES:, :] + vk[u] for u in units}

    ys = [[None] * npair for _ in range(nchunk)]
    state = [s_ref[pr] for pr in range(npair)]
    for c in range(nchunk):
        for pr in range(npair):
            u = (pr, c)
            s_bf = state[pr].astype(bf16)
            ys[c][pr] = _dot_nt(q_eff[u], s_bf) + ly[u][:, LANES:]
            state[pr] = state[pr] * jnp.exp(cend[u]) + _dot(s_bf, a_eff[u]) + g_eff[u]
    for pr in range(npair):
        s_ref[pr] = state[pr]

    y = jnp.concatenate([jnp.concatenate(yc, axis=1) for yc in ys], axis=0)
    mean = seg(y) * (1.0 / HEAD)
    d = y - mean
    var = seg(d * d) * (1.0 / HEAD)
    yn = d * lax.rsqrt(var + GN_EPS) * prow(P_LNW) + prow(P_LNB)
    bonus = seg(r * k2 * prow(P_RK)) * v
    g = zg_ref[...]
    o_ref[...] = ((yn + bonus) * g * _sigmoid(g)).astype(o_ref.dtype)


def _rwkv(z, par, mul, w2p, a2p, *, tb, npair):
    T = z.shape[0]
    width = npair * LANES
    nb = WIDTH // width
    zspec = lambda off: pl.BlockSpec((tb, width), lambda g, t: (t, off + g))
    return pl.pallas_call(
        functools.partial(_rwkv_kernel, tb=tb, npair=npair),
        grid=(nb, T // tb),
        in_specs=[
            zspec(0), zspec(nb), zspec(2 * nb), zspec(3 * nb),
            pl.BlockSpec((tb, LANES), lambda g, t: (t, Z_LORA // LANES)),
            pl.BlockSpec((16, width), lambda g, t: (0, g)),
            pl.BlockSpec((8, LANES), lambda g, t: (0, 0)),
            pl.BlockSpec((LANES, width), lambda g, t: (0, g)),
            pl.BlockSpec((LANES, width), lambda g, t: (0, g)),
        ],
        out_specs=pl.BlockSpec((tb, width), lambda g, t: (t, g)),
        out_shape=jax.ShapeDtypeStruct((T, WIDTH), bf16),
        scratch_shapes=[pltpu.VMEM((npair, LANES, LANES), f32), pltpu.VMEM((3, 8, width), f32),
                        pltpu.VMEM((8, LANES), f32)],
        compiler_params=pltpu.CompilerParams(
            dimension_semantics=("parallel", "arbitrary"), vmem_limit_bytes=VMEM_LIMIT),
        name="rwkv7",
    )(z, z, z, z, z, par, mul, w2p, a2p)


def _out_kernel(x_ref, yr_ref, yf_ref, p_ref, wo_ref, wg_ref, wp_ref, g1_ref, g2_ref, o_ref):
    def rms(t, g):
        return t * lax.rsqrt(jnp.mean(t * t, axis=-1, keepdims=True) + RMS_EPS) * g

    m = _dot(yr_ref[...], wo_ref[0:WIDTH, :]) + _dot(yf_ref[...], wo_ref[WIDTH:, :])
    x1 = x_ref[...] + rms(m, g1_ref[...])
    gate = _sigmoid(_dot(rms(x1, g2_ref[...]).astype(bf16), wg_ref[...]))
    o_ref[...] = x1 + gate * _dot(p_ref[...].astype(bf16), wp_ref[...])


def _out_proj(x, yr, yf, p, wo, wg, wp, g1, g2, *, tm):
    T = x.shape[0]
    const = lambda shape: pl.BlockSpec(shape, lambda i: (0, 0))
    return pl.pallas_call(
        _out_kernel,
        grid=(T // tm,),
        in_specs=[
            pl.BlockSpec((tm, D_MODEL), lambda i: (i, 0)),
            pl.BlockSpec((tm, WIDTH), lambda i: (i, 0)),
            pl.BlockSpec((tm, WIDTH), lambda i: (i, 0)),
            pl.BlockSpec((tm, D_PLE), lambda i: (i, 0)),
            const((D_MODEL, D_MODEL)), const((D_MODEL, D_MODEL)), const((D_PLE, D_MODEL)),
            const((1, D_MODEL)), const((1, D_MODEL)),
        ],
        out_specs=pl.BlockSpec((tm, D_MODEL), lambda i: (i, 0)),
        out_shape=jax.ShapeDtypeStruct((T, D_MODEL), f32),
        compiler_params=pltpu.CompilerParams(
            dimension_semantics=("parallel",), vmem_limit_bytes=VMEM_LIMIT),
        name="out_proj",
    )(x, yr, yf, p, wo, wg, wp, g1, g2)


def _layer(x, p, pre_g, w_in, mu_r, mu_k, mu_v, mu_w, mu_a, w0, w2, a0, a2, k_k, k_a, r_k, ln_w, ln_b,
           b_f, q_g, k_g, w_out, post_g, ple_g, w_gate, w_ple):
    T = x.shape[0]
    rw_end = 4 * WIDTH
    lora_end = rw_end + 2 * LORA
    fx_end = lora_end + 4 * WIDTH
    w = jnp.concatenate([
        w_in[:, :rw_end], w_in[:, lora_end:fx_end], w_in[:, rw_end:lora_end], w_in[:, fx_end:],
        jnp.zeros((D_MODEL, Z_COLS - w_in.shape[1]), w_in.dtype)], axis=1).astype(bf16)
    z = _in_proj(x, pre_g.reshape(1, D_MODEL), w, tm=min(1024, T), tn=1408)

    gq = jnp.tile(q_g, NHEADS).reshape(1, WIDTH)
    gk = jnp.tile(k_g, NHEADS).reshape(1, WIDTH)
    bfp = jnp.pad(b_f, (0, LANES - NHEADS)).reshape(1, LANES)
    qt, ka, vt = _fox_prep(z, gq, gk, bfp, tb=min(256, T))
    y_fx = _attention(qt, ka, vt, z, tq=min(512, T))

    par = jnp.stack([mu_r, mu_k, mu_v, w0, a0, k_k, k_a, r_k.reshape(WIDTH), ln_w, ln_b])
    par = jnp.pad(par, ((0, 16 - par.shape[0]), (0, 0)))
    mul = jnp.broadcast_to(jnp.concatenate([mu_w, mu_a]).reshape(1, LANES), (8, LANES))
    zeros = jnp.zeros((LORA, WIDTH), f32)
    w2p = jnp.concatenate([w2, zeros], axis=0)
    a2p = jnp.concatenate([zeros, a2], axis=0)
    y_rw = _rwkv(z, par, mul, w2p, a2p, tb=min(256, T), npair=2)

    return _out_proj(x, y_rw, y_fx, p, w_out.astype(bf16), w_gate.astype(bf16), w_ple.astype(bf16),
                     post_g.reshape(1, D_MODEL), ple_g.reshape(1, D_MODEL), tm=min(256, T))


def kernel(x, p, pre_norm_g, w_in, rw_mu_r, rw_mu_k, rw_mu_v, rw_mu_w, rw_mu_a, rw_w0, rw_w2, rw_a0, rw_a2,
           rw_k_k, rw_k_a, rw_r_k, rw_ln_w, rw_ln_b, fx_b_f, fx_q_g, fx_k_g, w_out, post_norm_g, ple_norm_g,
           w_ple_gate, w_ple):
    B = x.shape[0]
    outs = []
    for b in range(B):
        xb = x[b]
        for i in range(p.shape[0]):
            xb = _layer(xb, p[i, b], pre_norm_g[i], w_in[i], rw_mu_r[i], rw_mu_k[i], rw_mu_v[i], rw_mu_w[i],
                        rw_mu_a[i], rw_w0[i], rw_w2[i], rw_a0[i], rw_a2[i], rw_k_k[i], rw_k_a[i], rw_r_k[i],
                        rw_ln_w[i], rw_ln_b[i], fx_b_f[i], fx_q_g[i], fx_k_g[i], w_out[i], post_norm_g[i],
                        ple_norm_g[i], w_ple_gate[i], w_ple[i])
        outs.append(xb)
    return jnp.stack(outs)
```

```python
import functools

import numpy as np
import jax
import jax.numpy as jnp
from jax import lax
from jax.experimental import pallas as pl
from jax.experimental.pallas import tpu as pltpu

f32 = jnp.float32
bf16 = jnp.bfloat16

D_MODEL = 2048
D_PLE = 256
WIDTH = 1024
HEAD = 64
NHEADS = 16
NPAIRS = NHEADS // 2
LORA = 64
LANES = 128
RMS_EPS = 1e-6
GN_EPS = 64e-5
CHUNK = 64
LOG2E = 1.4426950408889634
EXP_NEG_HALF = 0.6065306597126334
NEG_BIG = -1e30

Z_RW = 0
Z_FX = 4096
Z_LORA = 8192
Z_FF = 8320
Z_COLS = 8448
VT_ROWS = 80
Q_COL = 256
SKIP_LOG2 = 48.0

VMEM_LIMIT = 56 * 1024 * 1024


def _dot(a, b):
    return jnp.dot(a, b, preferred_element_type=f32)


def _dot_nt(a, b):
    return lax.dot_general(a, b, (((1,), (1,)), ((), ())), preferred_element_type=f32)


def _dot_tn(a, b):
    return lax.dot_general(a, b, (((0,), (0,)), ((), ())), preferred_element_type=f32)


def _split2(x):
    hi = x.astype(bf16)
    lo = (x - hi.astype(f32)).astype(bf16)
    return hi, lo


def _split3(x):
    hi = x.astype(bf16)
    r = x - hi.astype(f32)
    mid = r.astype(bf16)
    lo = (r - mid.astype(f32)).astype(bf16)
    return hi, mid, lo


def _dot_sel(x, sel):
    hi, mid, lo = _split3(x)
    return _dot(hi, sel) + _dot(mid, sel) + _dot(lo, sel)


def _sel_dot(sel, x):
    hi, mid, lo = _split3(x)
    return _dot(sel, hi) + _dot(sel, mid) + _dot(sel, lo)


def _dot_x3(a, b):
    ah, al = _split2(a)
    bh, bl = _split2(b)
    return _dot(ah, bh) + _dot(al, bh) + _dot(ah, bl)


def _sigmoid(x):
    return 1.0 / (1.0 + jnp.exp(-x))


def _in_proj_kernel(x_ref, g_ref, w_ref, z_ref, h_ref):
    @pl.when(pl.program_id(1) == 0)
    def _():
        x = x_ref[...]
        ms = jnp.mean(x * x, axis=-1, keepdims=True)
        h_ref[...] = (x * lax.rsqrt(ms + RMS_EPS) * g_ref[...]).astype(bf16)

    z_ref[...] = _dot(h_ref[...], w_ref[...])


def _in_proj(x, g, w, *, tm, tn):
    T = x.shape[0]
    return pl.pallas_call(
        _in_proj_kernel,
        grid=(T // tm, Z_COLS // tn),
        in_specs=[
            pl.BlockSpec((tm, D_MODEL), lambda i, j: (i, 0)),
            pl.BlockSpec((1, D_MODEL), lambda i, j: (0, 0)),
            pl.BlockSpec((D_MODEL, tn), lambda i, j: (0, j)),
        ],
        out_specs=pl.BlockSpec((tm, tn), lambda i, j: (i, j)),
        out_shape=jax.ShapeDtypeStruct((T, Z_COLS), f32),
        scratch_shapes=[pltpu.VMEM((tm, D_MODEL), bf16)],
        compiler_params=pltpu.CompilerParams(
            dimension_semantics=("parallel", "arbitrary"), vmem_limit_bytes=VMEM_LIMIT),
        name="in_proj",
    )(x, g, w)


def _fox_prep_kernel(fq_ref, fk_ref, fv_ref, ff_ref, gq_ref, gk_ref, bf_ref,
                     qt_ref, ka_ref, vt_ref, fs_ref, carry_ref, *, tb):
    @pl.when(pl.program_id(0) == 0)
    def _():
        carry_ref[...] = jnp.zeros_like(carry_ref)

    ci = lax.broadcasted_iota(jnp.int32, (WIDTH, LANES), 0)
    hi_ = lax.broadcasted_iota(jnp.int32, (WIDTH, LANES), 1)
    ind = jnp.where((ci >> 6) == hi_, 1.0, 0.0).astype(bf16)
    hj = lax.broadcasted_iota(jnp.int32, (LANES, WIDTH), 0)
    cj = lax.broadcasted_iota(jnp.int32, (LANES, WIDTH), 1)
    ind_t = jnp.where((cj >> 6) == hj, 1.0, 0.0).astype(bf16)

    def head_rms(x, gain):
        ssq = _dot_sel(x * x, ind)
        rinv = lax.rsqrt(ssq * (1.0 / HEAD) + RMS_EPS)
        return x * _dot_sel(rinv, ind_t) * gain

    qn = head_rms(fq_ref[...], gq_ref[...]) * (HEAD ** -0.5 * LOG2E)
    kn = head_rms(fk_ref[...], gk_ref[...])

    xf = ff_ref[...] + bf_ref[...]
    logf = jnp.minimum(xf, 0.0) - jnp.log(1.0 + jnp.exp(-jnp.abs(xf)))
    ri = lax.broadcasted_iota(jnp.int32, (tb, tb), 0)
    rj = lax.broadcasted_iota(jnp.int32, (tb, tb), 1)
    tri = jnp.where(rj <= ri, 1.0, 0.0).astype(bf16)
    cum = _sel_dot(tri, logf) + carry_ref[0:1, :]
    carry_ref[...] = jnp.broadcast_to(cum[tb - 1:tb, :], carry_ref.shape)
    cum2 = cum * LOG2E
    c_hi, c_mid, c_lo = (t.astype(f32) for t in _split3(cum2))
    srow = lax.broadcasted_iota(jnp.int32, (8, LANES), 0)
    fs_ref[0] = jnp.where(srow == 0, jnp.max(cum2, axis=0, keepdims=True), jnp.min(cum2, axis=0, keepdims=True))

    lane = lax.broadcasted_iota(jnp.int32, (tb, LANES), 1)
    low = lane < HEAD
    vt = fv_ref[...].T
    vpad = jnp.concatenate(
        [jnp.ones((1, tb), f32), jnp.zeros((VT_ROWS - HEAD - 1, tb), f32)], axis=0)
    for h in range(NHEADS):
        col = slice((h // 2) * LANES, (h // 2 + 1) * LANES)
        qc, kc = qn[:, col], kn[:, col]
        if h % 2 == 1:
            qc, kc = pltpu.roll(qc, HEAD, axis=1), pltpu.roll(kc, HEAD, axis=1)
        fh, fm, fl = c_hi[:, h:h + 1], c_mid[:, h:h + 1], c_lo[:, h:h + 1]
        eq = jnp.where(lane == HEAD, fh, jnp.where(lane == HEAD + 1, fm, jnp.where(lane == HEAD + 2, fl,
             jnp.where(lane < HEAD + 6, 1.0, 0.0))))
        ek = jnp.where(lane == HEAD + 3, -fh, jnp.where(lane == HEAD + 4, -fm, jnp.where(lane == HEAD + 5, -fl,
             jnp.where(lane < HEAD + 3, 1.0, 0.0))))
        qt_ref[h] = jnp.where(low, qc, eq).T.astype(bf16)
        ka_ref[h] = jnp.where(low, kc, ek).astype(bf16)
        vt_ref[h] = jnp.concatenate([vt[h * HEAD:(h + 1) * HEAD, :], vpad], axis=0).astype(bf16)


def _fox_prep(z, gq, gk, bfp, *, tb):
    T = z.shape[0]
    wb = Z_FX // WIDTH
    return pl.pallas_call(
        functools.partial(_fox_prep_kernel, tb=tb),
        grid=(T // tb,),
        in_specs=[
            pl.BlockSpec((tb, WIDTH), lambda i: (i, wb)),
            pl.BlockSpec((tb, WIDTH), lambda i: (i, wb + 1)),
            pl.BlockSpec((tb, WIDTH), lambda i: (i, wb + 2)),
            pl.BlockSpec((tb, LANES), lambda i: (i, Z_FF // LANES)),
            pl.BlockSpec((1, WIDTH), lambda i: (0, 0)),
            pl.BlockSpec((1, WIDTH), lambda i: (0, 0)),
            pl.BlockSpec((1, LANES), lambda i: (0, 0)),
        ],
        out_specs=[
            pl.BlockSpec((NHEADS, LANES, tb), lambda i: (0, 0, i)),
            pl.BlockSpec((NHEADS, tb, LANES), lambda i: (0, i, 0)),
            pl.BlockSpec((NHEADS, VT_ROWS, tb), lambda i: (0, 0, i)),
            pl.BlockSpec((1, 8, LANES), lambda i: (i, 0, 0)),
        ],
        out_shape=[
            jax.ShapeDtypeStruct((NHEADS, LANES, T), bf16),
            jax.ShapeDtypeStruct((NHEADS, T, LANES), bf16),
            jax.ShapeDtypeStruct((NHEADS, VT_ROWS, T), bf16),
            jax.ShapeDtypeStruct((T // tb, 8, LANES), f32),
        ],
        scratch_shapes=[pltpu.VMEM((8, LANES), f32)],
        compiler_params=pltpu.CompilerParams(
            dimension_semantics=("arbitrary",), vmem_limit_bytes=VMEM_LIMIT),
        name="fox_prep",
    )(z, z, z, z, gq, gk, bfp)


def _attn_kernel(work_ref, qt_ref, ka_ref, vt_ref, fg_ref, o_ref, m_ref, acc_ref, *, tq, tk):
    code = work_ref[pl.program_id(0) * pl.num_programs(1) + pl.program_id(1)]
    variant = ((code >> 16) & 0xF) - 1
    first = (code >> 20) & 1
    last = (code >> 21) & 1
    valid = (code >> 22) & 1

    @pl.when(first == 1)
    def _():
        m_ref[...] = jnp.full(m_ref.shape, NEG_BIG, f32)
        acc_ref[...] = jnp.zeros_like(acc_ref)

    def update(d):
        def n_keys(q_lo):
            return tk if d < 0 else max(0, min(tk, q_lo + Q_COL - d * tk))

        chains = [(hh, qs * Q_COL) for qs in range(tq // Q_COL) for hh in range(2) if n_keys(qs * Q_COL) > 0]

        def scores(hh, q_lo):
            nk = n_keys(q_lo)
            st = _dot(ka_ref[hh, 0:nk, :], qt_ref[hh, :, q_lo:q_lo + Q_COL])
            if d >= 0 and d * tk + nk - 1 > q_lo:
                kpos = d * tk + lax.broadcasted_iota(jnp.int32, st.shape, 0)
                qpos = q_lo + lax.broadcasted_iota(jnp.int32, st.shape, 1)
                st = jnp.where(kpos <= qpos, st, NEG_BIG)
            return st

        def consume(hh, q_lo, st):
            qsl = slice(q_lo, q_lo + Q_COL)
            m_old = m_ref[hh, :, qsl]
            m_new = jnp.maximum(m_old, jnp.max(st, axis=0, keepdims=True))
            m_ref[hh, :, qsl] = m_new
            p = jnp.exp2(st - m_new).astype(bf16)
            acc_ref[hh, :, qsl] = (jnp.exp2(m_old - m_new) * acc_ref[hh, :, qsl]
                                   + _dot(vt_ref[hh, :, 0:n_keys(q_lo)], p))

        st_next = scores(*chains[0])
        for n, chain in enumerate(chains):
            st = st_next
            if n + 1 < len(chains):
                st_next = scores(*chains[n + 1])
            consume(*chain, st)

    for d in range(-1, tq // tk):
        @pl.when((valid == 1) & (variant == d))
        def _(d=d):
            update(d)

    @pl.when((valid == 1) & (last == 1))
    def _():
        outs = []
        for hh in range(2):
            acc = acc_ref[hh]
            outs.append(acc[0:HEAD, :] / acc[HEAD:HEAD + 1, :])
        y = jnp.concatenate(outs, axis=0).T
        g = fg_ref[...]
        o_ref[...] = (y * g * _sigmoid(g)).astype(o_ref.dtype)


def _attention_work_list(fstat, qk_bound, *, T, tb, tq, tk):
    r = tq // tk
    nq = T // tq
    qi = np.concatenate([np.full(r * (i + 1), i, np.int32) for i in range(nq)])
    kj = np.concatenate([np.arange(r * (i + 1), dtype=np.int32) for i in range(nq)])
    steps = len(qi)
    fmax = fstat[:, 0, :NHEADS]
    fmin = fstat[:, 1, :NHEADS]
    fmax_q = fmax.reshape(nq, tq // tb, NHEADS).max(axis=1)
    fmin_k = fmin.reshape(T // tk, tk // tb, NHEADS).min(axis=1)
    need = (fmax_q[qi] - fmin_k[kj] + 2.0 * qk_bound) > -SKIP_LOG2
    need = (need[:, 0::2] | need[:, 1::2]).T
    count = jnp.sum(need, axis=1, keepdims=True).astype(jnp.int32)
    order = jnp.argsort(jnp.logical_not(need), axis=1, stable=True).astype(jnp.int32)
    pos = jnp.arange(steps, dtype=jnp.int32)[None, :]
    sel = jnp.take_along_axis(order, jnp.minimum(pos, count - 1), axis=1)
    qs, ks = jnp.asarray(qi)[sel], jnp.asarray(kj)[sel]
    variant = jnp.maximum(ks - r * qs, -1)
    first = jnp.concatenate([jnp.ones((NPAIRS, 1), jnp.int32),
                             (qs[:, 1:] != qs[:, :-1]).astype(jnp.int32)], axis=1)
    last = (ks == r * qs + r - 1).astype(jnp.int32)
    valid = (pos < count).astype(jnp.int32)
    code = qs | (ks << 8) | ((variant + 1) << 16) | ((first & valid) << 20) | (last << 21) | (valid << 22)
    return code.reshape(-1).astype(jnp.int32), steps


def _attention(qt, ka, vt, z, fstat, qk_bound, *, tb, tq, tk):
    T = ka.shape[1]
    work, steps = _attention_work_list(fstat, qk_bound, T=T, tb=tb, tq=tq, tk=tk)
    fg_col = (Z_FX + 3 * WIDTH) // LANES
    qblk = lambda g, s, w: w[g * steps + s] & 0xFF
    kblk = lambda g, s, w: (w[g * steps + s] >> 8) & 0xFF
    grid_spec = pltpu.PrefetchScalarGridSpec(
        num_scalar_prefetch=1,
        grid=(NPAIRS, steps),
        in_specs=[
            pl.BlockSpec((2, LANES, tq), lambda g, s, w: (g, 0, qblk(g, s, w))),
            pl.BlockSpec((2, tk, LANES), lambda g, s, w: (g, kblk(g, s, w), 0)),
            pl.BlockSpec((2, VT_ROWS, tk), lambda g, s, w: (g, 0, kblk(g, s, w))),
            pl.BlockSpec((tq, LANES), lambda g, s, w: (qblk(g, s, w), fg_col + g)),
        ],
        out_specs=pl.BlockSpec((tq, LANES), lambda g, s, w: (qblk(g, s, w), g)),
        scratch_shapes=[pltpu.VMEM((2, 1, tq), f32), pltpu.VMEM((2, VT_ROWS, tq), f32)],
    )
    return pl.pallas_call(
        functools.partial(_attn_kernel, tq=tq, tk=tk),
        grid_spec=grid_spec,
        out_shape=jax.ShapeDtypeStruct((T, WIDTH), bf16),
        compiler_params=pltpu.CompilerParams(
            dimension_semantics=("parallel", "arbitrary"), vmem_limit_bytes=VMEM_LIMIT),
        name="fox_attention",
    )(work, qt, ka, vt, z)


P_MU_R, P_MU_K, P_MU_V, P_W0, P_A0, P_KK, P_KA, P_RK, P_LNW, P_LNB = range(10)


def _rwkv_kernel(zr_ref, zk_ref, zv_ref, zg_ref, zl_ref, par_ref, mul_ref, w2_ref, a2_ref,
                 o_ref, s_ref, prev_ref, prevl_ref, *, tb, npair):
    width = npair * LANES

    @pl.when(pl.program_id(1) == 0)
    def _():
        s_ref[...] = jnp.zeros_like(s_ref)
        prev_ref[...] = jnp.zeros_like(prev_ref)
        prevl_ref[...] = jnp.zeros_like(prevl_ref)

    par = par_ref[...]
    prow = lambda k: par[k:k + 1, :]

    def shift(x, mu, last_ref):
        row = lax.broadcasted_iota(jnp.int32, x.shape, 0)
        xp = pltpu.roll(x, 1, axis=0)
        xp = jnp.where(row == 0, last_ref[7:8, :], xp)
        last_ref[...] = x[tb - 8:tb, :]
        return x + (xp - x) * mu

    r = shift(zr_ref[...], prow(P_MU_R), prev_ref.at[0])
    k = shift(zk_ref[...], prow(P_MU_K), prev_ref.at[1])
    v = shift(zv_ref[...], prow(P_MU_V), prev_ref.at[2])
    sl = shift(zl_ref[...], mul_ref[0:1, :], prevl_ref)

    bi = lax.broadcasted_iota(jnp.int32, (width, width), 0)
    bj = lax.broadcasted_iota(jnp.int32, (width, width), 1)
    same_head = jnp.where((bi >> 6) == (bj >> 6), 1.0, 0.0).astype(bf16)
    seg = lambda x: _dot_sel(x, same_head)

    u = prow(P_W0) + _dot_x3(jnp.tanh(sl), w2_ref[...])
    ld = -EXP_NEG_HALF * _sigmoid(u)
    av = _sigmoid(prow(P_A0) + _dot_x3(sl, a2_ref[...]))

    kk = k * prow(P_KK)
    kk = kk * lax.rsqrt(jnp.maximum(seg(kk * kk), 1e-24))
    k2 = k * (1.0 + (av - 1.0) * prow(P_KA))

    ti = lax.broadcasted_iota(jnp.int32, (tb, tb), 0)
    tj = lax.broadcasted_iota(jnp.int32, (tb, tb), 1)
    tri = jnp.where(((ti >> 6) == (tj >> 6)) & (tj <= ti), 1.0, 0.0).astype(bf16)
    cum = _sel_dot(tri, ld)

    e_pos = jnp.exp(cum)
    e_neg = jnp.exp(-cum)
    a_t = -kk * jnp.exp(cum - ld)
    b_t = kk * av * e_neg
    k_t = k2 * e_neg
    r_t = r * e_pos

    ri = lax.broadcasted_iota(jnp.int32, (LANES, LANES), 0)
    rj = lax.broadcasted_iota(jnp.int32, (LANES, LANES), 1)
    strict_lower = rj < ri
    eye = jnp.where(ri == rj, 1.0, 0.0)
    qi = lax.broadcasted_iota(jnp.int32, (CHUNK, LANES), 0)
    qj = lax.broadcasted_iota(jnp.int32, (CHUNK, LANES), 1)
    incl_lower = (qj & (CHUNK - 1)) <= qi
    chunk_head0 = qj < HEAD

    def stack(x):
        return jnp.concatenate([jnp.where(chunk_head0, x, 0.0), jnp.where(chunk_head0, 0.0, x)], axis=0)

    def level_mask(b):
        sh = b.bit_length()
        return ((ri >> sh) == (rj >> sh)) & ((ri & (2 * b - 1)) >= b) & ((rj & (2 * b - 1)) < b)

    nchunk = tb // CHUNK
    units = [(pr, c) for pr in range(npair) for c in range(nchunk)]
    rows = lambda c: slice(c * CHUNK, (c + 1) * CHUNK)
    lanes = lambda pr: slice(pr * LANES, (pr + 1) * LANES)
    cut = lambda x, u: x[rows(u[1]), lanes(u[0])]
    cend = {u: cum[(u[1] + 1) * CHUNK - 1:(u[1] + 1) * CHUNK, lanes(u[0])] for u in units}

    am = {u: stack(cut(a_t, u)) for u in units}
    vm = {u: stack(cut(v, u)) for u in units}
    x = {u: _dot_nt(jnp.concatenate([am[u], cut(r_t, u)], axis=0).astype(bf16),
                    jnp.concatenate([stack(cut(b_t, u)), stack(cut(k_t, u))], axis=0).astype(bf16))
         for u in units}
    l_ab = {u: jnp.where(strict_lower, x[u][0:LANES, 0:LANES], 0.0) for u in units}
    l_ak = {u: jnp.where(strict_lower, x[u][0:LANES, LANES:], 0.0) for u in units}
    l_r = {u: jnp.where(jnp.concatenate([incl_lower, incl_lower], axis=1), x[u][LANES:, :], 0.0).astype(bf16)
           for u in units}

    inv = {u: eye + jnp.where(level_mask(1), l_ab[u], 0.0) for u in units}
    b = 2
    while b < CHUNK:
        mask = level_mask(b)
        y_ = {u: _dot(jnp.where(mask, l_ab[u], 0.0).astype(bf16), inv[u].astype(bf16)) for u in units}
        inv = {u: inv[u] + _dot(inv[u].astype(bf16), y_[u].astype(bf16)) for u in units}
        b *= 2

    lv = {u: _dot(l_ak[u].astype(bf16), vm[u].astype(bf16)) for u in units}
    tg = {u: _dot(inv[u].astype(bf16), jnp.concatenate([am[u], lv[u]], axis=1).astype(bf16)) for u in units}
    zeros = jnp.zeros((LANES, LANES), f32)
    ly = {u: _dot(l_r[u], jnp.concatenate(
              [tg[u], jnp.concatenate([zeros, vm[u]], axis=1)], axis=0).astype(bf16)) for u in units}
    bh = {u: stack(cut(kk, u) * cut(av, u) * jnp.exp(cend[u] - cut(cum, u))).astype(bf16) for u in units}
    kh = {u: stack(cut(k2, u) * jnp.exp(cend[u] - cut(cum, u))).astype(bf16) for u in units}
    ag = {u: _dot_tn(tg[u].astype(bf16), bh[u]) for u in units}
    vk = {u: _dot_tn(vm[u].astype(bf16), kh[u]) for u in units}
    q_eff = {u: (cut(r_t, u) + ly[u][:, 0:LANES]).astype(bf16) for u in units}
    a_eff = {u: ag[u][0:LANES, :].astype(bf16) for u in units}
    g_eff = {u: ag[u][LANES:, :] + vk[u] for u in units}

    ys = [[None] * npair for _ in range(nchunk)]
    state = [s_ref[pr] for pr in range(npair)]
    for c in range(nchunk):
        for pr in range(npair):
            u = (pr, c)
            s_bf = state[pr].astype(bf16)
            ys[c][pr] = _dot_nt(q_eff[u], s_bf) + ly[u][:, LANES:]
            state[pr] = state[pr] * jnp.exp(cend[u]) + _dot(s_bf, a_eff[u]) + g_eff[u]
    for pr in range(npair):
        s_ref[pr] = state[pr]

    y = jnp.concatenate([jnp.concatenate(yc, axis=1) for yc in ys], axis=0)
    mean = seg(y) * (1.0 / HEAD)
    d = y - mean
    var = seg(d * d) * (1.0 / HEAD)
    yn = d * lax.rsqrt(var + GN_EPS) * prow(P_LNW) + prow(P_LNB)
    bonus = seg(r * k2 * prow(P_RK)) * v
    g = zg_ref[...]
    o_ref[...] = ((yn + bonus) * g * _sigmoid(g)).astype(o_ref.dtype)


def _rwkv(z, par, mul, w2p, a2p, *, tb, npair):
    T = z.shape[0]
    width = npair * LANES
    nb = WIDTH // width
    zspec = lambda off: pl.BlockSpec((tb, width), lambda g, t: (t, off + g))
    return pl.pallas_call(
        functools.partial(_rwkv_kernel, tb=tb, npair=npair),
        grid=(nb, T // tb),
        in_specs=[
            zspec(0), zspec(nb), zspec(2 * nb), zspec(3 * nb),
            pl.BlockSpec((tb, LANES), lambda g, t: (t, Z_LORA // LANES)),
            pl.BlockSpec((16, width), lambda g, t: (0, g)),
            pl.BlockSpec((8, LANES), lambda g, t: (0, 0)),
            pl.BlockSpec((LANES, width), lambda g, t: (0, g)),
            pl.BlockSpec((LANES, width), lambda g, t: (0, g)),
        ],
        out_specs=pl.BlockSpec((tb, width), lambda g, t: (t, g)),
        out_shape=jax.ShapeDtypeStruct((T, WIDTH), bf16),
        scratch_shapes=[pltpu.VMEM((npair, LANES, LANES), f32), pltpu.VMEM((3, 8, width), f32),
                        pltpu.VMEM((8, LANES), f32)],
        compiler_params=pltpu.CompilerParams(
            dimension_semantics=("parallel", "arbitrary"), vmem_limit_bytes=VMEM_LIMIT),
        name="rwkv7",
    )(z, z, z, z, z, par, mul, w2p, a2p)


def _out_kernel(x_ref, yr_ref, yf_ref, p_ref, wo_ref, wg_ref, wp_ref, g1_ref, g2_ref, o_ref):
    def rms(t, g):
        return t * lax.rsqrt(jnp.mean(t * t, axis=-1, keepdims=True) + RMS_EPS) * g

    m = _dot(yr_ref[...], wo_ref[0:WIDTH, :]) + _dot(yf_ref[...], wo_ref[WIDTH:, :])
    x1 = x_ref[...] + rms(m, g1_ref[...])
    gate = _sigmoid(_dot(rms(x1, g2_ref[...]).astype(bf16), wg_ref[...]))
    o_ref[...] = x1 + gate * _dot(p_ref[...].astype(bf16), wp_ref[...])


def _out_proj(x, yr, yf, p, wo, wg, wp, g1, g2, *, tm):
    T = x.shape[0]
    const = lambda shape: pl.BlockSpec(shape, lambda i: (0, 0))
    return pl.pallas_call(
        _out_kernel,
        grid=(T // tm,),
        in_specs=[
            pl.BlockSpec((tm, D_MODEL), lambda i: (i, 0)),
            pl.BlockSpec((tm, WIDTH), lambda i: (i, 0)),
            pl.BlockSpec((tm, WIDTH), lambda i: (i, 0)),
            pl.BlockSpec((tm, D_PLE), lambda i: (i, 0)),
            const((D_MODEL, D_MODEL)), const((D_MODEL, D_MODEL)), const((D_PLE, D_MODEL)),
            const((1, D_MODEL)), const((1, D_MODEL)),
        ],
        out_specs=pl.BlockSpec((tm, D_MODEL), lambda i: (i, 0)),
        out_shape=jax.ShapeDtypeStruct((T, D_MODEL), f32),
        compiler_params=pltpu.CompilerParams(
            dimension_semantics=("parallel",), vmem_limit_bytes=VMEM_LIMIT),
        name="out_proj",
    )(x, yr, yf, p, wo, wg, wp, g1, g2)


def _layer(x, p, pre_g, w_in, mu_r, mu_k, mu_v, mu_w, mu_a, w0, w2, a0, a2, k_k, k_a, r_k, ln_w, ln_b,
           b_f, q_g, k_g, w_out, post_g, ple_g, w_gate, w_ple):
    T = x.shape[0]
    rw_end = 4 * WIDTH
    lora_end = rw_end + 2 * LORA
    fx_end = lora_end + 4 * WIDTH
    w = jnp.concatenate([
        w_in[:, :rw_end], w_in[:, lora_end:fx_end], w_in[:, rw_end:lora_end], w_in[:, fx_end:],
        jnp.zeros((D_MODEL, Z_COLS - w_in.shape[1]), w_in.dtype)], axis=1).astype(bf16)
    z = _in_proj(x, pre_g.reshape(1, D_MODEL), w, tm=min(1024, T), tn=1408)

    gq = jnp.tile(q_g, NHEADS).reshape(1, WIDTH)
    gk = jnp.tile(k_g, NHEADS).reshape(1, WIDTH)
    bfp = jnp.pad(b_f, (0, LANES - NHEADS)).reshape(1, LANES)
    tb = min(256, T)
    qt, ka, vt, fstat = _fox_prep(z, gq, gk, bfp, tb=tb)
    qk_bound = 1.02 * HEAD ** 0.5 * LOG2E * jnp.max(jnp.abs(q_g)) * jnp.max(jnp.abs(k_g))
    y_fx = _attention(qt, ka, vt, z, fstat, qk_bound, tb=tb, tq=min(1024, T), tk=min(512, T))

    par = jnp.stack([mu_r, mu_k, mu_v, w0, a0, k_k, k_a, r_k.reshape(WIDTH), ln_w, ln_b])
    par = jnp.pad(par, ((0, 16 - par.shape[0]), (0, 0)))
    mul = jnp.broadcast_to(jnp.concatenate([mu_w, mu_a]).reshape(1, LANES), (8, LANES))
    zeros = jnp.zeros((LORA, WIDTH), f32)
    w2p = jnp.concatenate([w2, zeros], axis=0)
    a2p = jnp.concatenate([zeros, a2], axis=0)
    y_rw = _rwkv(z, par, mul, w2p, a2p, tb=min(256, T), npair=2)

    return _out_proj(x, y_rw, y_fx, p, w_out.astype(bf16), w_gate.astype(bf16), w_ple.astype(bf16),
                     post_g.reshape(1, D_MODEL), ple_g.reshape(1, D_MODEL), tm=min(256, T))


def kernel(x, p, pre_norm_g, w_in, rw_mu_r, rw_mu_k, rw_mu_v, rw_mu_w, rw_mu_a, rw_w0, rw_w2, rw_a0, rw_a2,
           rw_k_k, rw_k_a, rw_r_k, rw_ln_w, rw_ln_b, fx_b_f, fx_q_g, fx_k_g, w_out, post_norm_g, ple_norm_g,
           w_ple_gate, w_ple):
    B = x.shape[0]
    outs = []
    for b in range(B):
        xb = x[b]
        for i in range(p.shape[0]):
            xb = _layer(xb, p[i, b], pre_norm_g[i], w_in[i], rw_mu_r[i], rw_mu_k[i], rw_mu_v[i], rw_mu_w[i],
                        rw_mu_a[i], rw_w0[i], rw_w2[i], rw_a0[i], rw_a2[i], rw_k_k[i], rw_k_a[i], rw_r_k[i],
                        rw_ln_w[i], rw_ln_b[i], fx_b_f[i], fx_q_g[i], fx_k_g[i], w_out[i], post_norm_g[i],
                        ple_norm_g[i], w_ple_gate[i], w_ple[i])
        outs.append(xb)
    return jnp.stack(outs)
```

```python
import functools

import numpy as np
import jax
import jax.numpy as jnp
from jax import lax
from jax.experimental import pallas as pl
from jax.experimental.pallas import tpu as pltpu

f32 = jnp.float32
bf16 = jnp.bfloat16

D_MODEL = 2048
D_PLE = 256
WIDTH = 1024
HEAD = 64
NHEADS = 16
NPAIRS = NHEADS // 2
LORA = 64
LANES = 128
RMS_EPS = 1e-6
GN_EPS = 64e-5
CHUNK = 64
LOG2E = 1.4426950408889634
EXP_NEG_HALF = 0.6065306597126334
NEG_BIG = -1e30

Z_RW = 0
Z_FX = 4096
Z_LORA = 8192
Z_FF = 8320
Z_COLS = 8448
VT_ROWS = 80
Q_COL = 256
SKIP_LOG2 = 48.0

VMEM_LIMIT = 56 * 1024 * 1024


def _dot(a, b):
    return jnp.dot(a, b, preferred_element_type=f32)


def _dot_nt(a, b):
    return lax.dot_general(a, b, (((1,), (1,)), ((), ())), preferred_element_type=f32)


def _dot_tn(a, b):
    return lax.dot_general(a, b, (((0,), (0,)), ((), ())), preferred_element_type=f32)


def _split2(x):
    hi = x.astype(bf16)
    lo = (x - hi.astype(f32)).astype(bf16)
    return hi, lo


def _split3(x):
    hi = x.astype(bf16)
    r = x - hi.astype(f32)
    mid = r.astype(bf16)
    lo = (r - mid.astype(f32)).astype(bf16)
    return hi, mid, lo


def _dot_sel(x, sel):
    hi, mid, lo = _split3(x)
    return _dot(hi, sel) + _dot(mid, sel) + _dot(lo, sel)


def _sel_dot(sel, x):
    hi, mid, lo = _split3(x)
    return _dot(sel, hi) + _dot(sel, mid) + _dot(sel, lo)


def _dot_x3(a, b):
    ah, al = _split2(a)
    bh, bl = _split2(b)
    return _dot(ah, bh) + _dot(al, bh) + _dot(ah, bl)


def _sigmoid(x):
    return 1.0 / (1.0 + jnp.exp(-x))


def _in_proj_kernel(x_ref, g_ref, w_ref, z_ref, h_ref):
    @pl.when(pl.program_id(1) == 0)
    def _():
        x = x_ref[...]
        ms = jnp.mean(x * x, axis=-1, keepdims=True)
        h_ref[...] = (x * lax.rsqrt(ms + RMS_EPS) * g_ref[...]).astype(bf16)

    z_ref[...] = _dot(h_ref[...], w_ref[...])


def _in_proj(x, g, w, *, tm, tn):
    T = x.shape[0]
    return pl.pallas_call(
        _in_proj_kernel,
        grid=(T // tm, Z_COLS // tn),
        in_specs=[
            pl.BlockSpec((tm, D_MODEL), lambda i, j: (i, 0)),
            pl.BlockSpec((1, D_MODEL), lambda i, j: (0, 0)),
            pl.BlockSpec((D_MODEL, tn), lambda i, j: (0, j)),
        ],
        out_specs=pl.BlockSpec((tm, tn), lambda i, j: (i, j)),
        out_shape=jax.ShapeDtypeStruct((T, Z_COLS), f32),
        scratch_shapes=[pltpu.VMEM((tm, D_MODEL), bf16)],
        compiler_params=pltpu.CompilerParams(
            dimension_semantics=("parallel", "arbitrary"), vmem_limit_bytes=VMEM_LIMIT),
        name="in_proj",
    )(x, g, w)


def _fox_prep_kernel(fq_ref, fk_ref, fv_ref, ff_ref, gq_ref, gk_ref, bf_ref,
                     qt_ref, ka_ref, vt_ref, fs_ref, carry_ref, *, tb):
    @pl.when(pl.program_id(0) == 0)
    def _():
        carry_ref[...] = jnp.zeros_like(carry_ref)

    ci = lax.broadcasted_iota(jnp.int32, (WIDTH, LANES), 0)
    hi_ = lax.broadcasted_iota(jnp.int32, (WIDTH, LANES), 1)
    ind = jnp.where((ci >> 6) == hi_, 1.0, 0.0).astype(bf16)
    hj = lax.broadcasted_iota(jnp.int32, (LANES, WIDTH), 0)
    cj = lax.broadcasted_iota(jnp.int32, (LANES, WIDTH), 1)
    ind_t = jnp.where((cj >> 6) == hj, 1.0, 0.0).astype(bf16)

    def head_rms(x, gain):
        ssq = _dot_sel(x * x, ind)
        rinv = lax.rsqrt(ssq * (1.0 / HEAD) + RMS_EPS)
        return x * _dot_sel(rinv, ind_t) * gain

    qn = head_rms(fq_ref[...], gq_ref[...]) * (HEAD ** -0.5 * LOG2E)
    kn = head_rms(fk_ref[...], gk_ref[...])

    xf = ff_ref[...] + bf_ref[...]
    logf = jnp.minimum(xf, 0.0) - jnp.log(1.0 + jnp.exp(-jnp.abs(xf)))
    ri = lax.broadcasted_iota(jnp.int32, (tb, tb), 0)
    rj = lax.broadcasted_iota(jnp.int32, (tb, tb), 1)
    tri = jnp.where(rj <= ri, 1.0, 0.0).astype(bf16)
    cum = _sel_dot(tri, logf) + carry_ref[0:1, :]
    carry_ref[...] = jnp.broadcast_to(cum[tb - 1:tb, :], carry_ref.shape)
    cum2 = cum * LOG2E
    c_hi, c_mid, c_lo = (t.astype(f32) for t in _split3(cum2))
    srow = lax.broadcasted_iota(jnp.int32, (8, LANES), 0)
    fs_ref[0] = jnp.where(srow == 0, jnp.max(cum2, axis=0, keepdims=True), jnp.min(cum2, axis=0, keepdims=True))

    lane = lax.broadcasted_iota(jnp.int32, (tb, LANES), 1)
    low = lane < HEAD
    vt = fv_ref[...].T
    vpad = jnp.concatenate(
        [jnp.ones((1, tb), f32), jnp.zeros((VT_ROWS - HEAD - 1, tb), f32)], axis=0)
    for h in range(NHEADS):
        col = slice((h // 2) * LANES, (h // 2 + 1) * LANES)
        qc, kc = qn[:, col], kn[:, col]
        if h % 2 == 1:
            qc, kc = pltpu.roll(qc, HEAD, axis=1), pltpu.roll(kc, HEAD, axis=1)
        fh, fm, fl = c_hi[:, h:h + 1], c_mid[:, h:h + 1], c_lo[:, h:h + 1]
        eq = jnp.where(lane == HEAD, fh, jnp.where(lane == HEAD + 1, fm, jnp.where(lane == HEAD + 2, fl,
             jnp.where(lane < HEAD + 6, 1.0, 0.0))))
        ek = jnp.where(lane == HEAD + 3, -fh, jnp.where(lane == HEAD + 4, -fm, jnp.where(lane == HEAD + 5, -fl,
             jnp.where(lane < HEAD + 3, 1.0, 0.0))))
        qt_ref[h] = jnp.where(low, qc, eq).T.astype(bf16)
        ka_ref[h] = jnp.where(low, kc, ek).astype(bf16)
        vt_ref[h] = jnp.concatenate([vt[h * HEAD:(h + 1) * HEAD, :], vpad], axis=0).astype(bf16)


def _fox_prep(z, gq, gk, bfp, *, tb):
    T = z.shape[0]
    wb = Z_FX // WIDTH
    return pl.pallas_call(
        functools.partial(_fox_prep_kernel, tb=tb),
        grid=(T // tb,),
        in_specs=[
            pl.BlockSpec((tb, WIDTH), lambda i: (i, wb)),
            pl.BlockSpec((tb, WIDTH), lambda i: (i, wb + 1)),
            pl.BlockSpec((tb, WIDTH), lambda i: (i, wb + 2)),
            pl.BlockSpec((tb, LANES), lambda i: (i, Z_FF // LANES)),
            pl.BlockSpec((1, WIDTH), lambda i: (0, 0)),
            pl.BlockSpec((1, WIDTH), lambda i: (0, 0)),
            pl.BlockSpec((1, LANES), lambda i: (0, 0)),
        ],
        out_specs=[
            pl.BlockSpec((NHEADS, LANES, tb), lambda i: (0, 0, i)),
            pl.BlockSpec((NHEADS, tb, LANES), lambda i: (0, i, 0)),
            pl.BlockSpec((NHEADS, VT_ROWS, tb), lambda i: (0, 0, i)),
            pl.BlockSpec((1, 8, LANES), lambda i: (i, 0, 0)),
        ],
        out_shape=[
            jax.ShapeDtypeStruct((NHEADS, LANES, T), bf16),
            jax.ShapeDtypeStruct((NHEADS, T, LANES), bf16),
            jax.ShapeDtypeStruct((NHEADS, VT_ROWS, T), bf16),
            jax.ShapeDtypeStruct((T // tb, 8, LANES), f32),
        ],
        scratch_shapes=[pltpu.VMEM((8, LANES), f32)],
        compiler_params=pltpu.CompilerParams(
            dimension_semantics=("arbitrary",), vmem_limit_bytes=VMEM_LIMIT),
        name="fox_prep",
    )(z, z, z, z, gq, gk, bfp)


def _attn_kernel(work_ref, qt_ref, ka_ref, vt_ref, fg_ref, o_ref, m_ref, acc_ref, *, tq, tk):
    code = work_ref[pl.program_id(0) * pl.num_programs(1) + pl.program_id(1)]
    variant = ((code >> 16) & 0xF) - 1
    first = (code >> 20) & 1
    last = (code >> 21) & 1
    valid = (code >> 22) & 1

    @pl.when(first == 1)
    def _():
        m_ref[...] = jnp.full(m_ref.shape, NEG_BIG, f32)
        acc_ref[...] = jnp.zeros_like(acc_ref)

    def update(d):
        def n_keys(q_lo):
            return tk if d < 0 else max(0, min(tk, q_lo + Q_COL - d * tk))

        chains = [(hh, qs * Q_COL) for qs in range(tq // Q_COL) for hh in range(2) if n_keys(qs * Q_COL) > 0]

        def scores(hh, q_lo):
            nk = n_keys(q_lo)
            st = _dot(ka_ref[hh, 0:nk, :], qt_ref[hh, :, q_lo:q_lo + Q_COL])
            if d >= 0 and d * tk + nk - 1 > q_lo:
                kpos = d * tk + lax.broadcasted_iota(jnp.int32, st.shape, 0)
                qpos = q_lo + lax.broadcasted_iota(jnp.int32, st.shape, 1)
                st = jnp.where(kpos <= qpos, st, NEG_BIG)
            return st

        def consume(hh, q_lo, st):
            qsl = slice(q_lo, q_lo + Q_COL)
            m_old = m_ref[hh, :, qsl]
            m_new = jnp.maximum(m_old, jnp.max(st, axis=0, keepdims=True))
            m_ref[hh, :, qsl] = m_new
            p = jnp.exp2(st - m_new).astype(bf16)
            acc_ref[hh, :, qsl] = (jnp.exp2(m_old - m_new) * acc_ref[hh, :, qsl]
                                   + _dot(vt_ref[hh, :, 0:n_keys(q_lo)], p))

        st_next = scores(*chains[0])
        for n, chain in enumerate(chains):
            st = st_next
            if n + 1 < len(chains):
                st_next = scores(*chains[n + 1])
            consume(*chain, st)

    for d in range(-1, tq // tk):
        @pl.when((valid == 1) & (variant == d))
        def _(d=d):
            update(d)

    @pl.when((valid == 1) & (last == 1))
    def _():
        outs = []
        for hh in range(2):
            acc = acc_ref[hh]
            outs.append(acc[0:HEAD, :] / acc[HEAD:HEAD + 1, :])
        y = jnp.concatenate(outs, axis=0).T
        g = fg_ref[...]
        o_ref[...] = (y * g * _sigmoid(g)).astype(o_ref.dtype)


def _attention_work_list(fstat, qk_bound, *, T, tb, tq, tk):
    r = tq // tk
    nq = T // tq
    qi = np.concatenate([np.full(r * (i + 1), i, np.int32) for i in range(nq)])
    kj = np.concatenate([np.arange(r * (i + 1), dtype=np.int32) for i in range(nq)])
    steps = len(qi)
    fmax = fstat[:, 0, :NHEADS]
    fmin = fstat[:, 1, :NHEADS]
    fmax_q = fmax.reshape(nq, tq // tb, NHEADS).max(axis=1)
    fmin_k = fmin.reshape(T // tk, tk // tb, NHEADS).min(axis=1)
    need = (fmax_q[qi] - fmin_k[kj] + 2.0 * qk_bound) > -SKIP_LOG2
    need = (need[:, 0::2] | need[:, 1::2]).T
    count = jnp.sum(need, axis=1, keepdims=True).astype(jnp.int32)
    order = jnp.argsort(jnp.logical_not(need), axis=1, stable=True).astype(jnp.int32)
    pos = jnp.arange(steps, dtype=jnp.int32)[None, :]
    sel = jnp.take_along_axis(order, jnp.minimum(pos, count - 1), axis=1)
    qs, ks = jnp.asarray(qi)[sel], jnp.asarray(kj)[sel]
    variant = jnp.maximum(ks - r * qs, -1)
    first = jnp.concatenate([jnp.ones((NPAIRS, 1), jnp.int32),
                             (qs[:, 1:] != qs[:, :-1]).astype(jnp.int32)], axis=1)
    last = (ks == r * qs + r - 1).astype(jnp.int32)
    valid = (pos < count).astype(jnp.int32)
    code = qs | (ks << 8) | ((variant + 1) << 16) | ((first & valid) << 20) | (last << 21) | (valid << 22)
    return code.reshape(-1).astype(jnp.int32), steps


def _attention(qt, ka, vt, z, fstat, qk_bound, *, tb, tq, tk):
    T = ka.shape[1]
    work, steps = _attention_work_list(fstat, qk_bound, T=T, tb=tb, tq=tq, tk=tk)
    fg_col = (Z_FX + 3 * WIDTH) // LANES
    qblk = lambda g, s, w: w[g * steps + s] & 0xFF
    kblk = lambda g, s, w: (w[g * steps + s] >> 8) & 0xFF
    grid_spec = pltpu.PrefetchScalarGridSpec(
        num_scalar_prefetch=1,
        grid=(NPAIRS, steps),
        in_specs=[
            pl.BlockSpec((2, LANES, tq), lambda g, s, w: (g, 0, qblk(g, s, w))),
            pl.BlockSpec((2, tk, LANES), lambda g, s, w: (g, kblk(g, s, w), 0)),
            pl.BlockSpec((2, VT_ROWS, tk), lambda g, s, w: (g, 0, kblk(g, s, w))),
            pl.BlockSpec((tq, LANES), lambda g, s, w: (qblk(g, s, w), fg_col + g)),
        ],
        out_specs=pl.BlockSpec((tq, LANES), lambda g, s, w: (qblk(g, s, w), g)),
        scratch_shapes=[pltpu.VMEM((2, 1, tq), f32), pltpu.VMEM((2, VT_ROWS, tq), f32)],
    )
    return pl.pallas_call(
        functools.partial(_attn_kernel, tq=tq, tk=tk),
        grid_spec=grid_spec,
        out_shape=jax.ShapeDtypeStruct((T, WIDTH), bf16),
        compiler_params=pltpu.CompilerParams(
            dimension_semantics=("parallel", "arbitrary"), vmem_limit_bytes=VMEM_LIMIT),
        name="fox_attention",
    )(work, qt, ka, vt, z)


P_MU_R, P_MU_K, P_MU_V, P_W0, P_A0, P_KK, P_KA, P_RK, P_LNW, P_LNB = range(10)


def _rwkv_kernel(zr_ref, zk_ref, zv_ref, zg_ref, zl_ref, par_ref, mul_ref, w2_ref, a2_ref,
                 o_ref, s_ref, prev_ref, prevl_ref, *, tb, npair):
    width = npair * LANES

    @pl.when(pl.program_id(1) == 0)
    def _():
        s_ref[...] = jnp.zeros_like(s_ref)
        prev_ref[...] = jnp.zeros_like(prev_ref)
        prevl_ref[...] = jnp.zeros_like(prevl_ref)

    par = par_ref[...]
    prow = lambda k: par[k:k + 1, :]

    def shift(x, mu, last_ref):
        row = lax.broadcasted_iota(jnp.int32, x.shape, 0)
        xp = pltpu.roll(x, 1, axis=0)
        xp = jnp.where(row == 0, last_ref[7:8, :], xp)
        last_ref[...] = x[tb - 8:tb, :]
        return x + (xp - x) * mu

    r = shift(zr_ref[...], prow(P_MU_R), prev_ref.at[0])
    k = shift(zk_ref[...], prow(P_MU_K), prev_ref.at[1])
    v = shift(zv_ref[...], prow(P_MU_V), prev_ref.at[2])
    sl = shift(zl_ref[...], mul_ref[0:1, :], prevl_ref)

    bi = lax.broadcasted_iota(jnp.int32, (LANES, LANES), 0)
    bj = lax.broadcasted_iota(jnp.int32, (LANES, LANES), 1)
    same_head = jnp.where((bi >> 6) == (bj >> 6), 1.0, 0.0).astype(bf16)

    def seg(x):
        cols = []
        for pr in range(npair):
            hi, lo = _split2(x[:, pr * LANES:(pr + 1) * LANES])
            cols.append(_dot(hi, same_head) + _dot(lo, same_head))
        return jnp.concatenate(cols, axis=1)

    u = prow(P_W0) + _dot_x3(jnp.tanh(sl), w2_ref[...])
    ld = -EXP_NEG_HALF * _sigmoid(u)
    av = _sigmoid(prow(P_A0) + _dot_x3(sl, a2_ref[...]))

    kk = k * prow(P_KK)
    kk = kk * lax.rsqrt(jnp.maximum(seg(kk * kk), 1e-24))
    k2 = k * (1.0 + (av - 1.0) * prow(P_KA))

    ti = lax.broadcasted_iota(jnp.int32, (tb, tb), 0)
    tj = lax.broadcasted_iota(jnp.int32, (tb, tb), 1)
    tri = jnp.where(((ti >> 6) == (tj >> 6)) & (tj <= ti), 1.0, 0.0).astype(bf16)
    cum = _sel_dot(tri, ld)

    e_pos = jnp.exp(cum)
    e_neg = jnp.exp(-cum)
    a_t = -kk * jnp.exp(cum - ld)
    b_t = kk * av * e_neg
    k_t = k2 * e_neg
    r_t = r * e_pos

    ri = lax.broadcasted_iota(jnp.int32, (LANES, LANES), 0)
    rj = lax.broadcasted_iota(jnp.int32, (LANES, LANES), 1)
    strict_lower = rj < ri
    eye = jnp.where(ri == rj, 1.0, 0.0)
    qi = lax.broadcasted_iota(jnp.int32, (CHUNK, LANES), 0)
    qj = lax.broadcasted_iota(jnp.int32, (CHUNK, LANES), 1)
    incl_lower = (qj & (CHUNK - 1)) <= qi
    chunk_head0 = qj < HEAD

    def stack(x):
        return jnp.concatenate([jnp.where(chunk_head0, x, 0.0), jnp.where(chunk_head0, 0.0, x)], axis=0)

    def level_mask(b):
        sh = b.bit_length()
        return ((ri >> sh) == (rj >> sh)) & ((ri & (2 * b - 1)) >= b) & ((rj & (2 * b - 1)) < b)

    nchunk = tb // CHUNK
    units = [(pr, c) for pr in range(npair) for c in range(nchunk)]
    rows = lambda c: slice(c * CHUNK, (c + 1) * CHUNK)
    lanes = lambda pr: slice(pr * LANES, (pr + 1) * LANES)
    cut = lambda x, u: x[rows(u[1]), lanes(u[0])]
    cend = {u: cum[(u[1] + 1) * CHUNK - 1:(u[1] + 1) * CHUNK, lanes(u[0])] for u in units}

    am = {u: stack(cut(a_t, u)) for u in units}
    vm = {u: stack(cut(v, u)) for u in units}
    x = {u: _dot_nt(jnp.concatenate([am[u], cut(r_t, u)], axis=0).astype(bf16),
                    jnp.concatenate([stack(cut(b_t, u)), stack(cut(k_t, u))], axis=0).astype(bf16))
         for u in units}
    l_ab = {u: jnp.where(strict_lower, x[u][0:LANES, 0:LANES], 0.0) for u in units}
    l_ak = {u: jnp.where(strict_lower, x[u][0:LANES, LANES:], 0.0) for u in units}
    l_r = {u: jnp.where(jnp.concatenate([incl_lower, incl_lower], axis=1), x[u][LANES:, :], 0.0).astype(bf16)
           for u in units}

    inv = {u: eye + jnp.where(level_mask(1), l_ab[u], 0.0) for u in units}
    b = 2
    while b < CHUNK:
        mask = level_mask(b)
        y_ = {u: _dot(jnp.where(mask, l_ab[u], 0.0).astype(bf16), inv[u].astype(bf16)) for u in units}
        inv = {u: inv[u] + _dot(inv[u].astype(bf16), y_[u].astype(bf16)) for u in units}
        b *= 2

    lv = {u: _dot(l_ak[u].astype(bf16), vm[u].astype(bf16)) for u in units}
    tg = {u: _dot(inv[u].astype(bf16), jnp.concatenate([am[u], lv[u]], axis=1).astype(bf16)) for u in units}
    zeros = jnp.zeros((LANES, LANES), f32)
    ly = {u: _dot(l_r[u], jnp.concatenate(
              [tg[u], jnp.concatenate([zeros, vm[u]], axis=1)], axis=0).astype(bf16)) for u in units}
    bh = {u: stack(cut(kk, u) * cut(av, u) * jnp.exp(cend[u] - cut(cum, u))).astype(bf16) for u in units}
    kh = {u: stack(cut(k2, u) * jnp.exp(cend[u] - cut(cum, u))).astype(bf16) for u in units}
    ag = {u: _dot_tn(tg[u].astype(bf16), bh[u]) for u in units}
    vk = {u: _dot_tn(vm[u].astype(bf16), kh[u]) for u in units}
    q_eff = {u: (cut(r_t, u) + ly[u][:, 0:LANES]).astype(bf16) for u in units}
    a_eff = {u: ag[u][0:LANES, :].astype(bf16) for u in units}
    g_eff = {u: ag[u][LANES:, :] + vk[u] for u in units}

    ys = [[None] * npair for _ in range(nchunk)]
    state = [s_ref[pr] for pr in range(npair)]
    for c in range(nchunk):
        for pr in range(npair):
            u = (pr, c)
            s_bf = state[pr].astype(bf16)
            ys[c][pr] = _dot_nt(q_eff[u], s_bf) + ly[u][:, LANES:]
            state[pr] = state[pr] * jnp.exp(cend[u]) + _dot(s_bf, a_eff[u]) + g_eff[u]
    for pr in range(npair):
        s_ref[pr] = state[pr]

    y = jnp.concatenate([jnp.concatenate(yc, axis=1) for yc in ys], axis=0)
    mean = seg(y) * (1.0 / HEAD)
    d = y - mean
    var = seg(d * d) * (1.0 / HEAD)
    yn = d * lax.rsqrt(var + GN_EPS) * prow(P_LNW) + prow(P_LNB)
    bonus = seg(r * k2 * prow(P_RK)) * v
    g = zg_ref[...]
    o_ref[...] = ((yn + bonus) * g * _sigmoid(g)).astype(o_ref.dtype)


def _rwkv(z, par, mul, w2p, a2p, *, tb, npair):
    T = z.shape[0]
    width = npair * LANES
    nb = WIDTH // width
    zspec = lambda off: pl.BlockSpec((tb, width), lambda g, t: (t, off + g))
    return pl.pallas_call(
        functools.partial(_rwkv_kernel, tb=tb, npair=npair),
        grid=(nb, T // tb),
        in_specs=[
            zspec(0), zspec(nb), zspec(2 * nb), zspec(3 * nb),
            pl.BlockSpec((tb, LANES), lambda g, t: (t, Z_LORA // LANES)),
            pl.BlockSpec((16, width), lambda g, t: (0, g)),
            pl.BlockSpec((8, LANES), lambda g, t: (0, 0)),
            pl.BlockSpec((LANES, width), lambda g, t: (0, g)),
            pl.BlockSpec((LANES, width), lambda g, t: (0, g)),
        ],
        out_specs=pl.BlockSpec((tb, width), lambda g, t: (t, g)),
        out_shape=jax.ShapeDtypeStruct((T, WIDTH), bf16),
        scratch_shapes=[pltpu.VMEM((npair, LANES, LANES), f32), pltpu.VMEM((3, 8, width), f32),
                        pltpu.VMEM((8, LANES), f32)],
        compiler_params=pltpu.CompilerParams(
            dimension_semantics=("parallel", "arbitrary"), vmem_limit_bytes=VMEM_LIMIT),
        name="rwkv7",
    )(z, z, z, z, z, par, mul, w2p, a2p)


def _out_kernel(x_ref, yr_ref, yf_ref, p_ref, wo_ref, wg_ref, wp_ref, g1_ref, g2_ref, o_ref):
    def rms(t, g):
        return t * lax.rsqrt(jnp.mean(t * t, axis=-1, keepdims=True) + RMS_EPS) * g

    m = _dot(yr_ref[...], wo_ref[0:WIDTH, :]) + _dot(yf_ref[...], wo_ref[WIDTH:, :])
    x1 = x_ref[...] + rms(m, g1_ref[...])
    gate = _sigmoid(_dot(rms(x1, g2_ref[...]).astype(bf16), wg_ref[...]))
    o_ref[...] = x1 + gate * _dot(p_ref[...].astype(bf16), wp_ref[...])


def _out_proj(x, yr, yf, p, wo, wg, wp, g1, g2, *, tm):
    T = x.shape[0]
    const = lambda shape: pl.BlockSpec(shape, lambda i: (0, 0))
    return pl.pallas_call(
        _out_kernel,
        grid=(T // tm,),
        in_specs=[
            pl.BlockSpec((tm, D_MODEL), lambda i: (i, 0)),
            pl.BlockSpec((tm, WIDTH), lambda i: (i, 0)),
            pl.BlockSpec((tm, WIDTH), lambda i: (i, 0)),
            pl.BlockSpec((tm, D_PLE), lambda i: (i, 0)),
            const((D_MODEL, D_MODEL)), const((D_MODEL, D_MODEL)), const((D_PLE, D_MODEL)),
            const((1, D_MODEL)), const((1, D_MODEL)),
        ],
        out_specs=pl.BlockSpec((tm, D_MODEL), lambda i: (i, 0)),
        out_shape=jax.ShapeDtypeStruct((T, D_MODEL), f32),
        compiler_params=pltpu.CompilerParams(
            dimension_semantics=("parallel",), vmem_limit_bytes=VMEM_LIMIT),
        name="out_proj",
    )(x, yr, yf, p, wo, wg, wp, g1, g2)


def _layer(x, p, pre_g, w_in, mu_r, mu_k, mu_v, mu_w, mu_a, w0, w2, a0, a2, k_k, k_a, r_k, ln_w, ln_b,
           b_f, q_g, k_g, w_out, post_g, ple_g, w_gate, w_ple):
    T = x.shape[0]
    rw_end = 4 * WIDTH
    lora_end = rw_end + 2 * LORA
    fx_end = lora_end + 4 * WIDTH
    perm = jnp.argsort(b_f)
    by_head = lambda w_, axis: jnp.take(w_, perm, axis=axis)
    w_fx = by_head(w_in[:, lora_end:fx_end].reshape(D_MODEL, 4, NHEADS, HEAD), 2).reshape(D_MODEL, 4 * WIDTH)
    w = jnp.concatenate([
        w_in[:, :rw_end], w_fx, w_in[:, rw_end:lora_end], by_head(w_in[:, fx_end:], 1),
        jnp.zeros((D_MODEL, Z_COLS - w_in.shape[1]), w_in.dtype)], axis=1).astype(bf16)
    w_out = jnp.concatenate(
        [w_out[:WIDTH], by_head(w_out[WIDTH:].reshape(NHEADS, HEAD, D_MODEL), 0).reshape(WIDTH, D_MODEL)], axis=0)
    b_f = by_head(b_f, 0)
    z = _in_proj(x, pre_g.reshape(1, D_MODEL), w, tm=min(1024, T), tn=1408)

    gq = jnp.tile(q_g, NHEADS).reshape(1, WIDTH)
    gk = jnp.tile(k_g, NHEADS).reshape(1, WIDTH)
    bfp = jnp.pad(b_f, (0, LANES - NHEADS)).reshape(1, LANES)
    tb = min(256, T)
    qt, ka, vt, fstat = _fox_prep(z, gq, gk, bfp, tb=tb)
    qk_bound = 1.02 * HEAD ** 0.5 * LOG2E * jnp.max(jnp.abs(q_g)) * jnp.max(jnp.abs(k_g))
    y_fx = _attention(qt, ka, vt, z, fstat, qk_bound, tb=tb, tq=min(1024, T), tk=min(512, T))

    par = jnp.stack([mu_r, mu_k, mu_v, w0, a0, k_k, k_a, r_k.reshape(WIDTH), ln_w, ln_b])
    par = jnp.pad(par, ((0, 16 - par.shape[0]), (0, 0)))
    mul = jnp.broadcast_to(jnp.concatenate([mu_w, mu_a]).reshape(1, LANES), (8, LANES))
    zeros = jnp.zeros((LORA, WIDTH), f32)
    w2p = jnp.concatenate([w2, zeros], axis=0)
    a2p = jnp.concatenate([zeros, a2], axis=0)
    y_rw = _rwkv(z, par, mul, w2p, a2p, tb=min(256, T), npair=4)

    return _out_proj(x, y_rw, y_fx, p, w_out.astype(bf16), w_gate.astype(bf16), w_ple.astype(bf16),
                     post_g.reshape(1, D_MODEL), ple_g.reshape(1, D_MODEL), tm=min(256, T))


def kernel(x, p, pre_norm_g, w_in, rw_mu_r, rw_mu_k, rw_mu_v, rw_mu_w, rw_mu_a, rw_w0, rw_w2, rw_a0, rw_a2,
           rw_k_k, rw_k_a, rw_r_k, rw_ln_w, rw_ln_b, fx_b_f, fx_q_g, fx_k_g, w_out, post_norm_g, ple_norm_g,
           w_ple_gate, w_ple):
    B = x.shape[0]
    outs = []
    for b in range(B):
        xb = x[b]
        for i in range(p.shape[0]):
            xb = _layer(xb, p[i, b], pre_norm_g[i], w_in[i], rw_mu_r[i], rw_mu_k[i], rw_mu_v[i], rw_mu_w[i],
                        rw_mu_a[i], rw_w0[i], rw_w2[i], rw_a0[i], rw_a2[i], rw_k_k[i], rw_k_a[i], rw_r_k[i],
                        rw_ln_w[i], rw_ln_b[i], fx_b_f[i], fx_q_g[i], fx_k_g[i], w_out[i], post_norm_g[i],
                        ple_norm_g[i], w_ple_gate[i], w_ple[i])
        outs.append(xb)
    return jnp.stack(outs)
```

```python
import functools

import numpy as np
import jax
import jax.numpy as jnp
from jax import lax
from jax.experimental import pallas as pl
from jax.experimental.pallas import tpu as pltpu

f32 = jnp.float32
bf16 = jnp.bfloat16

D_MODEL = 2048
D_PLE = 256
WIDTH = 1024
HEAD = 64
NHEADS = 16
NPAIRS = NHEADS // 2
LORA = 64
LANES = 128
RMS_EPS = 1e-6
GN_EPS = 64e-5
CHUNK = 64
LOG2E = 1.4426950408889634
EXP_NEG_HALF = 0.6065306597126334
NEG_BIG = -1e30

Z_RW = 0
Z_FX = 4096
Z_LORA = 8192
Z_FF = 8320
Z_COLS = 8448
VT_ROWS = 80
Q_COL = 256
SKIP_LOG2 = 48.0

VMEM_LIMIT = 56 * 1024 * 1024


def _dot(a, b):
    return jnp.dot(a, b, preferred_element_type=f32)


def _dot_nt(a, b):
    return lax.dot_general(a, b, (((1,), (1,)), ((), ())), preferred_element_type=f32)


def _dot_tn(a, b):
    return lax.dot_general(a, b, (((0,), (0,)), ((), ())), preferred_element_type=f32)


def _split2(x):
    hi = x.astype(bf16)
    lo = (x - hi.astype(f32)).astype(bf16)
    return hi, lo


def _split3(x):
    hi = x.astype(bf16)
    r = x - hi.astype(f32)
    mid = r.astype(bf16)
    lo = (r - mid.astype(f32)).astype(bf16)
    return hi, mid, lo


def _dot_sel(x, sel):
    hi, mid, lo = _split3(x)
    return _dot(hi, sel) + _dot(mid, sel) + _dot(lo, sel)


def _sel_dot(sel, x):
    hi, mid, lo = _split3(x)
    return _dot(sel, hi) + _dot(sel, mid) + _dot(sel, lo)


def _dot_x3(a, b):
    ah, al = _split2(a)
    bh, bl = _split2(b)
    return _dot(ah, bh) + _dot(al, bh) + _dot(ah, bl)


def _sigmoid(x):
    return 1.0 / (1.0 + jnp.exp(-x))


def _in_proj_kernel(x_ref, g_ref, w_ref, z_ref, h_ref):
    @pl.when(pl.program_id(1) == 0)
    def _():
        x = x_ref[...]
        ms = jnp.mean(x * x, axis=-1, keepdims=True)
        h_ref[...] = (x * lax.rsqrt(ms + RMS_EPS) * g_ref[...]).astype(bf16)

    z_ref[...] = _dot(h_ref[...], w_ref[...])


def _in_proj(x, g, w, *, tm, tn):
    T = x.shape[0]
    return pl.pallas_call(
        _in_proj_kernel,
        grid=(T // tm, Z_COLS // tn),
        in_specs=[
            pl.BlockSpec((tm, D_MODEL), lambda i, j: (i, 0)),
            pl.BlockSpec((1, D_MODEL), lambda i, j: (0, 0)),
            pl.BlockSpec((D_MODEL, tn), lambda i, j: (0, j)),
        ],
        out_specs=pl.BlockSpec((tm, tn), lambda i, j: (i, j)),
        out_shape=jax.ShapeDtypeStruct((T, Z_COLS), f32),
        scratch_shapes=[pltpu.VMEM((tm, D_MODEL), bf16)],
        compiler_params=pltpu.CompilerParams(
            dimension_semantics=("parallel", "arbitrary"), vmem_limit_bytes=VMEM_LIMIT),
        name="in_proj",
    )(x, g, w)


def _fox_prep_kernel(fq_ref, fk_ref, fv_ref, ff_ref, gq_ref, gk_ref, bf_ref,
                     qt_ref, ka_ref, vt_ref, fs_ref, carry_ref, *, tb):
    @pl.when(pl.program_id(0) == 0)
    def _():
        carry_ref[...] = jnp.zeros_like(carry_ref)

    bi = lax.broadcasted_iota(jnp.int32, (LANES, LANES), 0)
    bj = lax.broadcasted_iota(jnp.int32, (LANES, LANES), 1)
    same_head = jnp.where((bi >> 6) == (bj >> 6), 1.0, 0.0).astype(bf16)

    def head_rms(x, gain):
        cols = []
        for c in range(WIDTH // LANES):
            xc = x[:, c * LANES:(c + 1) * LANES]
            ssq = _dot((xc * xc).astype(bf16), same_head)
            cols.append(xc * lax.rsqrt(ssq * (1.0 / HEAD) + RMS_EPS))
        return jnp.concatenate(cols, axis=1) * gain

    qn = head_rms(fq_ref[...], gq_ref[...]) * (HEAD ** -0.5 * LOG2E)
    kn = head_rms(fk_ref[...], gk_ref[...])

    xf = ff_ref[...] + bf_ref[...]
    logf = jnp.minimum(xf, 0.0) - jnp.log(1.0 + jnp.exp(-jnp.abs(xf)))
    ri = lax.broadcasted_iota(jnp.int32, (tb, tb), 0)
    rj = lax.broadcasted_iota(jnp.int32, (tb, tb), 1)
    tri = jnp.where(rj <= ri, 1.0, 0.0).astype(bf16)
    cum = _sel_dot(tri, logf) + carry_ref[0:1, :]
    carry_ref[...] = jnp.broadcast_to(cum[tb - 1:tb, :], carry_ref.shape)
    cum2 = cum * LOG2E
    c_hi, c_mid, c_lo = (t.astype(f32) for t in _split3(cum2))
    srow = lax.broadcasted_iota(jnp.int32, (8, LANES), 0)
    fs_ref[0] = jnp.where(srow == 0, jnp.max(cum2, axis=0, keepdims=True), jnp.min(cum2, axis=0, keepdims=True))

    lane = lax.broadcasted_iota(jnp.int32, (tb, LANES), 1)
    low = lane < HEAD
    vt = fv_ref[...].T
    vpad = jnp.concatenate(
        [jnp.ones((1, tb), f32), jnp.zeros((VT_ROWS - HEAD - 1, tb), f32)], axis=0)
    for h in range(NHEADS):
        col = slice((h // 2) * LANES, (h // 2 + 1) * LANES)
        qc, kc = qn[:, col], kn[:, col]
        if h % 2 == 1:
            qc, kc = pltpu.roll(qc, HEAD, axis=1), pltpu.roll(kc, HEAD, axis=1)
        fh, fm, fl = c_hi[:, h:h + 1], c_mid[:, h:h + 1], c_lo[:, h:h + 1]
        eq = jnp.where(lane == HEAD, fh, jnp.where(lane == HEAD + 1, fm, jnp.where(lane == HEAD + 2, fl,
             jnp.where(lane < HEAD + 6, 1.0, 0.0))))
        ek = jnp.where(lane == HEAD + 3, -fh, jnp.where(lane == HEAD + 4, -fm, jnp.where(lane == HEAD + 5, -fl,
             jnp.where(lane < HEAD + 3, 1.0, 0.0))))
        qt_ref[h] = jnp.where(low, qc, eq).T.astype(bf16)
        ka_ref[h] = jnp.where(low, kc, ek).astype(bf16)
        vt_ref[h] = jnp.concatenate([vt[h * HEAD:(h + 1) * HEAD, :], vpad], axis=0).astype(bf16)


def _fox_prep(z, gq, gk, bfp, *, tb):
    T = z.shape[0]
    wb = Z_FX // WIDTH
    return pl.pallas_call(
        functools.partial(_fox_prep_kernel, tb=tb),
        grid=(T // tb,),
        in_specs=[
            pl.BlockSpec((tb, WIDTH), lambda i: (i, wb)),
            pl.BlockSpec((tb, WIDTH), lambda i: (i, wb + 1)),
            pl.BlockSpec((tb, WIDTH), lambda i: (i, wb + 2)),
            pl.BlockSpec((tb, LANES), lambda i: (i, Z_FF // LANES)),
            pl.BlockSpec((1, WIDTH), lambda i: (0, 0)),
            pl.BlockSpec((1, WIDTH), lambda i: (0, 0)),
            pl.BlockSpec((1, LANES), lambda i: (0, 0)),
        ],
        out_specs=[
            pl.BlockSpec((NHEADS, LANES, tb), lambda i: (0, 0, i)),
            pl.BlockSpec((NHEADS, tb, LANES), lambda i: (0, i, 0)),
            pl.BlockSpec((NHEADS, VT_ROWS, tb), lambda i: (0, 0, i)),
            pl.BlockSpec((1, 8, LANES), lambda i: (i, 0, 0)),
        ],
        out_shape=[
            jax.ShapeDtypeStruct((NHEADS, LANES, T), bf16),
            jax.ShapeDtypeStruct((NHEADS, T, LANES), bf16),
            jax.ShapeDtypeStruct((NHEADS, VT_ROWS, T), bf16),
            jax.ShapeDtypeStruct((T // tb, 8, LANES), f32),
        ],
        scratch_shapes=[pltpu.VMEM((8, LANES), f32)],
        compiler_params=pltpu.CompilerParams(
            dimension_semantics=("arbitrary",), vmem_limit_bytes=VMEM_LIMIT),
        name="fox_prep",
    )(z, z, z, z, gq, gk, bfp)


def _attn_kernel(work_ref, qt_ref, ka_ref, vt_ref, fg_ref, o_ref, m_ref, acc_ref, *, tq, tk):
    code = work_ref[pl.program_id(0) * pl.num_programs(1) + pl.program_id(1)]
    variant = ((code >> 16) & 0xF) - 1
    first = (code >> 20) & 1
    last = (code >> 21) & 1
    valid = (code >> 22) & 1

    @pl.when(first == 1)
    def _():
        m_ref[...] = jnp.full(m_ref.shape, NEG_BIG, f32)
        acc_ref[...] = jnp.zeros_like(acc_ref)

    def update(d):
        def n_keys(q_lo):
            return tk if d < 0 else max(0, min(tk, q_lo + Q_COL - d * tk))

        chains = [(hh, qs * Q_COL) for qs in range(tq // Q_COL) for hh in range(2) if n_keys(qs * Q_COL) > 0]

        def scores(hh, q_lo):
            nk = n_keys(q_lo)
            st = _dot(ka_ref[hh, 0:nk, :], qt_ref[hh, :, q_lo:q_lo + Q_COL])
            if d >= 0 and d * tk + nk - 1 > q_lo:
                kpos = d * tk + lax.broadcasted_iota(jnp.int32, st.shape, 0)
                qpos = q_lo + lax.broadcasted_iota(jnp.int32, st.shape, 1)
                st = jnp.where(kpos <= qpos, st, NEG_BIG)
            return st

        def consume(hh, q_lo, st):
            qsl = slice(q_lo, q_lo + Q_COL)
            m_old = m_ref[hh, :, qsl]
            m_new = jnp.maximum(m_old, jnp.max(st, axis=0, keepdims=True))
            m_ref[hh, :, qsl] = m_new
            p = jnp.exp2(st - m_new).astype(bf16)
            acc_ref[hh, :, qsl] = (jnp.exp2(m_old - m_new) * acc_ref[hh, :, qsl]
                                   + _dot(vt_ref[hh, :, 0:n_keys(q_lo)], p))

        ahead = 4
        pending = [scores(*c) for c in chains[:ahead]]
        for n, chain in enumerate(chains):
            st = pending.pop(0)
            if n + ahead < len(chains):
                pending.append(scores(*chains[n + ahead]))
            consume(*chain, st)

    for d in range(-1, tq // tk):
        @pl.when((valid == 1) & (variant == d))
        def _(d=d):
            update(d)

    @pl.when((valid == 1) & (last == 1))
    def _():
        outs = []
        for hh in range(2):
            acc = acc_ref[hh]
            outs.append(acc[0:HEAD, :] / acc[HEAD:HEAD + 1, :])
        y = jnp.concatenate(outs, axis=0).T
        g = fg_ref[...]
        o_ref[...] = (y * g * _sigmoid(g)).astype(o_ref.dtype)


def _attention_work_list(fstat, qk_bound, *, T, tb, tq, tk):
    r = tq // tk
    nq = T // tq
    qi = np.concatenate([np.full(r * (i + 1), i, np.int32) for i in range(nq)])
    kj = np.concatenate([np.arange(r * (i + 1), dtype=np.int32) for i in range(nq)])
    steps = len(qi)
    fmax = fstat[:, 0, :NHEADS]
    fmin = fstat[:, 1, :NHEADS]
    fmax_q = fmax.reshape(nq, tq // tb, NHEADS).max(axis=1)
    fmin_k = fmin.reshape(T // tk, tk // tb, NHEADS).min(axis=1)
    need = (fmax_q[qi] - fmin_k[kj] + 2.0 * qk_bound) > -SKIP_LOG2
    need = (need[:, 0::2] | need[:, 1::2]).T
    count = jnp.sum(need, axis=1, keepdims=True).astype(jnp.int32)
    order = jnp.argsort(jnp.logical_not(need), axis=1, stable=True).astype(jnp.int32)
    pos = jnp.arange(steps, dtype=jnp.int32)[None, :]
    sel = jnp.take_along_axis(order, jnp.minimum(pos, count - 1), axis=1)
    qs, ks = jnp.asarray(qi)[sel], jnp.asarray(kj)[sel]
    variant = jnp.maximum(ks - r * qs, -1)
    first = jnp.concatenate([jnp.ones((NPAIRS, 1), jnp.int32),
                             (qs[:, 1:] != qs[:, :-1]).astype(jnp.int32)], axis=1)
    last = (ks == r * qs + r - 1).astype(jnp.int32)
    valid = (pos < count).astype(jnp.int32)
    code = qs | (ks << 8) | ((variant + 1) << 16) | ((first & valid) << 20) | (last << 21) | (valid << 22)
    return code.reshape(-1).astype(jnp.int32), steps


def _attention(qt, ka, vt, z, fstat, qk_bound, *, tb, tq, tk):
    T = ka.shape[1]
    work, steps = _attention_work_list(fstat, qk_bound, T=T, tb=tb, tq=tq, tk=tk)
    fg_col = (Z_FX + 3 * WIDTH) // LANES
    qblk = lambda g, s, w: w[g * steps + s] & 0xFF
    kblk = lambda g, s, w: (w[g * steps + s] >> 8) & 0xFF
    grid_spec = pltpu.PrefetchScalarGridSpec(
        num_scalar_prefetch=1,
        grid=(NPAIRS, steps),
        in_specs=[
            pl.BlockSpec((2, LANES, tq), lambda g, s, w: (g, 0, qblk(g, s, w))),
            pl.BlockSpec((2, tk, LANES), lambda g, s, w: (g, kblk(g, s, w), 0)),
            pl.BlockSpec((2, VT_ROWS, tk), lambda g, s, w: (g, 0, kblk(g, s, w))),
            pl.BlockSpec((tq, LANES), lambda g, s, w: (qblk(g, s, w), fg_col + g)),
        ],
        out_specs=pl.BlockSpec((tq, LANES), lambda g, s, w: (qblk(g, s, w), g)),
        scratch_shapes=[pltpu.VMEM((2, 1, tq), f32), pltpu.VMEM((2, VT_ROWS, tq), f32)],
    )
    return pl.pallas_call(
        functools.partial(_attn_kernel, tq=tq, tk=tk),
        grid_spec=grid_spec,
        out_shape=jax.ShapeDtypeStruct((T, WIDTH), bf16),
        compiler_params=pltpu.CompilerParams(
            dimension_semantics=("parallel", "arbitrary"), vmem_limit_bytes=VMEM_LIMIT),
        name="fox_attention",
    )(work, qt, ka, vt, z)


P_MU_R, P_MU_K, P_MU_V, P_W0, P_A0, P_KK, P_KA, P_RK, P_LNW, P_LNB = range(10)


def _rwkv_kernel(zr_ref, zk_ref, zv_ref, zg_ref, zl_ref, par_ref, mul_ref, w2_ref, a2_ref,
                 o_ref, s_ref, prev_ref, prevl_ref, *, tb, npair):
    width = npair * LANES

    @pl.when(pl.program_id(1) == 0)
    def _():
        s_ref[...] = jnp.zeros_like(s_ref)
        prev_ref[...] = jnp.zeros_like(prev_ref)
        prevl_ref[...] = jnp.zeros_like(prevl_ref)

    par = par_ref[...]
    prow = lambda k: par[k:k + 1, :]

    def shift(x, mu, last_ref):
        row = lax.broadcasted_iota(jnp.int32, x.shape, 0)
        xp = pltpu.roll(x, 1, axis=0)
        xp = jnp.where(row == 0, last_ref[7:8, :], xp)
        last_ref[...] = x[tb - 8:tb, :]
        return x + (xp - x) * mu

    r = shift(zr_ref[...], prow(P_MU_R), prev_ref.at[0])
    k = shift(zk_ref[...], prow(P_MU_K), prev_ref.at[1])
    v = shift(zv_ref[...], prow(P_MU_V), prev_ref.at[2])
    sl = shift(zl_ref[...], mul_ref[0:1, :], prevl_ref)

    bi = lax.broadcasted_iota(jnp.int32, (LANES, LANES), 0)
    bj = lax.broadcasted_iota(jnp.int32, (LANES, LANES), 1)
    same_head = jnp.where((bi >> 6) == (bj >> 6), 1.0, 0.0).astype(bf16)

    def seg(x):
        return jnp.concatenate(
            [_dot(x[:, pr * LANES:(pr + 1) * LANES].astype(bf16), same_head) for pr in range(npair)], axis=1)

    u = prow(P_W0) + _dot(jnp.tanh(sl).astype(bf16), w2_ref[...])
    ld = -EXP_NEG_HALF * _sigmoid(u)
    av = _sigmoid(prow(P_A0) + _dot(sl.astype(bf16), a2_ref[...]))

    kk = k * prow(P_KK)
    kk = kk * lax.rsqrt(jnp.maximum(seg(kk * kk), 1e-24))
    k2 = k * (1.0 + (av - 1.0) * prow(P_KA))

    ti = lax.broadcasted_iota(jnp.int32, (tb, tb), 0)
    tj = lax.broadcasted_iota(jnp.int32, (tb, tb), 1)
    tri = jnp.where(((ti >> 6) == (tj >> 6)) & (tj <= ti), 1.0, 0.0).astype(bf16)
    ld_hi, ld_lo = _split2(ld)
    cum = _dot(tri, ld_hi) + _dot(tri, ld_lo)

    e_pos = jnp.exp(cum)
    e_neg = jnp.exp(-cum)
    a_t = -kk * jnp.exp(cum - ld)
    b_t = kk * av * e_neg
    k_t = k2 * e_neg
    r_t = r * e_pos

    ri = lax.broadcasted_iota(jnp.int32, (LANES, LANES), 0)
    rj = lax.broadcasted_iota(jnp.int32, (LANES, LANES), 1)
    strict_lower = rj < ri
    eye = jnp.where(ri == rj, 1.0, 0.0)
    qi = lax.broadcasted_iota(jnp.int32, (CHUNK, LANES), 0)
    qj = lax.broadcasted_iota(jnp.int32, (CHUNK, LANES), 1)
    incl_lower = (qj & (CHUNK - 1)) <= qi
    chunk_head0 = qj < HEAD

    def stack(x):
        return jnp.concatenate([jnp.where(chunk_head0, x, 0.0), jnp.where(chunk_head0, 0.0, x)], axis=0)

    def level_mask(b):
        sh = b.bit_length()
        return ((ri >> sh) == (rj >> sh)) & ((ri & (2 * b - 1)) >= b) & ((rj & (2 * b - 1)) < b)

    nchunk = tb // CHUNK
    units = [(pr, c) for pr in range(npair) for c in range(nchunk)]
    rows = lambda c: slice(c * CHUNK, (c + 1) * CHUNK)
    lanes = lambda pr: slice(pr * LANES, (pr + 1) * LANES)
    cut = lambda x, u: x[rows(u[1]), lanes(u[0])]
    cend = {u: cum[(u[1] + 1) * CHUNK - 1:(u[1] + 1) * CHUNK, lanes(u[0])] for u in units}

    am = {u: stack(cut(a_t, u)) for u in units}
    vm = {u: stack(cut(v, u)) for u in units}
    x = {u: _dot_nt(jnp.concatenate([am[u], cut(r_t, u)], axis=0).astype(bf16),
                    jnp.concatenate([stack(cut(b_t, u)), stack(cut(k_t, u))], axis=0).astype(bf16))
         for u in units}
    l_ab = {u: jnp.where(strict_lower, x[u][0:LANES, 0:LANES], 0.0) for u in units}
    l_ak = {u: jnp.where(strict_lower, x[u][0:LANES, LANES:], 0.0) for u in units}
    l_r = {u: jnp.where(jnp.concatenate([incl_lower, incl_lower], axis=1), x[u][LANES:, :], 0.0).astype(bf16)
           for u in units}

    inv = {u: eye + jnp.where(level_mask(1), l_ab[u], 0.0) for u in units}
    b = 2
    while b < CHUNK:
        mask = level_mask(b)
        y_ = {u: _dot(jnp.where(mask, l_ab[u], 0.0).astype(bf16), inv[u].astype(bf16)) for u in units}
        inv = {u: inv[u] + _dot(inv[u].astype(bf16), y_[u].astype(bf16)) for u in units}
        b *= 2

    lv = {u: _dot(l_ak[u].astype(bf16), vm[u].astype(bf16)) for u in units}
    tg = {u: _dot(inv[u].astype(bf16), jnp.concatenate([am[u], lv[u]], axis=1).astype(bf16)) for u in units}
    zeros = jnp.zeros((LANES, LANES), f32)
    ly = {u: _dot(l_r[u], jnp.concatenate(
              [tg[u], jnp.concatenate([zeros, vm[u]], axis=1)], axis=0).astype(bf16)) for u in units}
    bh = {u: stack(cut(kk, u) * cut(av, u) * jnp.exp(cend[u] - cut(cum, u))).astype(bf16) for u in units}
    kh = {u: stack(cut(k2, u) * jnp.exp(cend[u] - cut(cum, u))).astype(bf16) for u in units}
    ag = {u: _dot_tn(tg[u].astype(bf16), bh[u]) for u in units}
    vk = {u: _dot_tn(vm[u].astype(bf16), kh[u]) for u in units}
    q_eff = {u: (cut(r_t, u) + ly[u][:, 0:LANES]).astype(bf16) for u in units}
    a_eff = {u: ag[u][0:LANES, :].astype(bf16) for u in units}
    g_eff = {u: ag[u][LANES:, :] + vk[u] for u in units}

    ys = [[None] * npair for _ in range(nchunk)]
    state = [s_ref[pr] for pr in range(npair)]
    for c in range(nchunk):
        for pr in range(npair):
            u = (pr, c)
            s_bf = state[pr].astype(bf16)
            ys[c][pr] = _dot_nt(q_eff[u], s_bf) + ly[u][:, LANES:]
            state[pr] = state[pr] * jnp.exp(cend[u]) + _dot(s_bf, a_eff[u]) + g_eff[u]
    for pr in range(npair):
        s_ref[pr] = state[pr]

    y = jnp.concatenate([jnp.concatenate(yc, axis=1) for yc in ys], axis=0)
    mean = seg(y) * (1.0 / HEAD)
    d = y - mean
    var = seg(d * d) * (1.0 / HEAD)
    yn = d * lax.rsqrt(var + GN_EPS) * prow(P_LNW) + prow(P_LNB)
    bonus = seg(r * k2 * prow(P_RK)) * v
    g = zg_ref[...]
    o_ref[...] = ((yn + bonus) * g * _sigmoid(g)).astype(o_ref.dtype)


def _rwkv(z, par, mul, w2p, a2p, *, tb, npair):
    T = z.shape[0]
    width = npair * LANES
    nb = WIDTH // width
    zspec = lambda off: pl.BlockSpec((tb, width), lambda g, t: (t, off + g))
    return pl.pallas_call(
        functools.partial(_rwkv_kernel, tb=tb, npair=npair),
        grid=(nb, T // tb),
        in_specs=[
            zspec(0), zspec(nb), zspec(2 * nb), zspec(3 * nb),
            pl.BlockSpec((tb, LANES), lambda g, t: (t, Z_LORA // LANES)),
            pl.BlockSpec((16, width), lambda g, t: (0, g)),
            pl.BlockSpec((8, LANES), lambda g, t: (0, 0)),
            pl.BlockSpec((LANES, width), lambda g, t: (0, g)),
            pl.BlockSpec((LANES, width), lambda g, t: (0, g)),
        ],
        out_specs=pl.BlockSpec((tb, width), lambda g, t: (t, g)),
        out_shape=jax.ShapeDtypeStruct((T, WIDTH), bf16),
        scratch_shapes=[pltpu.VMEM((npair, LANES, LANES), f32), pltpu.VMEM((3, 8, width), f32),
                        pltpu.VMEM((8, LANES), f32)],
        compiler_params=pltpu.CompilerParams(
            dimension_semantics=("parallel", "arbitrary"), vmem_limit_bytes=VMEM_LIMIT),
        name="rwkv7",
    )(z, z, z, z, z, par, mul, w2p, a2p)


def _out_kernel(x_ref, yr_ref, yf_ref, p_ref, wo_ref, wg_ref, wp_ref, g1_ref, g2_ref, o_ref):
    def rms(t, g):
        return t * lax.rsqrt(jnp.mean(t * t, axis=-1, keepdims=True) + RMS_EPS) * g

    m = _dot(yr_ref[...], wo_ref[0:WIDTH, :]) + _dot(yf_ref[...], wo_ref[WIDTH:, :])
    x1 = x_ref[...] + rms(m, g1_ref[...])
    gate = _sigmoid(_dot(rms(x1, g2_ref[...]).astype(bf16), wg_ref[...]))
    o_ref[...] = x1 + gate * _dot(p_ref[...].astype(bf16), wp_ref[...])


def _out_proj(x, yr, yf, p, wo, wg, wp, g1, g2, *, tm):
    T = x.shape[0]
    const = lambda shape: pl.BlockSpec(shape, lambda i: (0, 0))
    return pl.pallas_call(
        _out_kernel,
        grid=(T // tm,),
        in_specs=[
            pl.BlockSpec((tm, D_MODEL), lambda i: (i, 0)),
            pl.BlockSpec((tm, WIDTH), lambda i: (i, 0)),
            pl.BlockSpec((tm, WIDTH), lambda i: (i, 0)),
            pl.BlockSpec((tm, D_PLE), lambda i: (i, 0)),
            const((D_MODEL, D_MODEL)), const((D_MODEL, D_MODEL)), const((D_PLE, D_MODEL)),
            const((1, D_MODEL)), const((1, D_MODEL)),
        ],
        out_specs=pl.BlockSpec((tm, D_MODEL), lambda i: (i, 0)),
        out_shape=jax.ShapeDtypeStruct((T, D_MODEL), f32),
        compiler_params=pltpu.CompilerParams(
            dimension_semantics=("parallel",), vmem_limit_bytes=VMEM_LIMIT),
        name="out_proj",
    )(x, yr, yf, p, wo, wg, wp, g1, g2)


def _layer(x, p, pre_g, w_in, mu_r, mu_k, mu_v, mu_w, mu_a, w0, w2, a0, a2, k_k, k_a, r_k, ln_w, ln_b,
           b_f, q_g, k_g, w_out, post_g, ple_g, w_gate, w_ple):
    T = x.shape[0]
    rw_end = 4 * WIDTH
    lora_end = rw_end + 2 * LORA
    fx_end = lora_end + 4 * WIDTH
    perm = jnp.argsort(b_f)
    by_head = lambda w_, axis: jnp.take(w_, perm, axis=axis)
    w_fx = by_head(w_in[:, lora_end:fx_end].reshape(D_MODEL, 4, NHEADS, HEAD), 2).reshape(D_MODEL, 4 * WIDTH)
    w = jnp.concatenate([
        w_in[:, :rw_end], w_fx, w_in[:, rw_end:lora_end], by_head(w_in[:, fx_end:], 1),
        jnp.zeros((D_MODEL, Z_COLS - w_in.shape[1]), w_in.dtype)], axis=1).astype(bf16)
    w_out = jnp.concatenate(
        [w_out[:WIDTH], by_head(w_out[WIDTH:].reshape(NHEADS, HEAD, D_MODEL), 0).reshape(WIDTH, D_MODEL)], axis=0)
    b_f = by_head(b_f, 0)
    z = _in_proj(x, pre_g.reshape(1, D_MODEL), w, tm=min(1024, T), tn=1408)

    gq = jnp.tile(q_g, NHEADS).reshape(1, WIDTH)
    gk = jnp.tile(k_g, NHEADS).reshape(1, WIDTH)
    bfp = jnp.pad(b_f, (0, LANES - NHEADS)).reshape(1, LANES)
    tb = min(256, T)
    qt, ka, vt, fstat = _fox_prep(z, gq, gk, bfp, tb=tb)
    qk_bound = 1.02 * HEAD ** 0.5 * LOG2E * jnp.max(jnp.abs(q_g)) * jnp.max(jnp.abs(k_g))
    y_fx = _attention(qt, ka, vt, z, fstat, qk_bound, tb=tb, tq=min(1024, T), tk=min(512, T))

    par = jnp.stack([mu_r, mu_k, mu_v, w0, a0, k_k, k_a, r_k.reshape(WIDTH), ln_w, ln_b])
    par = jnp.pad(par, ((0, 16 - par.shape[0]), (0, 0)))
    mul = jnp.broadcast_to(jnp.concatenate([mu_w, mu_a]).reshape(1, LANES), (8, LANES))
    zeros = jnp.zeros((LORA, WIDTH), f32)
    w2p = jnp.concatenate([w2, zeros], axis=0).astype(bf16)
    a2p = jnp.concatenate([zeros, a2], axis=0).astype(bf16)
    y_rw = _rwkv(z, par, mul, w2p, a2p, tb=min(256, T), npair=4)

    return _out_proj(x, y_rw, y_fx, p, w_out.astype(bf16), w_gate.astype(bf16), w_ple.astype(bf16),
                     post_g.reshape(1, D_MODEL), ple_g.reshape(1, D_MODEL), tm=min(256, T))


def kernel(x, p, pre_norm_g, w_in, rw_mu_r, rw_mu_k, rw_mu_v, rw_mu_w, rw_mu_a, rw_w0, rw_w2, rw_a0, rw_a2,
           rw_k_k, rw_k_a, rw_r_k, rw_ln_w, rw_ln_b, fx_b_f, fx_q_g, fx_k_g, w_out, post_norm_g, ple_norm_g,
           w_ple_gate, w_ple):
    B = x.shape[0]
    outs = []
    for b in range(B):
        xb = x[b]
        for i in range(p.shape[0]):
            xb = _layer(xb, p[i, b], pre_norm_g[i], w_in[i], rw_mu_r[i], rw_mu_k[i], rw_mu_v[i], rw_mu_w[i],
                        rw_mu_a[i], rw_w0[i], rw_w2[i], rw_a0[i], rw_a2[i], rw_k_k[i], rw_k_a[i], rw_r_k[i],
                        rw_ln_w[i], rw_ln_b[i], fx_b_f[i], fx_q_g[i], fx_k_g[i], w_out[i], post_norm_g[i],
                        ple_norm_g[i], w_ple_gate[i], w_ple[i])
        outs.append(xb)
    return jnp.stack(outs)
```

```python
import functools

import numpy as np
import jax
import jax.numpy as jnp
from jax import lax
from jax.experimental import pallas as pl
from jax.experimental.pallas import tpu as pltpu

f32 = jnp.float32
bf16 = jnp.bfloat16

D_MODEL = 2048
D_PLE = 256
WIDTH = 1024
HEAD = 64
NHEADS = 16
NPAIRS = NHEADS // 2
LORA = 64
LANES = 128
RMS_EPS = 1e-6
GN_EPS = 64e-5
CHUNK = 64
LOG2E = 1.4426950408889634
EXP_NEG_HALF = 0.6065306597126334
NEG_BIG = -1e30

Z_RW = 0
Z_FX = 4096
Z_LORA = 8192
Z_FF = 8320
Z_COLS = 8448
VT_ROWS = 80
Q_COL = 256
SKIP_LOG2 = 48.0

VMEM_LIMIT = 56 * 1024 * 1024


def _dot(a, b):
    return jnp.dot(a, b, preferred_element_type=f32)


def _dot_nt(a, b):
    return lax.dot_general(a, b, (((1,), (1,)), ((), ())), preferred_element_type=f32)


def _dot_tn(a, b):
    return lax.dot_general(a, b, (((0,), (0,)), ((), ())), preferred_element_type=f32)


def _split2(x):
    hi = x.astype(bf16)
    lo = (x - hi.astype(f32)).astype(bf16)
    return hi, lo


def _split3(x):
    hi = x.astype(bf16)
    r = x - hi.astype(f32)
    mid = r.astype(bf16)
    lo = (r - mid.astype(f32)).astype(bf16)
    return hi, mid, lo


def _dot_sel(x, sel):
    hi, mid, lo = _split3(x)
    return _dot(hi, sel) + _dot(mid, sel) + _dot(lo, sel)


def _sel_dot(sel, x):
    hi, mid, lo = _split3(x)
    return _dot(sel, hi) + _dot(sel, mid) + _dot(sel, lo)


def _dot_x3(a, b):
    ah, al = _split2(a)
    bh, bl = _split2(b)
    return _dot(ah, bh) + _dot(al, bh) + _dot(ah, bl)


def _sigmoid(x):
    return 1.0 / (1.0 + jnp.exp(-x))


def _relayout_kernel(src_ref, a_ref, b_ref, ff_ref, o_ref):
    c = pl.program_id(0)
    half_a = src_ref[2 * c]
    half_b = src_ref[2 * c + 1]
    a, b = a_ref[...], b_ref[...]
    a = jnp.where(half_a % 2 == 1, pltpu.roll(a, HEAD, axis=1), a)
    b = jnp.where(half_b % 2 == 0, pltpu.roll(b, HEAD, axis=1), b)
    lane = lax.broadcasted_iota(jnp.int32, a.shape, 1)
    out = jnp.where(lane < HEAD, a, b)
    out = jnp.where(c == pl.num_programs(0) - 1, ff_ref[...], out)
    o_ref[...] = out.astype(bf16)


def _relayout_w_in(src_half, w_in, w_ff):
    nblk = Z_COLS // LANES
    grid_spec = pltpu.PrefetchScalarGridSpec(
        num_scalar_prefetch=1,
        grid=(nblk,),
        in_specs=[
            pl.BlockSpec((D_MODEL, LANES), lambda c, src: (0, src[2 * c] // 2)),
            pl.BlockSpec((D_MODEL, LANES), lambda c, src: (0, src[2 * c + 1] // 2)),
            pl.BlockSpec((D_MODEL, LANES), lambda c, src: (0, 0)),
        ],
        out_specs=pl.BlockSpec((D_MODEL, LANES), lambda c, src: (0, c)),
    )
    return pl.pallas_call(
        _relayout_kernel,
        grid_spec=grid_spec,
        out_shape=jax.ShapeDtypeStruct((D_MODEL, Z_COLS), bf16),
        compiler_params=pltpu.CompilerParams(
            dimension_semantics=("parallel",), vmem_limit_bytes=VMEM_LIMIT),
        name="w_in_relayout",
    )(src_half, w_in, w_in, w_ff)


def _in_proj_kernel(x_ref, g_ref, w_ref, z_ref, h_ref):
    @pl.when(pl.program_id(1) == 0)
    def _():
        x = x_ref[...]
        ms = jnp.mean(x * x, axis=-1, keepdims=True)
        h_ref[...] = (x * lax.rsqrt(ms + RMS_EPS) * g_ref[...]).astype(bf16)

    z_ref[...] = _dot(h_ref[...], w_ref[...])


def _in_proj(x, g, w, *, tm, tn):
    T = x.shape[0]
    return pl.pallas_call(
        _in_proj_kernel,
        grid=(T // tm, Z_COLS // tn),
        in_specs=[
            pl.BlockSpec((tm, D_MODEL), lambda i, j: (i, 0)),
            pl.BlockSpec((1, D_MODEL), lambda i, j: (0, 0)),
            pl.BlockSpec((D_MODEL, tn), lambda i, j: (0, j)),
        ],
        out_specs=pl.BlockSpec((tm, tn), lambda i, j: (i, j)),
        out_shape=jax.ShapeDtypeStruct((T, Z_COLS), f32),
        scratch_shapes=[pltpu.VMEM((tm, D_MODEL), bf16)],
        compiler_params=pltpu.CompilerParams(
            dimension_semantics=("parallel", "arbitrary"), vmem_limit_bytes=VMEM_LIMIT),
        name="in_proj",
    )(x, g, w)


def _fox_prep_kernel(fq_ref, fk_ref, fv_ref, ff_ref, gq_ref, gk_ref, bf_ref,
                     qt_ref, ka_ref, vt_ref, fs_ref, carry_ref, *, tb):
    @pl.when(pl.program_id(0) == 0)
    def _():
        carry_ref[...] = jnp.zeros_like(carry_ref)

    bi = lax.broadcasted_iota(jnp.int32, (LANES, LANES), 0)
    bj = lax.broadcasted_iota(jnp.int32, (LANES, LANES), 1)
    same_head = jnp.where((bi >> 6) == (bj >> 6), 1.0, 0.0).astype(bf16)

    def head_rms(x, gain):
        cols = []
        for c in range(WIDTH // LANES):
            xc = x[:, c * LANES:(c + 1) * LANES]
            ssq = _dot((xc * xc).astype(bf16), same_head)
            cols.append(xc * lax.rsqrt(ssq * (1.0 / HEAD) + RMS_EPS))
        return jnp.concatenate(cols, axis=1) * gain

    qn = head_rms(fq_ref[...], gq_ref[...]) * (HEAD ** -0.5 * LOG2E)
    kn = head_rms(fk_ref[...], gk_ref[...])

    xf = ff_ref[...] + bf_ref[...]
    logf = jnp.minimum(xf, 0.0) - jnp.log(1.0 + jnp.exp(-jnp.abs(xf)))
    ri = lax.broadcasted_iota(jnp.int32, (tb, tb), 0)
    rj = lax.broadcasted_iota(jnp.int32, (tb, tb), 1)
    tri = jnp.where(rj <= ri, 1.0, 0.0).astype(bf16)
    cum = _sel_dot(tri, logf) + carry_ref[0:1, :]
    carry_ref[...] = jnp.broadcast_to(cum[tb - 1:tb, :], carry_ref.shape)
    cum2 = cum * LOG2E
    c_hi, c_mid, c_lo = (t.astype(f32) for t in _split3(cum2))
    srow = lax.broadcasted_iota(jnp.int32, (8, LANES), 0)
    fs_ref[0] = jnp.where(srow == 0, jnp.max(cum2, axis=0, keepdims=True), jnp.min(cum2, axis=0, keepdims=True))

    lane = lax.broadcasted_iota(jnp.int32, (tb, LANES), 1)
    low = lane < HEAD
    vt = fv_ref[...].T
    vpad = jnp.concatenate(
        [jnp.ones((1, tb), f32), jnp.zeros((VT_ROWS - HEAD - 1, tb), f32)], axis=0)
    for h in range(NHEADS):
        col = slice((h // 2) * LANES, (h // 2 + 1) * LANES)
        qc, kc = qn[:, col], kn[:, col]
        if h % 2 == 1:
            qc, kc = pltpu.roll(qc, HEAD, axis=1), pltpu.roll(kc, HEAD, axis=1)
        fh, fm, fl = c_hi[:, h:h + 1], c_mid[:, h:h + 1], c_lo[:, h:h + 1]
        eq = jnp.where(lane == HEAD, fh, jnp.where(lane == HEAD + 1, fm, jnp.where(lane == HEAD + 2, fl,
             jnp.where(lane < HEAD + 6, 1.0, 0.0))))
        ek = jnp.where(lane == HEAD + 3, -fh, jnp.where(lane == HEAD + 4, -fm, jnp.where(lane == HEAD + 5, -fl,
             jnp.where(lane < HEAD + 3, 1.0, 0.0))))
        qt_ref[h] = jnp.where(low, qc, eq).T.astype(bf16)
        ka_ref[h] = jnp.where(low, kc, ek).astype(bf16)
        vt_ref[h] = jnp.concatenate([vt[h * HEAD:(h + 1) * HEAD, :], vpad], axis=0).astype(bf16)


def _fox_prep(z, gq, gk, bfp, *, tb):
    T = z.shape[0]
    wb = Z_FX // WIDTH
    return pl.pallas_call(
        functools.partial(_fox_prep_kernel, tb=tb),
        grid=(T // tb,),
        in_specs=[
            pl.BlockSpec((tb, WIDTH), lambda i: (i, wb)),
            pl.BlockSpec((tb, WIDTH), lambda i: (i, wb + 1)),
            pl.BlockSpec((tb, WIDTH), lambda i: (i, wb + 2)),
            pl.BlockSpec((tb, LANES), lambda i: (i, Z_FF // LANES)),
            pl.BlockSpec((1, WIDTH), lambda i: (0, 0)),
            pl.BlockSpec((1, WIDTH), lambda i: (0, 0)),
            pl.BlockSpec((1, LANES), lambda i: (0, 0)),
        ],
        out_specs=[
            pl.BlockSpec((NHEADS, LANES, tb), lambda i: (0, 0, i)),
            pl.BlockSpec((NHEADS, tb, LANES), lambda i: (0, i, 0)),
            pl.BlockSpec((NHEADS, VT_ROWS, tb), lambda i: (0, 0, i)),
            pl.BlockSpec((1, 8, LANES), lambda i: (i, 0, 0)),
        ],
        out_shape=[
            jax.ShapeDtypeStruct((NHEADS, LANES, T), bf16),
            jax.ShapeDtypeStruct((NHEADS, T, LANES), bf16),
            jax.ShapeDtypeStruct((NHEADS, VT_ROWS, T), bf16),
            jax.ShapeDtypeStruct((T // tb, 8, LANES), f32),
        ],
        scratch_shapes=[pltpu.VMEM((8, LANES), f32)],
        compiler_params=pltpu.CompilerParams(
            dimension_semantics=("arbitrary",), vmem_limit_bytes=VMEM_LIMIT),
        name="fox_prep",
    )(z, z, z, z, gq, gk, bfp)


def _attn_kernel(work_ref, qt_ref, ka_ref, vt_ref, fg_ref, o_ref, m_ref, acc_ref, *, tq, tk):
    code = work_ref[pl.program_id(0) * pl.num_programs(1) + pl.program_id(1)]
    variant = ((code >> 16) & 0xF) - 1
    first = (code >> 20) & 1
    last = (code >> 21) & 1
    valid = (code >> 22) & 1

    @pl.when(first == 1)
    def _():
        m_ref[...] = jnp.full(m_ref.shape, NEG_BIG, f32)
        acc_ref[...] = jnp.zeros_like(acc_ref)

    def update(d):
        def n_keys(q_lo):
            return tk if d < 0 else max(0, min(tk, q_lo + Q_COL - d * tk))

        chains = [(hh, qs * Q_COL) for qs in range(tq // Q_COL) for hh in range(2) if n_keys(qs * Q_COL) > 0]

        def scores(hh, q_lo):
            nk = n_keys(q_lo)
            st = _dot(ka_ref[hh, 0:nk, :], qt_ref[hh, :, q_lo:q_lo + Q_COL])
            if d >= 0 and d * tk + nk - 1 > q_lo:
                kpos = d * tk + lax.broadcasted_iota(jnp.int32, st.shape, 0)
                qpos = q_lo + lax.broadcasted_iota(jnp.int32, st.shape, 1)
                st = jnp.where(kpos <= qpos, st, NEG_BIG)
            return st

        def consume(hh, q_lo, st):
            qsl = slice(q_lo, q_lo + Q_COL)
            m_old = m_ref[hh, :, qsl]
            m_new = jnp.maximum(m_old, jnp.max(st, axis=0, keepdims=True))
            m_ref[hh, :, qsl] = m_new
            p = jnp.exp2(st - m_new).astype(bf16)
            acc_ref[hh, :, qsl] = (jnp.exp2(m_old - m_new) * acc_ref[hh, :, qsl]
                                   + _dot(vt_ref[hh, :, 0:n_keys(q_lo)], p))

        ahead = 4
        pending = [scores(*c) for c in chains[:ahead]]
        for n, chain in enumerate(chains):
            st = pending.pop(0)
            if n + ahead < len(chains):
                pending.append(scores(*chains[n + ahead]))
            consume(*chain, st)

    for d in range(-1, tq // tk):
        @pl.when((valid == 1) & (variant == d))
        def _(d=d):
            update(d)

    @pl.when((valid == 1) & (last == 1))
    def _():
        outs = []
        for hh in range(2):
            acc = acc_ref[hh]
            outs.append(acc[0:HEAD, :] / acc[HEAD:HEAD + 1, :])
        y = jnp.concatenate(outs, axis=0).T
        g = fg_ref[...]
        o_ref[...] = (y * g * _sigmoid(g)).astype(o_ref.dtype)


def _attention_work_list(fstat, qk_bound, *, T, tb, tq, tk):
    r = tq // tk
    nq = T // tq
    qi = np.concatenate([np.full(r * (i + 1), i, np.int32) for i in range(nq)])
    kj = np.concatenate([np.arange(r * (i + 1), dtype=np.int32) for i in range(nq)])
    steps = len(qi)
    fmax = fstat[:, 0, :NHEADS]
    fmin = fstat[:, 1, :NHEADS]
    fmax_q = fmax.reshape(nq, tq // tb, NHEADS).max(axis=1)
    fmin_k = fmin.reshape(T // tk, tk // tb, NHEADS).min(axis=1)
    need = (fmax_q[qi] - fmin_k[kj] + 2.0 * qk_bound) > -SKIP_LOG2
    need = (need[:, 0::2] | need[:, 1::2]).T
    count = jnp.sum(need, axis=1, keepdims=True).astype(jnp.int32)
    order = jnp.argsort(jnp.logical_not(need), axis=1, stable=True).astype(jnp.int32)
    pos = jnp.arange(steps, dtype=jnp.int32)[None, :]
    sel = jnp.take_along_axis(order, jnp.minimum(pos, count - 1), axis=1)
    qs, ks = jnp.asarray(qi)[sel], jnp.asarray(kj)[sel]
    variant = jnp.maximum(ks - r * qs, -1)
    first = jnp.concatenate([jnp.ones((NPAIRS, 1), jnp.int32),
                             (qs[:, 1:] != qs[:, :-1]).astype(jnp.int32)], axis=1)
    last = (ks == r * qs + r - 1).astype(jnp.int32)
    valid = (pos < count).astype(jnp.int32)
    code = qs | (ks << 8) | ((variant + 1) << 16) | ((first & valid) << 20) | (last << 21) | (valid << 22)
    return code.reshape(-1).astype(jnp.int32), steps


def _attention(qt, ka, vt, z, fstat, qk_bound, *, tb, tq, tk):
    T = ka.shape[1]
    work, steps = _attention_work_list(fstat, qk_bound, T=T, tb=tb, tq=tq, tk=tk)
    fg_col = (Z_FX + 3 * WIDTH) // LANES
    qblk = lambda g, s, w: w[g * steps + s] & 0xFF
    kblk = lambda g, s, w: (w[g * steps + s] >> 8) & 0xFF
    grid_spec = pltpu.PrefetchScalarGridSpec(
        num_scalar_prefetch=1,
        grid=(NPAIRS, steps),
        in_specs=[
            pl.BlockSpec((2, LANES, tq), lambda g, s, w: (g, 0, qblk(g, s, w))),
            pl.BlockSpec((2, tk, LANES), lambda g, s, w: (g, kblk(g, s, w), 0)),
            pl.BlockSpec((2, VT_ROWS, tk), lambda g, s, w: (g, 0, kblk(g, s, w))),
            pl.BlockSpec((tq, LANES), lambda g, s, w: (qblk(g, s, w), fg_col + g)),
        ],
        out_specs=pl.BlockSpec((tq, LANES), lambda g, s, w: (qblk(g, s, w), g)),
        scratch_shapes=[pltpu.VMEM((2, 1, tq), f32), pltpu.VMEM((2, VT_ROWS, tq), f32)],
    )
    return pl.pallas_call(
        functools.partial(_attn_kernel, tq=tq, tk=tk),
        grid_spec=grid_spec,
        out_shape=jax.ShapeDtypeStruct((T, WIDTH), bf16),
        compiler_params=pltpu.CompilerParams(
            dimension_semantics=("parallel", "arbitrary"), vmem_limit_bytes=VMEM_LIMIT),
        name="fox_attention",
    )(work, qt, ka, vt, z)


P_MU_R, P_MU_K, P_MU_V, P_W0, P_A0, P_KK, P_KA, P_RK, P_LNW, P_LNB = range(10)


def _rwkv_kernel(zr_ref, zk_ref, zv_ref, zg_ref, zl_ref, par_ref, mul_ref, w2_ref, a2_ref,
                 o_ref, s_ref, prev_ref, prevl_ref, *, tb, npair):
    width = npair * LANES

    @pl.when(pl.program_id(1) == 0)
    def _():
        s_ref[...] = jnp.zeros_like(s_ref)
        prev_ref[...] = jnp.zeros_like(prev_ref)
        prevl_ref[...] = jnp.zeros_like(prevl_ref)

    par = par_ref[...]
    prow = lambda k: par[k:k + 1, :]

    def shift(x, mu, last_ref):
        row = lax.broadcasted_iota(jnp.int32, x.shape, 0)
        xp = pltpu.roll(x, 1, axis=0)
        xp = jnp.where(row == 0, last_ref[7:8, :], xp)
        last_ref[...] = x[tb - 8:tb, :]
        return x + (xp - x) * mu

    r = shift(zr_ref[...], prow(P_MU_R), prev_ref.at[0])
    k = shift(zk_ref[...], prow(P_MU_K), prev_ref.at[1])
    v = shift(zv_ref[...], prow(P_MU_V), prev_ref.at[2])
    sl = shift(zl_ref[...], mul_ref[0:1, :], prevl_ref)

    bi = lax.broadcasted_iota(jnp.int32, (LANES, LANES), 0)
    bj = lax.broadcasted_iota(jnp.int32, (LANES, LANES), 1)
    same_head = jnp.where((bi >> 6) == (bj >> 6), 1.0, 0.0).astype(bf16)

    def seg(x):
        return jnp.concatenate(
            [_dot(x[:, pr * LANES:(pr + 1) * LANES].astype(bf16), same_head) for pr in range(npair)], axis=1)

    u = prow(P_W0) + _dot(jnp.tanh(sl).astype(bf16), w2_ref[...])
    ld = -EXP_NEG_HALF * _sigmoid(u)
    av = _sigmoid(prow(P_A0) + _dot(sl.astype(bf16), a2_ref[...]))

    kk = k * prow(P_KK)
    kk = kk * lax.rsqrt(jnp.maximum(seg(kk * kk), 1e-24))
    k2 = k * (1.0 + (av - 1.0) * prow(P_KA))

    ti = lax.broadcasted_iota(jnp.int32, (tb, tb), 0)
    tj = lax.broadcasted_iota(jnp.int32, (tb, tb), 1)
    tri = jnp.where(((ti >> 6) == (tj >> 6)) & (tj <= ti), 1.0, 0.0).astype(bf16)
    ld_hi, ld_lo = _split2(ld)
    cum = _dot(tri, ld_hi) + _dot(tri, ld_lo)

    e_pos = jnp.exp(cum)
    e_neg = jnp.exp(-cum)
    a_t = -kk * jnp.exp(cum - ld)
    b_t = kk * av * e_neg
    k_t = k2 * e_neg
    r_t = r * e_pos

    ri = lax.broadcasted_iota(jnp.int32, (LANES, LANES), 0)
    rj = lax.broadcasted_iota(jnp.int32, (LANES, LANES), 1)
    strict_lower = rj < ri
    eye = jnp.where(ri == rj, 1.0, 0.0)
    qi = lax.broadcasted_iota(jnp.int32, (CHUNK, LANES), 0)
    qj = lax.broadcasted_iota(jnp.int32, (CHUNK, LANES), 1)
    incl_lower = (qj & (CHUNK - 1)) <= qi
    chunk_head0 = qj < HEAD

    def stack(x):
        return jnp.concatenate([jnp.where(chunk_head0, x, 0.0), jnp.where(chunk_head0, 0.0, x)], axis=0)

    def level_mask(b):
        sh = b.bit_length()
        return ((ri >> sh) == (rj >> sh)) & ((ri & (2 * b - 1)) >= b) & ((rj & (2 * b - 1)) < b)

    nchunk = tb // CHUNK
    units = [(pr, c) for pr in range(npair) for c in range(nchunk)]
    rows = lambda c: slice(c * CHUNK, (c + 1) * CHUNK)
    lanes = lambda pr: slice(pr * LANES, (pr + 1) * LANES)
    cut = lambda x, u: x[rows(u[1]), lanes(u[0])]
    cend = {u: cum[(u[1] + 1) * CHUNK - 1:(u[1] + 1) * CHUNK, lanes(u[0])] for u in units}

    am = {u: stack(cut(a_t, u)) for u in units}
    vm = {u: stack(cut(v, u)) for u in units}
    x = {u: _dot_nt(jnp.concatenate([am[u], cut(r_t, u)], axis=0).astype(bf16),
                    jnp.concatenate([stack(cut(b_t, u)), stack(cut(k_t, u))], axis=0).astype(bf16))
         for u in units}
    l_ab = {u: jnp.where(strict_lower, x[u][0:LANES, 0:LANES], 0.0) for u in units}
    l_ak = {u: jnp.where(strict_lower, x[u][0:LANES, LANES:], 0.0) for u in units}
    l_r = {u: jnp.where(jnp.concatenate([incl_lower, incl_lower], axis=1), x[u][LANES:, :], 0.0).astype(bf16)
           for u in units}

    inv = {u: eye + jnp.where(level_mask(1), l_ab[u], 0.0) for u in units}
    b = 2
    while b < CHUNK:
        mask = level_mask(b)
        y_ = {u: _dot(jnp.where(mask, l_ab[u], 0.0).astype(bf16), inv[u].astype(bf16)) for u in units}
        inv = {u: inv[u] + _dot(inv[u].astype(bf16), y_[u].astype(bf16)) for u in units}
        b *= 2

    lv = {u: _dot(l_ak[u].astype(bf16), vm[u].astype(bf16)) for u in units}
    tg = {u: _dot(inv[u].astype(bf16), jnp.concatenate([am[u], lv[u]], axis=1).astype(bf16)) for u in units}
    zeros = jnp.zeros((LANES, LANES), f32)
    ly = {u: _dot(l_r[u], jnp.concatenate(
              [tg[u], jnp.concatenate([zeros, vm[u]], axis=1)], axis=0).astype(bf16)) for u in units}
    bh = {u: stack(cut(kk, u) * cut(av, u) * jnp.exp(cend[u] - cut(cum, u))).astype(bf16) for u in units}
    kh = {u: stack(cut(k2, u) * jnp.exp(cend[u] - cut(cum, u))).astype(bf16) for u in units}
    ag = {u: _dot_tn(tg[u].astype(bf16), bh[u]) for u in units}
    vk = {u: _dot_tn(vm[u].astype(bf16), kh[u]) for u in units}
    q_eff = {u: (cut(r_t, u) + ly[u][:, 0:LANES]).astype(bf16) for u in units}
    a_eff = {u: ag[u][0:LANES, :].astype(bf16) for u in units}
    g_eff = {u: ag[u][LANES:, :] + vk[u] for u in units}

    ys = [[None] * npair for _ in range(nchunk)]
    state = [s_ref[pr] for pr in range(npair)]
    for c in range(nchunk):
        for pr in range(npair):
            u = (pr, c)
            s_bf = state[pr].astype(bf16)
            ys[c][pr] = _dot_nt(q_eff[u], s_bf) + ly[u][:, LANES:]
            state[pr] = state[pr] * jnp.exp(cend[u]) + _dot(s_bf, a_eff[u]) + g_eff[u]
    for pr in range(npair):
        s_ref[pr] = state[pr]

    y = jnp.concatenate([jnp.concatenate(yc, axis=1) for yc in ys], axis=0)
    mean = seg(y) * (1.0 / HEAD)
    d = y - mean
    var = seg(d * d) * (1.0 / HEAD)
    yn = d * lax.rsqrt(var + GN_EPS) * prow(P_LNW) + prow(P_LNB)
    bonus = seg(r * k2 * prow(P_RK)) * v
    g = zg_ref[...]
    o_ref[...] = ((yn + bonus) * g * _sigmoid(g)).astype(o_ref.dtype)


def _rwkv(z, par, mul, w2p, a2p, *, tb, npair):
    T = z.shape[0]
    width = npair * LANES
    nb = WIDTH // width
    zspec = lambda off: pl.BlockSpec((tb, width), lambda g, t: (t, off + g))
    return pl.pallas_call(
        functools.partial(_rwkv_kernel, tb=tb, npair=npair),
        grid=(nb, T // tb),
        in_specs=[
            zspec(0), zspec(nb), zspec(2 * nb), zspec(3 * nb),
            pl.BlockSpec((tb, LANES), lambda g, t: (t, Z_LORA // LANES)),
            pl.BlockSpec((16, width), lambda g, t: (0, g)),
            pl.BlockSpec((8, LANES), lambda g, t: (0, 0)),
            pl.BlockSpec((LANES, width), lambda g, t: (0, g)),
            pl.BlockSpec((LANES, width), lambda g, t: (0, g)),
        ],
        out_specs=pl.BlockSpec((tb, width), lambda g, t: (t, g)),
        out_shape=jax.ShapeDtypeStruct((T, WIDTH), bf16),
        scratch_shapes=[pltpu.VMEM((npair, LANES, LANES), f32), pltpu.VMEM((3, 8, width), f32),
                        pltpu.VMEM((8, LANES), f32)],
        compiler_params=pltpu.CompilerParams(
            dimension_semantics=("parallel", "arbitrary"), vmem_limit_bytes=VMEM_LIMIT),
        name="rwkv7",
    )(z, z, z, z, z, par, mul, w2p, a2p)


def _out_kernel(perm_ref, x_ref, yr_ref, yf_ref, p_ref, wo_ref, wg_ref, wp_ref, g1_ref, g2_ref, o_ref, wfx_ref):
    @pl.when(pl.program_id(0) == 0)
    def _():
        for s in range(NHEADS):
            src = pl.multiple_of(WIDTH + perm_ref[s] * HEAD, HEAD)
            wfx_ref[s * HEAD:(s + 1) * HEAD, :] = wo_ref[pl.ds(src, HEAD), :]

    def rms(t, g):
        return t * lax.rsqrt(jnp.mean(t * t, axis=-1, keepdims=True) + RMS_EPS) * g

    m = _dot(yr_ref[...], wo_ref[0:WIDTH, :]) + _dot(yf_ref[...], wfx_ref[...])
    x1 = x_ref[...] + rms(m, g1_ref[...])
    gate = _sigmoid(_dot(rms(x1, g2_ref[...]).astype(bf16), wg_ref[...]))
    o_ref[...] = x1 + gate * _dot(p_ref[...].astype(bf16), wp_ref[...])


def _out_proj(perm, x, yr, yf, p, wo, wg, wp, g1, g2, *, tm):
    T = x.shape[0]
    const = lambda shape: pl.BlockSpec(shape, lambda i, perm: (0, 0))
    rows = lambda width: pl.BlockSpec((tm, width), lambda i, perm: (i, 0))
    grid_spec = pltpu.PrefetchScalarGridSpec(
        num_scalar_prefetch=1,
        grid=(T // tm,),
        in_specs=[
            rows(D_MODEL), rows(WIDTH), rows(WIDTH), rows(D_PLE),
            const((D_MODEL, D_MODEL)), const((D_MODEL, D_MODEL)), const((D_PLE, D_MODEL)),
            const((1, D_MODEL)), const((1, D_MODEL)),
        ],
        out_specs=rows(D_MODEL),
        scratch_shapes=[pltpu.VMEM((WIDTH, D_MODEL), bf16)],
    )
    return pl.pallas_call(
        _out_kernel,
        grid_spec=grid_spec,
        out_shape=jax.ShapeDtypeStruct((T, D_MODEL), f32),
        compiler_params=pltpu.CompilerParams(
            dimension_semantics=("arbitrary",), vmem_limit_bytes=VMEM_LIMIT),
        name="out_proj",
    )(perm, x, yr, yf, p, wo, wg, wp, g1, g2)


def _layer(x, p, pre_g, w_in, mu_r, mu_k, mu_v, mu_w, mu_a, w0, w2, a0, a2, k_k, k_a, r_k, ln_w, ln_b,
           b_f, q_g, k_g, w_out, post_g, ple_g, w_gate, w_ple):
    T = x.shape[0]
    rw_end = 4 * WIDTH
    lora_end = rw_end + 2 * LORA
    fx_end = lora_end + 4 * WIDTH
    perm = jnp.argsort(b_f).astype(jnp.int32)
    fx_half = lora_end // HEAD
    src_half = jnp.concatenate([
        jnp.arange(rw_end // HEAD, dtype=jnp.int32),
        (fx_half + NHEADS * jnp.arange(4, dtype=jnp.int32)[:, None] + perm[None, :]).reshape(-1),
        jnp.arange(rw_end // HEAD, lora_end // HEAD, dtype=jnp.int32),
        jnp.zeros((2,), jnp.int32)])
    pick = (jnp.arange(NHEADS, dtype=jnp.int32)[:, None] == jnp.pad(perm, (0, LANES - NHEADS), constant_values=-1)[None, :])
    w_ff = jnp.dot(w_in[:, fx_end:], pick.astype(f32), precision=lax.Precision.HIGHEST)
    w = _relayout_w_in(src_half, w_in, w_ff)
    b_f = jnp.take(b_f, perm)
    z = _in_proj(x, pre_g.reshape(1, D_MODEL), w, tm=min(1024, T), tn=1408)

    gq = jnp.tile(q_g, NHEADS).reshape(1, WIDTH)
    gk = jnp.tile(k_g, NHEADS).reshape(1, WIDTH)
    bfp = jnp.pad(b_f, (0, LANES - NHEADS)).reshape(1, LANES)
    tb = min(256, T)
    qt, ka, vt, fstat = _fox_prep(z, gq, gk, bfp, tb=tb)
    qk_bound = 1.02 * HEAD ** 0.5 * LOG2E * jnp.max(jnp.abs(q_g)) * jnp.max(jnp.abs(k_g))
    y_fx = _attention(qt, ka, vt, z, fstat, qk_bound, tb=tb, tq=min(1024, T), tk=min(512, T))

    par = jnp.stack([mu_r, mu_k, mu_v, w0, a0, k_k, k_a, r_k.reshape(WIDTH), ln_w, ln_b])
    par = jnp.pad(par, ((0, 16 - par.shape[0]), (0, 0)))
    mul = jnp.broadcast_to(jnp.concatenate([mu_w, mu_a]).reshape(1, LANES), (8, LANES))
    zeros = jnp.zeros((LORA, WIDTH), f32)
    w2p = jnp.concatenate([w2, zeros], axis=0).astype(bf16)
    a2p = jnp.concatenate([zeros, a2], axis=0).astype(bf16)
    y_rw = _rwkv(z, par, mul, w2p, a2p, tb=min(256, T), npair=4)

    return _out_proj(perm, x, y_rw, y_fx, p, w_out.astype(bf16), w_gate.astype(bf16), w_ple.astype(bf16),
                     post_g.reshape(1, D_MODEL), ple_g.reshape(1, D_MODEL), tm=min(256, T))


def kernel(x, p, pre_norm_g, w_in, rw_mu_r, rw_mu_k, rw_mu_v, rw_mu_w, rw_mu_a, rw_w0, rw_w2, rw_a0, rw_a2,
           rw_k_k, rw_k_a, rw_r_k, rw_ln_w, rw_ln_b, fx_b_f, fx_q_g, fx_k_g, w_out, post_norm_g, ple_norm_g,
           w_ple_gate, w_ple):
    B = x.shape[0]
    outs = []
    for b in range(B):
        xb = x[b]
        for i in range(p.shape[0]):
            xb = _layer(xb, p[i, b], pre_norm_g[i], w_in[i], rw_mu_r[i], rw_mu_k[i], rw_mu_v[i], rw_mu_w[i],
                        rw_mu_a[i], rw_w0[i], rw_w2[i], rw_a0[i], rw_a2[i], rw_k_k[i], rw_k_a[i], rw_r_k[i],
                        rw_ln_w[i], rw_ln_b[i], fx_b_f[i], fx_q_g[i], fx_k_g[i], w_out[i], post_norm_g[i],
                        ple_norm_g[i], w_ple_gate[i], w_ple[i])
        outs.append(xb)
    return jnp.stack(outs)
```

```python
import functools

import numpy as np
import jax
import jax.numpy as jnp
from jax import lax
from jax.experimental import pallas as pl
from jax.experimental.pallas import tpu as pltpu

f32 = jnp.float32
bf16 = jnp.bfloat16

D_MODEL = 2048
D_PLE = 256
WIDTH = 1024
HEAD = 64
NHEADS = 16
NPAIRS = NHEADS // 2
LORA = 64
LANES = 128
RMS_EPS = 1e-6
GN_EPS = 64e-5
CHUNK = 64
LOG2E = 1.4426950408889634
EXP_NEG_HALF = 0.6065306597126334
NEG_BIG = -1e30

Z_RW = 0
Z_FX = 4096
Z_LORA = 8192
Z_FF = 8320
Z_COLS = 8448
VT_ROWS = 80
Q_COL = 256
SKIP_LOG2 = 48.0

VMEM_LIMIT = 56 * 1024 * 1024


def _dot(a, b):
    return jnp.dot(a, b, preferred_element_type=f32)


def _dot_nt(a, b):
    return lax.dot_general(a, b, (((1,), (1,)), ((), ())), preferred_element_type=f32)


def _dot_tn(a, b):
    return lax.dot_general(a, b, (((0,), (0,)), ((), ())), preferred_element_type=f32)


def _split2(x):
    hi = x.astype(bf16)
    lo = (x - hi.astype(f32)).astype(bf16)
    return hi, lo


def _split3(x):
    hi = x.astype(bf16)
    r = x - hi.astype(f32)
    mid = r.astype(bf16)
    lo = (r - mid.astype(f32)).astype(bf16)
    return hi, mid, lo


def _dot_sel(x, sel):
    hi, mid, lo = _split3(x)
    return _dot(hi, sel) + _dot(mid, sel) + _dot(lo, sel)


def _sel_dot(sel, x):
    hi, mid, lo = _split3(x)
    return _dot(sel, hi) + _dot(sel, mid) + _dot(sel, lo)


def _dot_x3(a, b):
    ah, al = _split2(a)
    bh, bl = _split2(b)
    return _dot(ah, bh) + _dot(al, bh) + _dot(ah, bl)


def _sigmoid(x):
    return 1.0 / (1.0 + jnp.exp(-x))


def _relayout_kernel(src_ref, a_ref, b_ref, ff_ref, o_ref):
    c = pl.program_id(0)
    half_a = src_ref[2 * c]
    half_b = src_ref[2 * c + 1]
    a, b = a_ref[...], b_ref[...]
    a = jnp.where(half_a % 2 == 1, pltpu.roll(a, HEAD, axis=1), a)
    b = jnp.where(half_b % 2 == 0, pltpu.roll(b, HEAD, axis=1), b)
    lane = lax.broadcasted_iota(jnp.int32, a.shape, 1)
    out = jnp.where(lane < HEAD, a, b)
    out = jnp.where(c == pl.num_programs(0) - 1, ff_ref[...], out)
    o_ref[...] = out.astype(bf16)


def _relayout_w_in(src_half, w_in, w_ff):
    nblk = Z_COLS // LANES
    grid_spec = pltpu.PrefetchScalarGridSpec(
        num_scalar_prefetch=1,
        grid=(nblk,),
        in_specs=[
            pl.BlockSpec((D_MODEL, LANES), lambda c, src: (0, src[2 * c] // 2)),
            pl.BlockSpec((D_MODEL, LANES), lambda c, src: (0, src[2 * c + 1] // 2)),
            pl.BlockSpec((D_MODEL, LANES), lambda c, src: (0, 0)),
        ],
        out_specs=pl.BlockSpec((D_MODEL, LANES), lambda c, src: (0, c)),
    )
    return pl.pallas_call(
        _relayout_kernel,
        grid_spec=grid_spec,
        out_shape=jax.ShapeDtypeStruct((D_MODEL, Z_COLS), bf16),
        compiler_params=pltpu.CompilerParams(
            dimension_semantics=("parallel",), vmem_limit_bytes=VMEM_LIMIT),
        name="w_in_relayout",
    )(src_half, w_in, w_in, w_ff)


def _in_proj_kernel(x_ref, g_ref, w_ref, z_ref, h_ref):
    @pl.when(pl.program_id(1) == 0)
    def _():
        x = x_ref[...]
        ms = jnp.mean(x * x, axis=-1, keepdims=True)
        h_ref[...] = (x * lax.rsqrt(ms + RMS_EPS) * g_ref[...]).astype(bf16)

    z_ref[...] = _dot(h_ref[...], w_ref[...])


def _in_proj(x, g, w, *, tm, tn):
    T = x.shape[0]
    return pl.pallas_call(
        _in_proj_kernel,
        grid=(T // tm, Z_COLS // tn),
        in_specs=[
            pl.BlockSpec((tm, D_MODEL), lambda i, j: (i, 0)),
            pl.BlockSpec((1, D_MODEL), lambda i, j: (0, 0)),
            pl.BlockSpec((D_MODEL, tn), lambda i, j: (0, j)),
        ],
        out_specs=pl.BlockSpec((tm, tn), lambda i, j: (i, j)),
        out_shape=jax.ShapeDtypeStruct((T, Z_COLS), f32),
        scratch_shapes=[pltpu.VMEM((tm, D_MODEL), bf16)],
        compiler_params=pltpu.CompilerParams(
            dimension_semantics=("parallel", "arbitrary"), vmem_limit_bytes=VMEM_LIMIT),
        name="in_proj",
    )(x, g, w)


def _fox_prep_kernel(fq_ref, fk_ref, fv_ref, ff_ref, gq_ref, gk_ref, bf_ref,
                     qt_ref, ka_ref, vt_ref, fs_ref, carry_ref, *, tb):
    @pl.when(pl.program_id(0) == 0)
    def _():
        carry_ref[...] = jnp.zeros_like(carry_ref)

    bi = lax.broadcasted_iota(jnp.int32, (LANES, LANES), 0)
    bj = lax.broadcasted_iota(jnp.int32, (LANES, LANES), 1)
    same_head = jnp.where((bi >> 6) == (bj >> 6), 1.0, 0.0).astype(bf16)

    def head_rms(x, gain):
        cols = []
        for c in range(WIDTH // LANES):
            xc = x[:, c * LANES:(c + 1) * LANES]
            ssq = _dot((xc * xc).astype(bf16), same_head)
            cols.append(xc * lax.rsqrt(ssq * (1.0 / HEAD) + RMS_EPS))
        return jnp.concatenate(cols, axis=1) * gain

    qn = head_rms(fq_ref[...], gq_ref[...]) * (HEAD ** -0.5 * LOG2E)
    kn = head_rms(fk_ref[...], gk_ref[...])

    xf = ff_ref[...] + bf_ref[...]
    logf = jnp.minimum(xf, 0.0) - jnp.log(1.0 + jnp.exp(-jnp.abs(xf)))
    ri = lax.broadcasted_iota(jnp.int32, (tb, tb), 0)
    rj = lax.broadcasted_iota(jnp.int32, (tb, tb), 1)
    tri = jnp.where(rj <= ri, 1.0, 0.0).astype(bf16)
    cum = _sel_dot(tri, logf) + carry_ref[0:1, :]
    carry_ref[...] = jnp.broadcast_to(cum[tb - 1:tb, :], carry_ref.shape)
    cum2 = cum * LOG2E
    c_hi, c_mid, c_lo = (t.astype(f32) for t in _split3(cum2))
    srow = lax.broadcasted_iota(jnp.int32, (8, LANES), 0)
    fs_ref[0] = jnp.where(srow == 0, jnp.max(cum2, axis=0, keepdims=True), jnp.min(cum2, axis=0, keepdims=True))

    lane = lax.broadcasted_iota(jnp.int32, (tb, LANES), 1)
    low = lane < HEAD
    vt = fv_ref[...].T
    vpad = jnp.concatenate(
        [jnp.ones((1, tb), f32), jnp.zeros((VT_ROWS - HEAD - 1, tb), f32)], axis=0)
    for h in range(NHEADS):
        col = slice((h // 2) * LANES, (h // 2 + 1) * LANES)
        qc, kc = qn[:, col], kn[:, col]
        if h % 2 == 1:
            qc, kc = pltpu.roll(qc, HEAD, axis=1), pltpu.roll(kc, HEAD, axis=1)
        fh, fm, fl = c_hi[:, h:h + 1], c_mid[:, h:h + 1], c_lo[:, h:h + 1]
        eq = jnp.where(lane == HEAD, fh, jnp.where(lane == HEAD + 1, fm, jnp.where(lane == HEAD + 2, fl,
             jnp.where(lane < HEAD + 6, 1.0, 0.0))))
        ek = jnp.where(lane == HEAD + 3, -fh, jnp.where(lane == HEAD + 4, -fm, jnp.where(lane == HEAD + 5, -fl,
             jnp.where(lane < HEAD + 3, 1.0, 0.0))))
        qt_ref[h] = jnp.where(low, qc, eq).T.astype(bf16)
        ka_ref[h] = jnp.where(low, kc, ek).astype(bf16)
        vt_ref[h] = jnp.concatenate([vt[h * HEAD:(h + 1) * HEAD, :], vpad], axis=0).astype(bf16)


def _fox_prep(z, gq, gk, bfp, *, tb):
    T = z.shape[0]
    wb = Z_FX // WIDTH
    return pl.pallas_call(
        functools.partial(_fox_prep_kernel, tb=tb),
        grid=(T // tb,),
        in_specs=[
            pl.BlockSpec((tb, WIDTH), lambda i: (i, wb)),
            pl.BlockSpec((tb, WIDTH), lambda i: (i, wb + 1)),
            pl.BlockSpec((tb, WIDTH), lambda i: (i, wb + 2)),
            pl.BlockSpec((tb, LANES), lambda i: (i, Z_FF // LANES)),
            pl.BlockSpec((1, WIDTH), lambda i: (0, 0)),
            pl.BlockSpec((1, WIDTH), lambda i: (0, 0)),
            pl.BlockSpec((1, LANES), lambda i: (0, 0)),
        ],
        out_specs=[
            pl.BlockSpec((NHEADS, LANES, tb), lambda i: (0, 0, i)),
            pl.BlockSpec((NHEADS, tb, LANES), lambda i: (0, i, 0)),
            pl.BlockSpec((NHEADS, VT_ROWS, tb), lambda i: (0, 0, i)),
            pl.BlockSpec((1, 8, LANES), lambda i: (i, 0, 0)),
        ],
        out_shape=[
            jax.ShapeDtypeStruct((NHEADS, LANES, T), bf16),
            jax.ShapeDtypeStruct((NHEADS, T, LANES), bf16),
            jax.ShapeDtypeStruct((NHEADS, VT_ROWS, T), bf16),
            jax.ShapeDtypeStruct((T // tb, 8, LANES), f32),
        ],
        scratch_shapes=[pltpu.VMEM((8, LANES), f32)],
        compiler_params=pltpu.CompilerParams(
            dimension_semantics=("arbitrary",), vmem_limit_bytes=VMEM_LIMIT),
        name="fox_prep",
    )(z, z, z, z, gq, gk, bfp)


def _attn_kernel(cnt_ref, blk_ref, qt_ref, ka_hbm, vt_hbm, fg_ref, o_ref, m_ref, acc_ref, kbuf, vbuf, sem,
                 *, tq, tk):
    g, i = pl.program_id(0), pl.program_id(1)
    nq = pl.num_programs(1)
    ndiag = tq // tk
    max_full = ndiag * nq
    step = g * nq + i
    n_full = cnt_ref[step]

    def block_id(step_, t, n_full_, i_):
        return jnp.where(t < n_full_, blk_ref[step_ * max_full + jnp.minimum(t, max_full - 1)], ndiag * i_ + t - n_full_)

    def copies(g_, j, slot):
        return (pltpu.make_async_copy(ka_hbm.at[pl.ds(2 * g_, 2), pl.ds(j * tk, tk), :], kbuf.at[slot], sem.at[0, slot]),
                pltpu.make_async_copy(vt_hbm.at[pl.ds(2 * g_, 2), :, pl.ds(j * tk, tk)], vbuf.at[slot], sem.at[1, slot]))

    def fetch(g_, j, slot):
        for c in copies(g_, j, slot):
            c.start()

    def wait(slot):
        for c in copies(g, 0, slot):
            c.wait()

    @pl.when(step == 0)
    def _():
        fetch(g, block_id(step, 0, n_full, i), 0)

    m_ref[...] = jnp.full(m_ref.shape, NEG_BIG, f32)
    acc_ref[...] = jnp.zeros_like(acc_ref)

    def update(d, slot):
        ka_ref, vt_ref = kbuf.at[slot], vbuf.at[slot]

        def n_keys(q_lo):
            return tk if d < 0 else max(0, min(tk, q_lo + Q_COL - d * tk))

        chains = [(hh, qs * Q_COL) for qs in range(tq // Q_COL) for hh in range(2) if n_keys(qs * Q_COL) > 0]

        def scores(hh, q_lo):
            nk = n_keys(q_lo)
            st = _dot(ka_ref[hh, 0:nk, :], qt_ref[hh, :, q_lo:q_lo + Q_COL])
            if d >= 0 and d * tk + nk - 1 > q_lo:
                kpos = d * tk + lax.broadcasted_iota(jnp.int32, st.shape, 0)
                qpos = q_lo + lax.broadcasted_iota(jnp.int32, st.shape, 1)
                st = jnp.where(kpos <= qpos, st, NEG_BIG)
            return st

        def consume(hh, q_lo, st):
            qsl = slice(q_lo, q_lo + Q_COL)
            m_old = m_ref[hh, :, qsl]
            m_new = jnp.maximum(m_old, jnp.max(st, axis=0, keepdims=True))
            m_ref[hh, :, qsl] = m_new
            p = jnp.exp2(st - m_new).astype(bf16)
            acc_ref[hh, :, qsl] = (jnp.exp2(m_old - m_new) * acc_ref[hh, :, qsl]
                                   + _dot(vt_ref[hh, :, 0:n_keys(q_lo)], p))

        ahead = 4
        pending = [scores(*c) for c in chains[:ahead]]
        for n, chain in enumerate(chains):
            st = pending.pop(0)
            if n + ahead < len(chains):
                pending.append(scores(*chains[n + ahead]))
            consume(*chain, st)

    def run_block(t, d):
        slot = t % 2
        wait(slot)
        fetch(g, block_id(step, t + 1, n_full, i), 1 - slot)
        update(d, slot)

    def full_block(t, carry):
        run_block(t, -1)
        return carry

    lax.fori_loop(0, n_full, full_block, 0)
    for d in range(ndiag - 1):
        run_block(n_full + d, d)

    last_slot = (n_full + ndiag - 1) % 2
    wait(last_slot)
    update(ndiag - 1, last_slot)

    @pl.when(step + 1 < pl.num_programs(0) * nq)
    def _():
        nstep = step + 1
        fetch(nstep // nq, block_id(nstep, 0, cnt_ref[nstep], nstep % nq), 0)

    outs = []
    for hh in range(2):
        acc = acc_ref[hh]
        outs.append(acc[0:HEAD, :] / acc[HEAD:HEAD + 1, :])
    y = jnp.concatenate(outs, axis=0).T
    gate = fg_ref[...]
    o_ref[...] = (y * gate * _sigmoid(gate)).astype(o_ref.dtype)


def _attention_work_list(fstat, qk_bound, *, T, tb, tq, tk):
    nq, nk = T // tq, T // tk
    fmax_q = fstat[:, 0, :NHEADS].reshape(nq, tq // tb, NHEADS).max(axis=1)
    fmin_k = fstat[:, 1, :NHEADS].reshape(nk, tk // tb, NHEADS).min(axis=1)
    need = (fmax_q[:, None, :] - fmin_k[None, :, :] + 2.0 * qk_bound) > -SKIP_LOG2
    need = jnp.transpose(need[:, :, 0::2] | need[:, :, 1::2], (2, 0, 1))
    before = np.arange(nk)[None, :] < (tq // tk) * np.arange(nq)[:, None]
    need = need & jnp.asarray(before)[None]
    count = jnp.sum(need, axis=2).astype(jnp.int32)
    ids = jnp.argsort(jnp.logical_not(need), axis=2, stable=True).astype(jnp.int32)
    return count.reshape(-1), ids.reshape(-1)


def _attention(qt, ka, vt, z, fstat, qk_bound, *, tb, tq, tk):
    T = ka.shape[1]
    nq = T // tq
    count, ids = _attention_work_list(fstat, qk_bound, T=T, tb=tb, tq=tq, tk=tk)
    fg_col = (Z_FX + 3 * WIDTH) // LANES
    grid_spec = pltpu.PrefetchScalarGridSpec(
        num_scalar_prefetch=2,
        grid=(NPAIRS, nq),
        in_specs=[
            pl.BlockSpec((2, LANES, tq), lambda g, i, cnt, blk: (g, 0, i)),
            pl.BlockSpec(memory_space=pl.ANY),
            pl.BlockSpec(memory_space=pl.ANY),
            pl.BlockSpec((tq, LANES), lambda g, i, cnt, blk: (i, fg_col + g)),
        ],
        out_specs=pl.BlockSpec((tq, LANES), lambda g, i, cnt, blk: (i, g)),
        scratch_shapes=[
            pltpu.VMEM((2, 1, tq), f32), pltpu.VMEM((2, VT_ROWS, tq), f32),
            pltpu.VMEM((2, 2, tk, LANES), bf16), pltpu.VMEM((2, 2, VT_ROWS, tk), bf16),
            pltpu.SemaphoreType.DMA((2, 2)),
        ],
    )
    return pl.pallas_call(
        functools.partial(_attn_kernel, tq=tq, tk=tk),
        grid_spec=grid_spec,
        out_shape=jax.ShapeDtypeStruct((T, WIDTH), bf16),
        compiler_params=pltpu.CompilerParams(
            dimension_semantics=("arbitrary", "arbitrary"), vmem_limit_bytes=VMEM_LIMIT),
        name="fox_attention",
    )(count, ids, qt, ka, vt, z)


P_MU_R, P_MU_K, P_MU_V, P_W0, P_A0, P_KK, P_KA, P_RK, P_LNW, P_LNB = range(10)


def _rwkv_kernel(zr_ref, zk_ref, zv_ref, zg_ref, zl_ref, par_ref, mul_ref, w2_ref, a2_ref,
                 o_ref, s_ref, prev_ref, prevl_ref, *, tb, npair):
    width = npair * LANES

    @pl.when(pl.program_id(1) == 0)
    def _():
        s_ref[...] = jnp.zeros_like(s_ref)
        prev_ref[...] = jnp.zeros_like(prev_ref)
        prevl_ref[...] = jnp.zeros_like(prevl_ref)

    par = par_ref[...]
    prow = lambda k: par[k:k + 1, :]

    def shift(x, mu, last_ref):
        row = lax.broadcasted_iota(jnp.int32, x.shape, 0)
        xp = pltpu.roll(x, 1, axis=0)
        xp = jnp.where(row == 0, last_ref[7:8, :], xp)
        last_ref[...] = x[tb - 8:tb, :]
        return x + (xp - x) * mu

    r = shift(zr_ref[...], prow(P_MU_R), prev_ref.at[0])
    k = shift(zk_ref[...], prow(P_MU_K), prev_ref.at[1])
    v = shift(zv_ref[...], prow(P_MU_V), prev_ref.at[2])
    sl = shift(zl_ref[...], mul_ref[0:1, :], prevl_ref)

    bi = lax.broadcasted_iota(jnp.int32, (LANES, LANES), 0)
    bj = lax.broadcasted_iota(jnp.int32, (LANES, LANES), 1)
    same_head = jnp.where((bi >> 6) == (bj >> 6), 1.0, 0.0).astype(bf16)

    def seg(x):
        return jnp.concatenate(
            [_dot(x[:, pr * LANES:(pr + 1) * LANES].astype(bf16), same_head) for pr in range(npair)], axis=1)

    u = prow(P_W0) + _dot(jnp.tanh(sl).astype(bf16), w2_ref[...])
    ld = -EXP_NEG_HALF * _sigmoid(u)
    av = _sigmoid(prow(P_A0) + _dot(sl.astype(bf16), a2_ref[...]))

    kk = k * prow(P_KK)
    kk = kk * lax.rsqrt(jnp.maximum(seg(kk * kk), 1e-24))
    k2 = k * (1.0 + (av - 1.0) * prow(P_KA))

    ti = lax.broadcasted_iota(jnp.int32, (tb, tb), 0)
    tj = lax.broadcasted_iota(jnp.int32, (tb, tb), 1)
    tri = jnp.where(((ti >> 6) == (tj >> 6)) & (tj <= ti), 1.0, 0.0).astype(bf16)
    ld_hi, ld_lo = _split2(ld)
    cum = _dot(tri, ld_hi) + _dot(tri, ld_lo)

    e_pos = jnp.exp(cum)
    e_neg = jnp.exp(-cum)
    a_t = -kk * jnp.exp(cum - ld)
    b_t = kk * av * e_neg
    k_t = k2 * e_neg
    r_t = r * e_pos

    ri = lax.broadcasted_iota(jnp.int32, (LANES, LANES), 0)
    rj = lax.broadcasted_iota(jnp.int32, (LANES, LANES), 1)
    strict_lower = rj < ri
    eye = jnp.where(ri == rj, 1.0, 0.0)
    qi = lax.broadcasted_iota(jnp.int32, (CHUNK, LANES), 0)
    qj = lax.broadcasted_iota(jnp.int32, (CHUNK, LANES), 1)
    incl_lower = (qj & (CHUNK - 1)) <= qi
    chunk_head0 = qj < HEAD

    def stack(x):
        return jnp.concatenate([jnp.where(chunk_head0, x, 0.0), jnp.where(chunk_head0, 0.0, x)], axis=0)

    def level_mask(b):
        sh = b.bit_length()
        return ((ri >> sh) == (rj >> sh)) & ((ri & (2 * b - 1)) >= b) & ((rj & (2 * b - 1)) < b)

    nchunk = tb // CHUNK
    units = [(pr, c) for pr in range(npair) for c in range(nchunk)]
    rows = lambda c: slice(c * CHUNK, (c + 1) * CHUNK)
    lanes = lambda pr: slice(pr * LANES, (pr + 1) * LANES)
    cut = lambda x, u: x[rows(u[1]), lanes(u[0])]
    cend = {u: cum[(u[1] + 1) * CHUNK - 1:(u[1] + 1) * CHUNK, lanes(u[0])] for u in units}

    am = {u: stack(cut(a_t, u)) for u in units}
    vm = {u: stack(cut(v, u)) for u in units}
    x = {u: _dot_nt(jnp.concatenate([am[u], cut(r_t, u)], axis=0).astype(bf16),
                    jnp.concatenate([stack(cut(b_t, u)), stack(cut(k_t, u))], axis=0).astype(bf16))
         for u in units}
    l_ab = {u: jnp.where(strict_lower, x[u][0:LANES, 0:LANES], 0.0) for u in units}
    l_ak = {u: jnp.where(strict_lower, x[u][0:LANES, LANES:], 0.0) for u in units}
    l_r = {u: jnp.where(jnp.concatenate([incl_lower, incl_lower], axis=1), x[u][LANES:, :], 0.0).astype(bf16)
           for u in units}

    inv = {u: eye + jnp.where(level_mask(1), l_ab[u], 0.0) for u in units}
    b = 2
    while b < CHUNK:
        mask = level_mask(b)
        y_ = {u: _dot(jnp.where(mask, l_ab[u], 0.0).astype(bf16), inv[u].astype(bf16)) for u in units}
        inv = {u: inv[u] + _dot(inv[u].astype(bf16), y_[u].astype(bf16)) for u in units}
        b *= 2

    lv = {u: _dot(l_ak[u].astype(bf16), vm[u].astype(bf16)) for u in units}
    tg = {u: _dot(inv[u].astype(bf16), jnp.concatenate([am[u], lv[u]], axis=1).astype(bf16)) for u in units}
    zeros = jnp.zeros((LANES, LANES), f32)
    ly = {u: _dot(l_r[u], jnp.concatenate(
              [tg[u], jnp.concatenate([zeros, vm[u]], axis=1)], axis=0).astype(bf16)) for u in units}
    bh = {u: stack(cut(kk, u) * cut(av, u) * jnp.exp(cend[u] - cut(cum, u))).astype(bf16) for u in units}
    kh = {u: stack(cut(k2, u) * jnp.exp(cend[u] - cut(cum, u))).astype(bf16) for u in units}
    ag = {u: _dot_tn(tg[u].astype(bf16), bh[u]) for u in units}
    vk = {u: _dot_tn(vm[u].astype(bf16), kh[u]) for u in units}
    q_eff = {u: (cut(r_t, u) + ly[u][:, 0:LANES]).astype(bf16) for u in units}
    a_eff = {u: ag[u][0:LANES, :].astype(bf16) for u in units}
    g_eff = {u: ag[u][LANES:, :] + vk[u] for u in units}

    ys = [[None] * npair for _ in range(nchunk)]
    state = [s_ref[pr] for pr in range(npair)]
    for c in range(nchunk):
        for pr in range(npair):
            u = (pr, c)
            s_bf = state[pr].astype(bf16)
            ys[c][pr] = _dot_nt(q_eff[u], s_bf) + ly[u][:, LANES:]
            state[pr] = state[pr] * jnp.exp(cend[u]) + _dot(s_bf, a_eff[u]) + g_eff[u]
    for pr in range(npair):
        s_ref[pr] = state[pr]

    y = jnp.concatenate([jnp.concatenate(yc, axis=1) for yc in ys], axis=0)
    mean = seg(y) * (1.0 / HEAD)
    d = y - mean
    var = seg(d * d) * (1.0 / HEAD)
    yn = d * lax.rsqrt(var + GN_EPS) * prow(P_LNW) + prow(P_LNB)
    bonus = seg(r * k2 * prow(P_RK)) * v
    g = zg_ref[...]
    o_ref[...] = ((yn + bonus) * g * _sigmoid(g)).astype(o_ref.dtype)


def _rwkv(z, par, mul, w2p, a2p, *, tb, npair):
    T = z.shape[0]
    width = npair * LANES
    nb = WIDTH // width
    zspec = lambda off: pl.BlockSpec((tb, width), lambda g, t: (t, off + g))
    return pl.pallas_call(
        functools.partial(_rwkv_kernel, tb=tb, npair=npair),
        grid=(nb, T // tb),
        in_specs=[
            zspec(0), zspec(nb), zspec(2 * nb), zspec(3 * nb),
            pl.BlockSpec((tb, LANES), lambda g, t: (t, Z_LORA // LANES)),
            pl.BlockSpec((16, width), lambda g, t: (0, g)),
            pl.BlockSpec((8, LANES), lambda g, t: (0, 0)),
            pl.BlockSpec((LANES, width), lambda g, t: (0, g)),
            pl.BlockSpec((LANES, width), lambda g, t: (0, g)),
        ],
        out_specs=pl.BlockSpec((tb, width), lambda g, t: (t, g)),
        out_shape=jax.ShapeDtypeStruct((T, WIDTH), bf16),
        scratch_shapes=[pltpu.VMEM((npair, LANES, LANES), f32), pltpu.VMEM((3, 8, width), f32),
                        pltpu.VMEM((8, LANES), f32)],
        compiler_params=pltpu.CompilerParams(
            dimension_semantics=("parallel", "arbitrary"), vmem_limit_bytes=VMEM_LIMIT),
        name="rwkv7",
    )(z, z, z, z, z, par, mul, w2p, a2p)


def _out_kernel(perm_ref, x_ref, yr_ref, yf_ref, p_ref, wo_ref, wg_ref, wp_ref, g1_ref, g2_ref, o_ref, wfx_ref):
    @pl.when(pl.program_id(0) == 0)
    def _():
        for s in range(NHEADS):
            src = pl.multiple_of(WIDTH + perm_ref[s] * HEAD, HEAD)
            wfx_ref[s * HEAD:(s + 1) * HEAD, :] = wo_ref[pl.ds(src, HEAD), :]

    def rms(t, g):
        return t * lax.rsqrt(jnp.mean(t * t, axis=-1, keepdims=True) + RMS_EPS) * g

    m = _dot(yr_ref[...], wo_ref[0:WIDTH, :]) + _dot(yf_ref[...], wfx_ref[...])
    x1 = x_ref[...] + rms(m, g1_ref[...])
    gate = _sigmoid(_dot(rms(x1, g2_ref[...]).astype(bf16), wg_ref[...]))
    o_ref[...] = x1 + gate * _dot(p_ref[...].astype(bf16), wp_ref[...])


def _out_proj(perm, x, yr, yf, p, wo, wg, wp, g1, g2, *, tm):
    T = x.shape[0]
    const = lambda shape: pl.BlockSpec(shape, lambda i, perm: (0, 0))
    rows = lambda width: pl.BlockSpec((tm, width), lambda i, perm: (i, 0))
    grid_spec = pltpu.PrefetchScalarGridSpec(
        num_scalar_prefetch=1,
        grid=(T // tm,),
        in_specs=[
            rows(D_MODEL), rows(WIDTH), rows(WIDTH), rows(D_PLE),
            const((D_MODEL, D_MODEL)), const((D_MODEL, D_MODEL)), const((D_PLE, D_MODEL)),
            const((1, D_MODEL)), const((1, D_MODEL)),
        ],
        out_specs=rows(D_MODEL),
        scratch_shapes=[pltpu.VMEM((WIDTH, D_MODEL), bf16)],
    )
    return pl.pallas_call(
        _out_kernel,
        grid_spec=grid_spec,
        out_shape=jax.ShapeDtypeStruct((T, D_MODEL), f32),
        compiler_params=pltpu.CompilerParams(
            dimension_semantics=("arbitrary",), vmem_limit_bytes=VMEM_LIMIT),
        name="out_proj",
    )(perm, x, yr, yf, p, wo, wg, wp, g1, g2)


def _layer(x, p, pre_g, w_in, mu_r, mu_k, mu_v, mu_w, mu_a, w0, w2, a0, a2, k_k, k_a, r_k, ln_w, ln_b,
           b_f, q_g, k_g, w_out, post_g, ple_g, w_gate, w_ple):
    T = x.shape[0]
    rw_end = 4 * WIDTH
    lora_end = rw_end + 2 * LORA
    fx_end = lora_end + 4 * WIDTH
    perm = jnp.argsort(b_f).astype(jnp.int32)
    fx_half = lora_end // HEAD
    src_half = jnp.concatenate([
        jnp.arange(rw_end // HEAD, dtype=jnp.int32),
        (fx_half + NHEADS * jnp.arange(4, dtype=jnp.int32)[:, None] + perm[None, :]).reshape(-1),
        jnp.arange(rw_end // HEAD, lora_end // HEAD, dtype=jnp.int32),
        jnp.zeros((2,), jnp.int32)])
    pick = (jnp.arange(NHEADS, dtype=jnp.int32)[:, None] == jnp.pad(perm, (0, LANES - NHEADS), constant_values=-1)[None, :])
    w_ff = jnp.dot(w_in[:, fx_end:], pick.astype(f32), precision=lax.Precision.HIGHEST)
    w = _relayout_w_in(src_half, w_in, w_ff)
    b_f = jnp.take(b_f, perm)
    z = _in_proj(x, pre_g.reshape(1, D_MODEL), w, tm=min(1024, T), tn=1408)

    gq = jnp.tile(q_g, NHEADS).reshape(1, WIDTH)
    gk = jnp.tile(k_g, NHEADS).reshape(1, WIDTH)
    bfp = jnp.pad(b_f, (0, LANES - NHEADS)).reshape(1, LANES)
    tb = min(256, T)
    qt, ka, vt, fstat = _fox_prep(z, gq, gk, bfp, tb=tb)
    qk_bound = 1.02 * HEAD ** 0.5 * LOG2E * jnp.max(jnp.abs(q_g)) * jnp.max(jnp.abs(k_g))
    y_fx = _attention(qt, ka, vt, z, fstat, qk_bound, tb=tb, tq=min(1024, T), tk=min(512, T))

    par = jnp.stack([mu_r, mu_k, mu_v, w0, a0, k_k, k_a, r_k.reshape(WIDTH), ln_w, ln_b])
    par = jnp.pad(par, ((0, 16 - par.shape[0]), (0, 0)))
    mul = jnp.broadcast_to(jnp.concatenate([mu_w, mu_a]).reshape(1, LANES), (8, LANES))
    zeros = jnp.zeros((LORA, WIDTH), f32)
    w2p = jnp.concatenate([w2, zeros], axis=0).astype(bf16)
    a2p = jnp.concatenate([zeros, a2], axis=0).astype(bf16)
    y_rw = _rwkv(z, par, mul, w2p, a2p, tb=min(256, T), npair=4)

    return _out_proj(perm, x, y_rw, y_fx, p, w_out.astype(bf16), w_gate.astype(bf16), w_ple.astype(bf16),
                     post_g.reshape(1, D_MODEL), ple_g.reshape(1, D_MODEL), tm=min(256, T))


def kernel(x, p, pre_norm_g, w_in, rw_mu_r, rw_mu_k, rw_mu_v, rw_mu_w, rw_mu_a, rw_w0, rw_w2, rw_a0, rw_a2,
           rw_k_k, rw_k_a, rw_r_k, rw_ln_w, rw_ln_b, fx_b_f, fx_q_g, fx_k_g, w_out, post_norm_g, ple_norm_g,
           w_ple_gate, w_ple):
    B = x.shape[0]
    outs = []
    for b in range(B):
        xb = x[b]
        for i in range(p.shape[0]):
            xb = _layer(xb, p[i, b], pre_norm_g[i], w_in[i], rw_mu_r[i], rw_mu_k[i], rw_mu_v[i], rw_mu_w[i],
                        rw_mu_a[i], rw_w0[i], rw_w2[i], rw_a0[i], rw_a2[i], rw_k_k[i], rw_k_a[i], rw_r_k[i],
                        rw_ln_w[i], rw_ln_b[i], fx_b_f[i], fx_q_g[i], fx_k_g[i], w_out[i], post_norm_g[i],
                        ple_norm_g[i], w_ple_gate[i], w_ple[i])
        outs.append(xb)
    return jnp.stack(outs)
```

```python
import functools

import numpy as np
import jax
import jax.numpy as jnp
from jax import lax
from jax.experimental import pallas as pl
from jax.experimental.pallas import tpu as pltpu

f32 = jnp.float32
bf16 = jnp.bfloat16

D_MODEL = 2048
D_PLE = 256
WIDTH = 1024
HEAD = 64
NHEADS = 16
NPAIRS = NHEADS // 2
LORA = 64
LANES = 128
RMS_EPS = 1e-6
GN_EPS = 64e-5
CHUNK = 64
LOG2E = 1.4426950408889634
EXP_NEG_HALF = 0.6065306597126334
NEG_BIG = -1e30

Z_RW = 0
Z_FX = 4096
Z_LORA = 8192
Z_FF = 8320
Z_COLS = 8448
VT_ROWS = 80
Q_COL = 256
SKIP_LOG2 = 48.0

VMEM_LIMIT = 56 * 1024 * 1024


def _dot(a, b):
    return jnp.dot(a, b, preferred_element_type=f32)


def _dot_nt(a, b):
    return lax.dot_general(a, b, (((1,), (1,)), ((), ())), preferred_element_type=f32)


def _dot_tn(a, b):
    return lax.dot_general(a, b, (((0,), (0,)), ((), ())), preferred_element_type=f32)


def _split2(x):
    hi = x.astype(bf16)
    lo = (x - hi.astype(f32)).astype(bf16)
    return hi, lo


def _split3(x):
    hi = x.astype(bf16)
    r = x - hi.astype(f32)
    mid = r.astype(bf16)
    lo = (r - mid.astype(f32)).astype(bf16)
    return hi, mid, lo


def _dot_sel(x, sel):
    hi, mid, lo = _split3(x)
    return _dot(hi, sel) + _dot(mid, sel) + _dot(lo, sel)


def _sel_dot(sel, x):
    hi, mid, lo = _split3(x)
    return _dot(sel, hi) + _dot(sel, mid) + _dot(sel, lo)


def _dot_x3(a, b):
    ah, al = _split2(a)
    bh, bl = _split2(b)
    return _dot(ah, bh) + _dot(al, bh) + _dot(ah, bl)


def _sigmoid(x):
    return 1.0 / (1.0 + jnp.exp(-x))


def _relayout_kernel(src_ref, a_ref, b_ref, ff_ref, o_ref):
    c = pl.program_id(0)
    rows = jnp.concatenate([a_ref[...], b_ref[...]], axis=0)
    rows = jnp.where(c == pl.num_programs(0) - 1, ff_ref[...], rows)
    o_ref[...] = rows.T.astype(bf16)


def _relayout_w_in(src_half, w_in_t, w_ff_t):
    nblk = Z_COLS // LANES
    grid_spec = pltpu.PrefetchScalarGridSpec(
        num_scalar_prefetch=1,
        grid=(nblk,),
        in_specs=[
            pl.BlockSpec((HEAD, D_MODEL), lambda c, src: (src[2 * c], 0)),
            pl.BlockSpec((HEAD, D_MODEL), lambda c, src: (src[2 * c + 1], 0)),
            pl.BlockSpec((LANES, D_MODEL), lambda c, src: (0, 0)),
        ],
        out_specs=pl.BlockSpec((D_MODEL, LANES), lambda c, src: (0, c)),
    )
    return pl.pallas_call(
        _relayout_kernel,
        grid_spec=grid_spec,
        out_shape=jax.ShapeDtypeStruct((D_MODEL, Z_COLS), bf16),
        compiler_params=pltpu.CompilerParams(
            dimension_semantics=("parallel",), vmem_limit_bytes=VMEM_LIMIT),
        name="w_in_relayout",
    )(src_half, w_in_t, w_in_t, w_ff_t)


def _in_proj_kernel(x_ref, g_ref, w_ref, z_ref, h_ref):
    @pl.when(pl.program_id(1) == 0)
    def _():
        x = x_ref[...]
        ms = jnp.mean(x * x, axis=-1, keepdims=True)
        h_ref[...] = (x * lax.rsqrt(ms + RMS_EPS) * g_ref[...]).astype(bf16)

    z_ref[...] = _dot(h_ref[...], w_ref[...])


def _in_proj(x, g, w, *, tm, tn):
    T = x.shape[0]
    return pl.pallas_call(
        _in_proj_kernel,
        grid=(T // tm, Z_COLS // tn),
        in_specs=[
            pl.BlockSpec((tm, D_MODEL), lambda i, j: (i, 0)),
            pl.BlockSpec((1, D_MODEL), lambda i, j: (0, 0)),
            pl.BlockSpec((D_MODEL, tn), lambda i, j: (0, j)),
        ],
        out_specs=pl.BlockSpec((tm, tn), lambda i, j: (i, j)),
        out_shape=jax.ShapeDtypeStruct((T, Z_COLS), f32),
        scratch_shapes=[pltpu.VMEM((tm, D_MODEL), bf16)],
        compiler_params=pltpu.CompilerParams(
            dimension_semantics=("parallel", "arbitrary"), vmem_limit_bytes=VMEM_LIMIT),
        name="in_proj",
    )(x, g, w)


def _fox_prep_kernel(fq_ref, fk_ref, fv_ref, ff_ref, gq_ref, gk_ref, bf_ref,
                     qt_ref, ka_ref, vt_ref, fs_ref, carry_ref, *, tb):
    @pl.when(pl.program_id(0) == 0)
    def _():
        carry_ref[...] = jnp.zeros_like(carry_ref)

    bi = lax.broadcasted_iota(jnp.int32, (LANES, LANES), 0)
    bj = lax.broadcasted_iota(jnp.int32, (LANES, LANES), 1)
    same_head = jnp.where((bi >> 6) == (bj >> 6), 1.0, 0.0).astype(bf16)

    def head_rms(x, gain):
        cols = []
        for c in range(WIDTH // LANES):
            xc = x[:, c * LANES:(c + 1) * LANES]
            ssq = _dot((xc * xc).astype(bf16), same_head)
            cols.append(xc * lax.rsqrt(ssq * (1.0 / HEAD) + RMS_EPS))
        return jnp.concatenate(cols, axis=1) * gain

    qn = head_rms(fq_ref[...], gq_ref[...]) * (HEAD ** -0.5 * LOG2E)
    kn = head_rms(fk_ref[...], gk_ref[...])

    xf = ff_ref[...] + bf_ref[...]
    logf = jnp.minimum(xf, 0.0) - jnp.log(1.0 + jnp.exp(-jnp.abs(xf)))
    ri = lax.broadcasted_iota(jnp.int32, (tb, tb), 0)
    rj = lax.broadcasted_iota(jnp.int32, (tb, tb), 1)
    tri = jnp.where(rj <= ri, 1.0, 0.0).astype(bf16)
    cum = _sel_dot(tri, logf) + carry_ref[0:1, :]
    carry_ref[...] = jnp.broadcast_to(cum[tb - 1:tb, :], carry_ref.shape)
    cum2 = cum * LOG2E
    c_hi, c_mid, c_lo = (t.astype(f32) for t in _split3(cum2))
    srow = lax.broadcasted_iota(jnp.int32, (8, LANES), 0)
    fs_ref[0] = jnp.where(srow == 0, jnp.max(cum2, axis=0, keepdims=True), jnp.min(cum2, axis=0, keepdims=True))

    lane = lax.broadcasted_iota(jnp.int32, (tb, LANES), 1)
    low = lane < HEAD
    vt = fv_ref[...].T
    vpad = jnp.concatenate(
        [jnp.ones((1, tb), f32), jnp.zeros((VT_ROWS - HEAD - 1, tb), f32)], axis=0)
    for h in range(NHEADS):
        col = slice((h // 2) * LANES, (h // 2 + 1) * LANES)
        qc, kc = qn[:, col], kn[:, col]
        if h % 2 == 1:
            qc, kc = pltpu.roll(qc, HEAD, axis=1), pltpu.roll(kc, HEAD, axis=1)
        fh, fm, fl = c_hi[:, h:h + 1], c_mid[:, h:h + 1], c_lo[:, h:h + 1]
        eq = jnp.where(lane == HEAD, fh, jnp.where(lane == HEAD + 1, fm, jnp.where(lane == HEAD + 2, fl,
             jnp.where(lane < HEAD + 6, 1.0, 0.0))))
        ek = jnp.where(lane == HEAD + 3, -fh, jnp.where(lane == HEAD + 4, -fm, jnp.where(lane == HEAD + 5, -fl,
             jnp.where(lane < HEAD + 3, 1.0, 0.0))))
        qt_ref[h] = jnp.where(low, qc, eq).T.astype(bf16)
        ka_ref[h] = jnp.where(low, kc, ek).astype(bf16)
        vt_ref[h] = jnp.concatenate([vt[h * HEAD:(h + 1) * HEAD, :], vpad], axis=0).astype(bf16)


def _fox_prep(z, gq, gk, bfp, *, tb):
    T = z.shape[0]
    wb = Z_FX // WIDTH
    return pl.pallas_call(
        functools.partial(_fox_prep_kernel, tb=tb),
        grid=(T // tb,),
        in_specs=[
            pl.BlockSpec((tb, WIDTH), lambda i: (i, wb)),
            pl.BlockSpec((tb, WIDTH), lambda i: (i, wb + 1)),
            pl.BlockSpec((tb, WIDTH), lambda i: (i, wb + 2)),
            pl.BlockSpec((tb, LANES), lambda i: (i, Z_FF // LANES)),
            pl.BlockSpec((1, WIDTH), lambda i: (0, 0)),
            pl.BlockSpec((1, WIDTH), lambda i: (0, 0)),
            pl.BlockSpec((1, LANES), lambda i: (0, 0)),
        ],
        out_specs=[
            pl.BlockSpec((NHEADS, LANES, tb), lambda i: (0, 0, i)),
            pl.BlockSpec((NHEADS, tb, LANES), lambda i: (0, i, 0)),
            pl.BlockSpec((NHEADS, VT_ROWS, tb), lambda i: (0, 0, i)),
            pl.BlockSpec((1, 8, LANES), lambda i: (i, 0, 0)),
        ],
        out_shape=[
            jax.ShapeDtypeStruct((NHEADS, LANES, T), bf16),
            jax.ShapeDtypeStruct((NHEADS, T, LANES), bf16),
            jax.ShapeDtypeStruct((NHEADS, VT_ROWS, T), bf16),
            jax.ShapeDtypeStruct((T // tb, 8, LANES), f32),
        ],
        scratch_shapes=[pltpu.VMEM((8, LANES), f32)],
        compiler_params=pltpu.CompilerParams(
            dimension_semantics=("arbitrary",), vmem_limit_bytes=VMEM_LIMIT),
        name="fox_prep",
    )(z, z, z, z, gq, gk, bfp)


def _attn_kernel(cnt_ref, blk_ref, qt_ref, ka_hbm, vt_hbm, fg_ref, o_ref, m_ref, acc_ref, kbuf, vbuf, sem,
                 *, tq, tk):
    g, i = pl.program_id(0), pl.program_id(1)
    nq = pl.num_programs(1)
    ndiag = tq // tk
    max_full = ndiag * nq
    step = g * nq + i
    n_full = cnt_ref[step]

    def block_id(step_, t, n_full_, i_):
        return jnp.where(t < n_full_, blk_ref[step_ * max_full + jnp.minimum(t, max_full - 1)], ndiag * i_ + t - n_full_)

    def copies(g_, j, slot):
        return (pltpu.make_async_copy(ka_hbm.at[pl.ds(2 * g_, 2), pl.ds(j * tk, tk), :], kbuf.at[slot], sem.at[0, slot]),
                pltpu.make_async_copy(vt_hbm.at[pl.ds(2 * g_, 2), :, pl.ds(j * tk, tk)], vbuf.at[slot], sem.at[1, slot]))

    def fetch(g_, j, slot):
        for c in copies(g_, j, slot):
            c.start()

    def wait(slot):
        for c in copies(g, 0, slot):
            c.wait()

    @pl.when(step == 0)
    def _():
        fetch(g, block_id(step, 0, n_full, i), 0)

    m_ref[...] = jnp.full(m_ref.shape, NEG_BIG, f32)
    acc_ref[...] = jnp.zeros_like(acc_ref)

    def update(d, slot):
        ka_ref, vt_ref = kbuf.at[slot], vbuf.at[slot]

        def n_keys(q_lo):
            return tk if d < 0 else max(0, min(tk, q_lo + Q_COL - d * tk))

        chains = [(hh, qs * Q_COL) for qs in range(tq // Q_COL) for hh in range(2) if n_keys(qs * Q_COL) > 0]

        def scores(hh, q_lo):
            nk = n_keys(q_lo)
            st = _dot(ka_ref[hh, 0:nk, :], qt_ref[hh, :, q_lo:q_lo + Q_COL])
            if d >= 0 and d * tk + nk - 1 > q_lo:
                kpos = d * tk + lax.broadcasted_iota(jnp.int32, st.shape, 0)
                qpos = q_lo + lax.broadcasted_iota(jnp.int32, st.shape, 1)
                st = jnp.where(kpos <= qpos, st, NEG_BIG)
            return st

        def consume(hh, q_lo, st):
            qsl = slice(q_lo, q_lo + Q_COL)
            m_old = m_ref[hh, :, qsl]
            m_new = jnp.maximum(m_old, jnp.max(st, axis=0, keepdims=True))
            m_ref[hh, :, qsl] = m_new
            p = jnp.exp2(st - m_new).astype(bf16)
            acc_ref[hh, :, qsl] = (jnp.exp2(m_old - m_new) * acc_ref[hh, :, qsl]
                                   + _dot(vt_ref[hh, :, 0:n_keys(q_lo)], p))

        ahead = 4
        pending = [scores(*c) for c in chains[:ahead]]
        for n, chain in enumerate(chains):
            st = pending.pop(0)
            if n + ahead < len(chains):
                pending.append(scores(*chains[n + ahead]))
            consume(*chain, st)

    def run_block(t, d):
        slot = t % 2
        wait(slot)
        fetch(g, block_id(step, t + 1, n_full, i), 1 - slot)
        update(d, slot)

    def full_block(t, carry):
        run_block(t, -1)
        return carry

    lax.fori_loop(0, n_full, full_block, 0)
    for d in range(ndiag - 1):
        run_block(n_full + d, d)

    last_slot = (n_full + ndiag - 1) % 2
    wait(last_slot)
    update(ndiag - 1, last_slot)

    @pl.when(step + 1 < pl.num_programs(0) * nq)
    def _():
        nstep = step + 1
        fetch(nstep // nq, block_id(nstep, 0, cnt_ref[nstep], nstep % nq), 0)

    outs = []
    for hh in range(2):
        acc = acc_ref[hh]
        outs.append(acc[0:HEAD, :] / acc[HEAD:HEAD + 1, :])
    y = jnp.concatenate(outs, axis=0).T
    gate = fg_ref[...]
    o_ref[...] = (y * gate * _sigmoid(gate)).astype(o_ref.dtype)


def _attention_work_list(fstat, qk_bound, *, T, tb, tq, tk):
    nq, nk = T // tq, T // tk
    fmax_q = fstat[:, 0, :NHEADS].reshape(nq, tq // tb, NHEADS).max(axis=1)
    fmin_k = fstat[:, 1, :NHEADS].reshape(nk, tk // tb, NHEADS).min(axis=1)
    need = (fmax_q[:, None, :] - fmin_k[None, :, :] + 2.0 * qk_bound) > -SKIP_LOG2
    need = jnp.transpose(need[:, :, 0::2] | need[:, :, 1::2], (2, 0, 1))
    before = np.arange(nk)[None, :] < (tq // tk) * np.arange(nq)[:, None]
    need = need & jnp.asarray(before)[None]
    count = jnp.sum(need, axis=2).astype(jnp.int32)
    ids = jnp.argsort(jnp.logical_not(need), axis=2, stable=True).astype(jnp.int32)
    return count.reshape(-1), ids.reshape(-1)


def _attention(qt, ka, vt, z, fstat, qk_bound, *, tb, tq, tk):
    T = ka.shape[1]
    nq = T // tq
    count, ids = _attention_work_list(fstat, qk_bound, T=T, tb=tb, tq=tq, tk=tk)
    fg_col = (Z_FX + 3 * WIDTH) // LANES
    grid_spec = pltpu.PrefetchScalarGridSpec(
        num_scalar_prefetch=2,
        grid=(NPAIRS, nq),
        in_specs=[
            pl.BlockSpec((2, LANES, tq), lambda g, i, cnt, blk: (g, 0, i)),
            pl.BlockSpec(memory_space=pl.ANY),
            pl.BlockSpec(memory_space=pl.ANY),
            pl.BlockSpec((tq, LANES), lambda g, i, cnt, blk: (i, fg_col + g)),
        ],
        out_specs=pl.BlockSpec((tq, LANES), lambda g, i, cnt, blk: (i, g)),
        scratch_shapes=[
            pltpu.VMEM((2, 1, tq), f32), pltpu.VMEM((2, VT_ROWS, tq), f32),
            pltpu.VMEM((2, 2, tk, LANES), bf16), pltpu.VMEM((2, 2, VT_ROWS, tk), bf16),
            pltpu.SemaphoreType.DMA((2, 2)),
        ],
    )
    return pl.pallas_call(
        functools.partial(_attn_kernel, tq=tq, tk=tk),
        grid_spec=grid_spec,
        out_shape=jax.ShapeDtypeStruct((T, WIDTH), bf16),
        compiler_params=pltpu.CompilerParams(
            dimension_semantics=("arbitrary", "arbitrary"), vmem_limit_bytes=VMEM_LIMIT),
        name="fox_attention",
    )(count, ids, qt, ka, vt, z)


P_MU_R, P_MU_K, P_MU_V, P_W0, P_A0, P_KK, P_KA, P_RK, P_LNW, P_LNB = range(10)


def _rwkv_kernel(zr_ref, zk_ref, zv_ref, zg_ref, zl_ref, par_ref, mul_ref, w2_ref, a2_ref,
                 o_ref, s_ref, prev_ref, prevl_ref, *, tb, npair):
    width = npair * LANES

    @pl.when(pl.program_id(1) == 0)
    def _():
        s_ref[...] = jnp.zeros_like(s_ref)
        prev_ref[...] = jnp.zeros_like(prev_ref)
        prevl_ref[...] = jnp.zeros_like(prevl_ref)

    par = par_ref[...]
    prow = lambda k: par[k:k + 1, :]

    def shift(x, mu, last_ref):
        row = lax.broadcasted_iota(jnp.int32, x.shape, 0)
        xp = pltpu.roll(x, 1, axis=0)
        xp = jnp.where(row == 0, last_ref[7:8, :], xp)
        last_ref[...] = x[tb - 8:tb, :]
        return x + (xp - x) * mu

    r = shift(zr_ref[...], prow(P_MU_R), prev_ref.at[0])
    k = shift(zk_ref[...], prow(P_MU_K), prev_ref.at[1])
    v = shift(zv_ref[...], prow(P_MU_V), prev_ref.at[2])
    sl = shift(zl_ref[...], mul_ref[0:1, :], prevl_ref)

    bi = lax.broadcasted_iota(jnp.int32, (LANES, LANES), 0)
    bj = lax.broadcasted_iota(jnp.int32, (LANES, LANES), 1)
    same_head = jnp.where((bi >> 6) == (bj >> 6), 1.0, 0.0).astype(bf16)

    def seg(x):
        return jnp.concatenate(
            [_dot(x[:, pr * LANES:(pr + 1) * LANES].astype(bf16), same_head) for pr in range(npair)], axis=1)

    u = prow(P_W0) + _dot(jnp.tanh(sl).astype(bf16), w2_ref[...])
    ld = -EXP_NEG_HALF * _sigmoid(u)
    av = _sigmoid(prow(P_A0) + _dot(sl.astype(bf16), a2_ref[...]))

    kk = k * prow(P_KK)
    kk = kk * lax.rsqrt(jnp.maximum(seg(kk * kk), 1e-24))
    k2 = k * (1.0 + (av - 1.0) * prow(P_KA))

    ti = lax.broadcasted_iota(jnp.int32, (tb, tb), 0)
    tj = lax.broadcasted_iota(jnp.int32, (tb, tb), 1)
    tri = jnp.where(((ti >> 6) == (tj >> 6)) & (tj <= ti), 1.0, 0.0).astype(bf16)
    ld_hi, ld_lo = _split2(ld)
    cum = _dot(tri, ld_hi) + _dot(tri, ld_lo)

    e_pos = jnp.exp(cum)
    e_neg = jnp.exp(-cum)
    a_t = -kk * jnp.exp(cum - ld)
    b_t = kk * av * e_neg
    k_t = k2 * e_neg
    r_t = r * e_pos

    ri = lax.broadcasted_iota(jnp.int32, (LANES, LANES), 0)
    rj = lax.broadcasted_iota(jnp.int32, (LANES, LANES), 1)
    strict_lower = rj < ri
    eye = jnp.where(ri == rj, 1.0, 0.0)
    qi = lax.broadcasted_iota(jnp.int32, (CHUNK, LANES), 0)
    qj = lax.broadcasted_iota(jnp.int32, (CHUNK, LANES), 1)
    incl_lower = (qj & (CHUNK - 1)) <= qi
    chunk_head0 = qj < HEAD

    def stack(x):
        return jnp.concatenate([jnp.where(chunk_head0, x, 0.0), jnp.where(chunk_head0, 0.0, x)], axis=0)

    def level_mask(b):
        sh = b.bit_length()
        return ((ri >> sh) == (rj >> sh)) & ((ri & (2 * b - 1)) >= b) & ((rj & (2 * b - 1)) < b)

    nchunk = tb // CHUNK
    units = [(pr, c) for pr in range(npair) for c in range(nchunk)]
    rows = lambda c: slice(c * CHUNK, (c + 1) * CHUNK)
    lanes = lambda pr: slice(pr * LANES, (pr + 1) * LANES)
    cut = lambda x, u: x[rows(u[1]), lanes(u[0])]
    cend = {u: cum[(u[1] + 1) * CHUNK - 1:(u[1] + 1) * CHUNK, lanes(u[0])] for u in units}

    am = {u: stack(cut(a_t, u)) for u in units}
    vm = {u: stack(cut(v, u)) for u in units}
    x = {u: _dot_nt(jnp.concatenate([am[u], cut(r_t, u)], axis=0).astype(bf16),
                    jnp.concatenate([stack(cut(b_t, u)), stack(cut(k_t, u))], axis=0).astype(bf16))
         for u in units}
    l_ab = {u: jnp.where(strict_lower, x[u][0:LANES, 0:LANES], 0.0) for u in units}
    l_ak = {u: jnp.where(strict_lower, x[u][0:LANES, LANES:], 0.0) for u in units}
    l_r = {u: jnp.where(jnp.concatenate([incl_lower, incl_lower], axis=1), x[u][LANES:, :], 0.0).astype(bf16)
           for u in units}

    inv = {u: eye + jnp.where(level_mask(1), l_ab[u], 0.0) for u in units}
    b = 2
    while b < CHUNK:
        mask = level_mask(b)
        y_ = {u: _dot(jnp.where(mask, l_ab[u], 0.0).astype(bf16), inv[u].astype(bf16)) for u in units}
        inv = {u: inv[u] + _dot(inv[u].astype(bf16), y_[u].astype(bf16)) for u in units}
        b *= 2

    lv = {u: _dot(l_ak[u].astype(bf16), vm[u].astype(bf16)) for u in units}
    tg = {u: _dot(inv[u].astype(bf16), jnp.concatenate([am[u], lv[u]], axis=1).astype(bf16)) for u in units}
    zeros = jnp.zeros((LANES, LANES), f32)
    ly = {u: _dot(l_r[u], jnp.concatenate(
              [tg[u], jnp.concatenate([zeros, vm[u]], axis=1)], axis=0).astype(bf16)) for u in units}
    bh = {u: stack(cut(kk, u) * cut(av, u) * jnp.exp(cend[u] - cut(cum, u))).astype(bf16) for u in units}
    kh = {u: stack(cut(k2, u) * jnp.exp(cend[u] - cut(cum, u))).astype(bf16) for u in units}
    ag = {u: _dot_tn(tg[u].astype(bf16), bh[u]) for u in units}
    vk = {u: _dot_tn(vm[u].astype(bf16), kh[u]) for u in units}
    q_eff = {u: (cut(r_t, u) + ly[u][:, 0:LANES]).astype(bf16) for u in units}
    a_eff = {u: ag[u][0:LANES, :].astype(bf16) for u in units}
    g_eff = {u: ag[u][LANES:, :] + vk[u] for u in units}

    ys = [[None] * npair for _ in range(nchunk)]
    state = [s_ref[pr] for pr in range(npair)]
    for c in range(nchunk):
        for pr in range(npair):
            u = (pr, c)
            s_bf = state[pr].astype(bf16)
            ys[c][pr] = _dot_nt(q_eff[u], s_bf) + ly[u][:, LANES:]
            state[pr] = state[pr] * jnp.exp(cend[u]) + _dot(s_bf, a_eff[u]) + g_eff[u]
    for pr in range(npair):
        s_ref[pr] = state[pr]

    y = jnp.concatenate([jnp.concatenate(yc, axis=1) for yc in ys], axis=0)
    mean = seg(y) * (1.0 / HEAD)
    d = y - mean
    var = seg(d * d) * (1.0 / HEAD)
    yn = d * lax.rsqrt(var + GN_EPS) * prow(P_LNW) + prow(P_LNB)
    bonus = seg(r * k2 * prow(P_RK)) * v
    g = zg_ref[...]
    o_ref[...] = ((yn + bonus) * g * _sigmoid(g)).astype(o_ref.dtype)


def _rwkv(z, par, mul, w2p, a2p, *, tb, npair):
    T = z.shape[0]
    width = npair * LANES
    nb = WIDTH // width
    zspec = lambda off: pl.BlockSpec((tb, width), lambda g, t: (t, off + g))
    return pl.pallas_call(
        functools.partial(_rwkv_kernel, tb=tb, npair=npair),
        grid=(nb, T // tb),
        in_specs=[
            zspec(0), zspec(nb), zspec(2 * nb), zspec(3 * nb),
            pl.BlockSpec((tb, LANES), lambda g, t: (t, Z_LORA // LANES)),
            pl.BlockSpec((16, width), lambda g, t: (0, g)),
            pl.BlockSpec((8, LANES), lambda g, t: (0, 0)),
            pl.BlockSpec((LANES, width), lambda g, t: (0, g)),
            pl.BlockSpec((LANES, width), lambda g, t: (0, g)),
        ],
        out_specs=pl.BlockSpec((tb, width), lambda g, t: (t, g)),
        out_shape=jax.ShapeDtypeStruct((T, WIDTH), bf16),
        scratch_shapes=[pltpu.VMEM((npair, LANES, LANES), f32), pltpu.VMEM((3, 8, width), f32),
                        pltpu.VMEM((8, LANES), f32)],
        compiler_params=pltpu.CompilerParams(
            dimension_semantics=("parallel", "arbitrary"), vmem_limit_bytes=VMEM_LIMIT),
        name="rwkv7",
    )(z, z, z, z, z, par, mul, w2p, a2p)


def _out_kernel(perm_ref, x_ref, yr_ref, yf_ref, p_ref, wo_ref, wg_ref, wp_ref, g1_ref, g2_ref, o_ref, wfx_ref):
    @pl.when(pl.program_id(0) == 0)
    def _():
        for s in range(NHEADS):
            src = pl.multiple_of(WIDTH + perm_ref[s] * HEAD, HEAD)
            wfx_ref[s * HEAD:(s + 1) * HEAD, :] = wo_ref[pl.ds(src, HEAD), :]

    def rms(t, g):
        return t * lax.rsqrt(jnp.mean(t * t, axis=-1, keepdims=True) + RMS_EPS) * g

    m = _dot(yr_ref[...], wo_ref[0:WIDTH, :]) + _dot(yf_ref[...], wfx_ref[...])
    x1 = x_ref[...] + rms(m, g1_ref[...])
    gate = _sigmoid(_dot(rms(x1, g2_ref[...]).astype(bf16), wg_ref[...]))
    o_ref[...] = x1 + gate * _dot(p_ref[...].astype(bf16), wp_ref[...])


def _out_proj(perm, x, yr, yf, p, wo, wg, wp, g1, g2, *, tm):
    T = x.shape[0]
    const = lambda shape: pl.BlockSpec(shape, lambda i, perm: (0, 0))
    rows = lambda width: pl.BlockSpec((tm, width), lambda i, perm: (i, 0))
    grid_spec = pltpu.PrefetchScalarGridSpec(
        num_scalar_prefetch=1,
        grid=(T // tm,),
        in_specs=[
            rows(D_MODEL), rows(WIDTH), rows(WIDTH), rows(D_PLE),
            const((D_MODEL, D_MODEL)), const((D_MODEL, D_MODEL)), const((D_PLE, D_MODEL)),
            const((1, D_MODEL)), const((1, D_MODEL)),
        ],
        out_specs=rows(D_MODEL),
        scratch_shapes=[pltpu.VMEM((WIDTH, D_MODEL), bf16)],
    )
    return pl.pallas_call(
        _out_kernel,
        grid_spec=grid_spec,
        out_shape=jax.ShapeDtypeStruct((T, D_MODEL), f32),
        compiler_params=pltpu.CompilerParams(
            dimension_semantics=("arbitrary",), vmem_limit_bytes=VMEM_LIMIT),
        name="out_proj",
    )(perm, x, yr, yf, p, wo, wg, wp, g1, g2)


def _layer(x, p, pre_g, w_in, mu_r, mu_k, mu_v, mu_w, mu_a, w0, w2, a0, a2, k_k, k_a, r_k, ln_w, ln_b,
           b_f, q_g, k_g, w_out, post_g, ple_g, w_gate, w_ple):
    T = x.shape[0]
    rw_end = 4 * WIDTH
    lora_end = rw_end + 2 * LORA
    fx_end = lora_end + 4 * WIDTH
    perm = jnp.argsort(b_f).astype(jnp.int32)
    fx_half = lora_end // HEAD
    src_half = jnp.concatenate([
        jnp.arange(rw_end // HEAD, dtype=jnp.int32),
        (fx_half + NHEADS * jnp.arange(4, dtype=jnp.int32)[:, None] + perm[None, :]).reshape(-1),
        jnp.arange(rw_end // HEAD, lora_end // HEAD, dtype=jnp.int32),
        jnp.zeros((2,), jnp.int32)])
    w_in_t = w_in.T
    pick = (jnp.pad(perm, (0, LANES - NHEADS), constant_values=-1)[:, None] == jnp.arange(NHEADS, dtype=jnp.int32)[None, :])
    w_ff_t = jnp.dot(pick.astype(f32), w_in_t[fx_end:], precision=lax.Precision.HIGHEST)
    w = _relayout_w_in(src_half, w_in_t, w_ff_t)
    b_f = jnp.take(b_f, perm)
    z = _in_proj(x, pre_g.reshape(1, D_MODEL), w, tm=min(1024, T), tn=1408)

    gq = jnp.tile(q_g, NHEADS).reshape(1, WIDTH)
    gk = jnp.tile(k_g, NHEADS).reshape(1, WIDTH)
    bfp = jnp.pad(b_f, (0, LANES - NHEADS)).reshape(1, LANES)
    tb = min(256, T)
    qt, ka, vt, fstat = _fox_prep(z, gq, gk, bfp, tb=tb)
    qk_bound = 1.02 * HEAD ** 0.5 * LOG2E * jnp.max(jnp.abs(q_g)) * jnp.max(jnp.abs(k_g))
    y_fx = _attention(qt, ka, vt, z, fstat, qk_bound, tb=tb, tq=min(1024, T), tk=min(512, T))

    par = jnp.stack([mu_r, mu_k, mu_v, w0, a0, k_k, k_a, r_k.reshape(WIDTH), ln_w, ln_b])
    par = jnp.pad(par, ((0, 16 - par.shape[0]), (0, 0)))
    mul = jnp.broadcast_to(jnp.concatenate([mu_w, mu_a]).reshape(1, LANES), (8, LANES))
    zeros = jnp.zeros((LORA, WIDTH), f32)
    w2p = jnp.concatenate([w2, zeros], axis=0).astype(bf16)
    a2p = jnp.concatenate([zeros, a2], axis=0).astype(bf16)
    y_rw = _rwkv(z, par, mul, w2p, a2p, tb=min(256, T), npair=8)

    return _out_proj(perm, x, y_rw, y_fx, p, w_out.astype(bf16), w_gate.astype(bf16), w_ple.astype(bf16),
                     post_g.reshape(1, D_MODEL), ple_g.reshape(1, D_MODEL), tm=min(256, T))


def kernel(x, p, pre_norm_g, w_in, rw_mu_r, rw_mu_k, rw_mu_v, rw_mu_w, rw_mu_a, rw_w0, rw_w2, rw_a0, rw_a2,
           rw_k_k, rw_k_a, rw_r_k, rw_ln_w, rw_ln_b, fx_b_f, fx_q_g, fx_k_g, w_out, post_norm_g, ple_norm_g,
           w_ple_gate, w_ple):
    B = x.shape[0]
    outs = []
    for b in range(B):
        xb = x[b]
        for i in range(p.shape[0]):
            xb = _layer(xb, p[i, b], pre_norm_g[i], w_in[i], rw_mu_r[i], rw_mu_k[i], rw_mu_v[i], rw_mu_w[i],
                        rw_mu_a[i], rw_w0[i], rw_w2[i], rw_a0[i], rw_a2[i], rw_k_k[i], rw_k_a[i], rw_r_k[i],
                        rw_ln_w[i], rw_ln_b[i], fx_b_f[i], fx_q_g[i], fx_k_g[i], w_out[i], post_norm_g[i],
                        ple_norm_g[i], w_ple_gate[i], w_ple[i])
        outs.append(xb)
    return jnp.stack(outs)
```

```python
import functools

import numpy as np
import jax
import jax.numpy as jnp
from jax import lax
from jax.experimental import pallas as pl
from jax.experimental.pallas import tpu as pltpu

f32 = jnp.float32
bf16 = jnp.bfloat16

D_MODEL = 2048
D_PLE = 256
WIDTH = 1024
HEAD = 64
NHEADS = 16
NPAIRS = NHEADS // 2
LORA = 64
LANES = 128
RMS_EPS = 1e-6
GN_EPS = 64e-5
CHUNK = 64
LOG2E = 1.4426950408889634
EXP_NEG_HALF = 0.6065306597126334
NEG_BIG = -1e30

Z_RW = 0
Z_FX = 4096
Z_LORA = 8192
Z_FF = 8320
Z_COLS = 8448
VT_ROWS = 80
Q_COL = 256
SKIP_LOG2 = 48.0

VMEM_LIMIT = 56 * 1024 * 1024


def _dot(a, b):
    return jnp.dot(a, b, preferred_element_type=f32)


def _dot_nt(a, b):
    return lax.dot_general(a, b, (((1,), (1,)), ((), ())), preferred_element_type=f32)


def _dot_tn(a, b):
    return lax.dot_general(a, b, (((0,), (0,)), ((), ())), preferred_element_type=f32)


def _split2(x):
    hi = x.astype(bf16)
    lo = (x - hi.astype(f32)).astype(bf16)
    return hi, lo


def _split3(x):
    hi = x.astype(bf16)
    r = x - hi.astype(f32)
    mid = r.astype(bf16)
    lo = (r - mid.astype(f32)).astype(bf16)
    return hi, mid, lo


def _dot_sel(x, sel):
    hi, mid, lo = _split3(x)
    return _dot(hi, sel) + _dot(mid, sel) + _dot(lo, sel)


def _sel_dot(sel, x):
    hi, mid, lo = _split3(x)
    return _dot(sel, hi) + _dot(sel, mid) + _dot(sel, lo)


def _dot_x3(a, b):
    ah, al = _split2(a)
    bh, bl = _split2(b)
    return _dot(ah, bh) + _dot(al, bh) + _dot(ah, bl)


def _sigmoid(x):
    return 1.0 / (1.0 + jnp.exp(-x))


def _relayout_kernel(src_ref, a_ref, b_ref, ff_ref, o_ref):
    c = pl.program_id(0)
    rows = jnp.concatenate([a_ref[...], b_ref[...]], axis=0)
    rows = jnp.where(c == pl.num_programs(0) - 1, ff_ref[...], rows)
    o_ref[...] = rows.T.astype(bf16)


def _relayout_w_in(src_half, w_in_t, w_ff_t):
    nblk = Z_COLS // LANES
    grid_spec = pltpu.PrefetchScalarGridSpec(
        num_scalar_prefetch=1,
        grid=(nblk,),
        in_specs=[
            pl.BlockSpec((HEAD, D_MODEL), lambda c, src: (src[2 * c], 0)),
            pl.BlockSpec((HEAD, D_MODEL), lambda c, src: (src[2 * c + 1], 0)),
            pl.BlockSpec((LANES, D_MODEL), lambda c, src: (0, 0)),
        ],
        out_specs=pl.BlockSpec((D_MODEL, LANES), lambda c, src: (0, c)),
    )
    return pl.pallas_call(
        _relayout_kernel,
        grid_spec=grid_spec,
        out_shape=jax.ShapeDtypeStruct((D_MODEL, Z_COLS), bf16),
        compiler_params=pltpu.CompilerParams(
            dimension_semantics=("parallel",), vmem_limit_bytes=VMEM_LIMIT),
        name="w_in_relayout",
    )(src_half, w_in_t, w_in_t, w_ff_t)


def _in_proj_kernel(x_ref, g_ref, w_ref, z_ref, h_ref):
    @pl.when(pl.program_id(1) == 0)
    def _():
        x = x_ref[...]
        ms = jnp.mean(x * x, axis=-1, keepdims=True)
        h_ref[...] = (x * lax.rsqrt(ms + RMS_EPS) * g_ref[...]).astype(bf16)

    z_ref[...] = _dot(h_ref[...], w_ref[...])


def _in_proj(x, g, w, *, tm, tn):
    T = x.shape[0]
    return pl.pallas_call(
        _in_proj_kernel,
        grid=(T // tm, Z_COLS // tn),
        in_specs=[
            pl.BlockSpec((tm, D_MODEL), lambda i, j: (i, 0)),
            pl.BlockSpec((1, D_MODEL), lambda i, j: (0, 0)),
            pl.BlockSpec((D_MODEL, tn), lambda i, j: (0, j)),
        ],
        out_specs=pl.BlockSpec((tm, tn), lambda i, j: (i, j)),
        out_shape=jax.ShapeDtypeStruct((T, Z_COLS), f32),
        scratch_shapes=[pltpu.VMEM((tm, D_MODEL), bf16)],
        compiler_params=pltpu.CompilerParams(
            dimension_semantics=("parallel", "arbitrary"), vmem_limit_bytes=VMEM_LIMIT),
        name="in_proj",
    )(x, g, w)


def _fox_prep_kernel(fq_ref, fk_ref, fv_ref, ff_ref, gq_ref, gk_ref, bf_ref,
                     qt_ref, ka_ref, vt_ref, fs_ref, carry_ref, *, tb):
    @pl.when(pl.program_id(0) == 0)
    def _():
        carry_ref[...] = jnp.zeros_like(carry_ref)

    bi = lax.broadcasted_iota(jnp.int32, (LANES, LANES), 0)
    bj = lax.broadcasted_iota(jnp.int32, (LANES, LANES), 1)
    same_head = jnp.where((bi >> 6) == (bj >> 6), 1.0, 0.0).astype(bf16)

    def head_rms(x, gain):
        cols = []
        for c in range(WIDTH // LANES):
            xc = x[:, c * LANES:(c + 1) * LANES]
            ssq = _dot((xc * xc).astype(bf16), same_head)
            cols.append(xc * lax.rsqrt(ssq * (1.0 / HEAD) + RMS_EPS))
        return jnp.concatenate(cols, axis=1) * gain

    qn = head_rms(fq_ref[...], gq_ref[...]) * (HEAD ** -0.5 * LOG2E)
    kn = head_rms(fk_ref[...], gk_ref[...])

    xf = ff_ref[...] + bf_ref[...]
    logf = jnp.minimum(xf, 0.0) - jnp.log(1.0 + jnp.exp(-jnp.abs(xf)))
    ri = lax.broadcasted_iota(jnp.int32, (tb, tb), 0)
    rj = lax.broadcasted_iota(jnp.int32, (tb, tb), 1)
    tri = jnp.where(rj <= ri, 1.0, 0.0).astype(bf16)
    cum = _sel_dot(tri, logf) + carry_ref[0:1, :]
    carry_ref[...] = jnp.broadcast_to(cum[tb - 1:tb, :], carry_ref.shape)
    cum2 = cum * LOG2E
    c_hi, c_mid, c_lo = (t.astype(f32) for t in _split3(cum2))
    srow = lax.broadcasted_iota(jnp.int32, (8, LANES), 0)
    fs_ref[0] = jnp.where(srow == 0, jnp.max(cum2, axis=0, keepdims=True), jnp.min(cum2, axis=0, keepdims=True))

    lane = lax.broadcasted_iota(jnp.int32, (tb, LANES), 1)
    erow = lax.broadcasted_iota(jnp.int32, (HEAD, tb), 0)
    q_ones = jnp.where(erow < 6, 1.0, 0.0)
    qn_t, vt = qn.T, fv_ref[...].T
    hi_t, mid_t, lo_t = c_hi.T, c_mid.T, c_lo.T
    vpad = jnp.concatenate(
        [jnp.ones((1, tb), f32), jnp.zeros((VT_ROWS - HEAD - 1, tb), f32)], axis=0)
    for h in range(NHEADS):
        even = h % 2 == 0
        base = HEAD if even else 0
        q_bias = jnp.where(erow == 0, hi_t[h:h + 1, :], jnp.where(erow == 1, mid_t[h:h + 1, :],
                 jnp.where(erow == 2, lo_t[h:h + 1, :], q_ones)))
        q_rows = qn_t[h * HEAD:(h + 1) * HEAD, :]
        qt_ref[h] = jnp.concatenate([q_rows, q_bias] if even else [q_bias, q_rows], axis=0).astype(bf16)
        fh, fm, fl = c_hi[:, h:h + 1], c_mid[:, h:h + 1], c_lo[:, h:h + 1]
        k_ones = jnp.where((lane >= base) & (lane < base + 3), 1.0, 0.0)
        k_bias = jnp.where(lane == base + 3, -fh, jnp.where(lane == base + 4, -fm,
                 jnp.where(lane == base + 5, -fl, k_ones)))
        kc = kn[:, (h // 2) * LANES:(h // 2 + 1) * LANES]
        ka_ref[h] = jnp.where((lane < HEAD) == even, kc, k_bias).astype(bf16)
        vt_ref[h] = jnp.concatenate([vt[h * HEAD:(h + 1) * HEAD, :], vpad], axis=0).astype(bf16)


def _fox_prep(z, gq, gk, bfp, *, tb):
    T = z.shape[0]
    wb = Z_FX // WIDTH
    return pl.pallas_call(
        functools.partial(_fox_prep_kernel, tb=tb),
        grid=(T // tb,),
        in_specs=[
            pl.BlockSpec((tb, WIDTH), lambda i: (i, wb)),
            pl.BlockSpec((tb, WIDTH), lambda i: (i, wb + 1)),
            pl.BlockSpec((tb, WIDTH), lambda i: (i, wb + 2)),
            pl.BlockSpec((tb, LANES), lambda i: (i, Z_FF // LANES)),
            pl.BlockSpec((1, WIDTH), lambda i: (0, 0)),
            pl.BlockSpec((1, WIDTH), lambda i: (0, 0)),
            pl.BlockSpec((1, LANES), lambda i: (0, 0)),
        ],
        out_specs=[
            pl.BlockSpec((NHEADS, LANES, tb), lambda i: (0, 0, i)),
            pl.BlockSpec((NHEADS, tb, LANES), lambda i: (0, i, 0)),
            pl.BlockSpec((NHEADS, VT_ROWS, tb), lambda i: (0, 0, i)),
            pl.BlockSpec((1, 8, LANES), lambda i: (i, 0, 0)),
        ],
        out_shape=[
            jax.ShapeDtypeStruct((NHEADS, LANES, T), bf16),
            jax.ShapeDtypeStruct((NHEADS, T, LANES), bf16),
            jax.ShapeDtypeStruct((NHEADS, VT_ROWS, T), bf16),
            jax.ShapeDtypeStruct((T // tb, 8, LANES), f32),
        ],
        scratch_shapes=[pltpu.VMEM((8, LANES), f32)],
        compiler_params=pltpu.CompilerParams(
            dimension_semantics=("arbitrary",), vmem_limit_bytes=VMEM_LIMIT),
        name="fox_prep",
    )(z, z, z, z, gq, gk, bfp)


def _attn_kernel(cnt_ref, blk_ref, qt_ref, ka_hbm, vt_hbm, fg_ref, o_ref, m_ref, acc_ref, kbuf, vbuf, sem, par_ref,
                 *, tq, tk):
    g, i = pl.program_id(0), pl.program_id(1)
    nq = pl.num_programs(1)
    ndiag = tq // tk
    max_full = ndiag * nq
    step = g * nq + i
    n_full = cnt_ref[step]

    def block_id(step_, t, n_full_, i_):
        return jnp.where(t < n_full_, blk_ref[step_ * max_full + jnp.minimum(t, max_full - 1)], ndiag * i_ + t - n_full_)

    def copies(g_, j, slot):
        return (pltpu.make_async_copy(ka_hbm.at[pl.ds(2 * g_, 2), pl.ds(j * tk, tk), :], kbuf.at[slot], sem.at[0, slot]),
                pltpu.make_async_copy(vt_hbm.at[pl.ds(2 * g_, 2), :, pl.ds(j * tk, tk)], vbuf.at[slot], sem.at[1, slot]))

    def fetch(g_, j, slot):
        for c in copies(g_, j, slot):
            c.start()

    def wait(slot):
        for c in copies(g, 0, slot):
            c.wait()

    @pl.when(step == 0)
    def _():
        par_ref[0] = 0
        fetch(g, block_id(step, 0, n_full, i), 0)

    slot0 = par_ref[0]
    par_ref[0] = (slot0 + n_full + ndiag) % 2
    m_ref[...] = jnp.full(m_ref.shape, NEG_BIG, f32)
    acc_ref[...] = jnp.zeros_like(acc_ref)

    def update(d, slot):
        ka_ref, vt_ref = kbuf.at[slot], vbuf.at[slot]

        def n_keys(q_lo):
            return tk if d < 0 else max(0, min(tk, q_lo + Q_COL - d * tk))

        chains = [(hh, qs * Q_COL) for qs in range(tq // Q_COL) for hh in range(2) if n_keys(qs * Q_COL) > 0]

        def scores(hh, q_lo):
            nk = n_keys(q_lo)
            st = _dot(ka_ref[hh, 0:nk, :], qt_ref[hh, :, q_lo:q_lo + Q_COL])
            if d >= 0 and d * tk + nk - 1 > q_lo:
                kpos = d * tk + lax.broadcasted_iota(jnp.int32, st.shape, 0)
                qpos = q_lo + lax.broadcasted_iota(jnp.int32, st.shape, 1)
                st = jnp.where(kpos <= qpos, st, NEG_BIG)
            return st

        def consume(hh, q_lo, st):
            qsl = slice(q_lo, q_lo + Q_COL)
            m_old = m_ref[hh, :, qsl]
            m_new = jnp.maximum(m_old, jnp.max(st, axis=0, keepdims=True))
            m_ref[hh, :, qsl] = m_new
            p = jnp.exp2(st - m_new).astype(bf16)
            acc_ref[hh, :, qsl] = (jnp.exp2(m_old - m_new) * acc_ref[hh, :, qsl]
                                   + _dot(vt_ref[hh, :, 0:n_keys(q_lo)], p))

        ahead = 4
        pending = [scores(*c) for c in chains[:ahead]]
        for n, chain in enumerate(chains):
            st = pending.pop(0)
            if n + ahead < len(chains):
                pending.append(scores(*chains[n + ahead]))
            consume(*chain, st)

    def run_block(t, d):
        slot = (slot0 + t) % 2
        wait(slot)
        fetch(g, block_id(step, t + 1, n_full, i), 1 - slot)
        update(d, slot)

    def full_block(t, carry):
        run_block(t, -1)
        return carry

    lax.fori_loop(0, n_full, full_block, 0)
    for d in range(ndiag - 1):
        run_block(n_full + d, d)

    last_slot = (slot0 + n_full + ndiag - 1) % 2
    wait(last_slot)

    @pl.when(step + 1 < pl.num_programs(0) * nq)
    def _():
        nstep = step + 1
        fetch(nstep // nq, block_id(nstep, 0, cnt_ref[nstep], nstep % nq), 1 - last_slot)

    update(ndiag - 1, last_slot)

    outs = []
    for hh in range(2):
        acc = acc_ref[hh]
        outs.append(acc[0:HEAD, :] / acc[HEAD:HEAD + 1, :])
    y = jnp.concatenate(outs, axis=0).T
    gate = fg_ref[...]
    o_ref[...] = (y * gate * _sigmoid(gate)).astype(o_ref.dtype)


def _attention_work_list(fstat, qk_bound, *, T, tb, tq, tk):
    nq, nk = T // tq, T // tk
    fmax_q = fstat[:, 0, :NHEADS].reshape(nq, tq // tb, NHEADS).max(axis=1)
    fmin_k = fstat[:, 1, :NHEADS].reshape(nk, tk // tb, NHEADS).min(axis=1)
    need = (fmax_q[:, None, :] - fmin_k[None, :, :] + 2.0 * qk_bound) > -SKIP_LOG2
    need = jnp.transpose(need[:, :, 0::2] | need[:, :, 1::2], (2, 0, 1))
    before = np.arange(nk)[None, :] < (tq // tk) * np.arange(nq)[:, None]
    need = need & jnp.asarray(before)[None]
    count = jnp.sum(need, axis=2).astype(jnp.int32)
    ids = jnp.argsort(jnp.logical_not(need), axis=2, stable=True).astype(jnp.int32)
    return count.reshape(-1), ids.reshape(-1)


def _attention(qt, ka, vt, z, fstat, qk_bound, *, tb, tq, tk):
    T = ka.shape[1]
    nq = T // tq
    count, ids = _attention_work_list(fstat, qk_bound, T=T, tb=tb, tq=tq, tk=tk)
    fg_col = (Z_FX + 3 * WIDTH) // LANES
    grid_spec = pltpu.PrefetchScalarGridSpec(
        num_scalar_prefetch=2,
        grid=(NPAIRS, nq),
        in_specs=[
            pl.BlockSpec((2, LANES, tq), lambda g, i, cnt, blk: (g, 0, i)),
            pl.BlockSpec(memory_space=pl.ANY),
            pl.BlockSpec(memory_space=pl.ANY),
            pl.BlockSpec((tq, LANES), lambda g, i, cnt, blk: (i, fg_col + g)),
        ],
        out_specs=pl.BlockSpec((tq, LANES), lambda g, i, cnt, blk: (i, g)),
        scratch_shapes=[
            pltpu.VMEM((2, 1, tq), f32), pltpu.VMEM((2, VT_ROWS, tq), f32),
            pltpu.VMEM((2, 2, tk, LANES), bf16), pltpu.VMEM((2, 2, VT_ROWS, tk), bf16),
            pltpu.SemaphoreType.DMA((2, 2)), pltpu.SMEM((1,), jnp.int32),
        ],
    )
    return pl.pallas_call(
        functools.partial(_attn_kernel, tq=tq, tk=tk),
        grid_spec=grid_spec,
        out_shape=jax.ShapeDtypeStruct((T, WIDTH), bf16),
        compiler_params=pltpu.CompilerParams(
            dimension_semantics=("arbitrary", "arbitrary"), vmem_limit_bytes=VMEM_LIMIT),
        name="fox_attention",
    )(count, ids, qt, ka, vt, z)


P_MU_R, P_MU_K, P_MU_V, P_W0, P_A0, P_KK, P_KA, P_RK, P_LNW, P_LNB = range(10)


def _rwkv_kernel(zr_ref, zk_ref, zv_ref, zg_ref, zl_ref, par_ref, mul_ref, w2_ref, a2_ref,
                 o_ref, s_ref, prev_ref, prevl_ref, *, tb, npair):
    width = npair * LANES

    @pl.when(pl.program_id(1) == 0)
    def _():
        s_ref[...] = jnp.zeros_like(s_ref)
        prev_ref[...] = jnp.zeros_like(prev_ref)
        prevl_ref[...] = jnp.zeros_like(prevl_ref)

    par = par_ref[...]
    prow = lambda k: par[k:k + 1, :]

    def shift(x, mu, last_ref):
        row = lax.broadcasted_iota(jnp.int32, x.shape, 0)
        xp = pltpu.roll(x, 1, axis=0)
        xp = jnp.where(row == 0, last_ref[7:8, :], xp)
        last_ref[...] = x[tb - 8:tb, :]
        return x + (xp - x) * mu

    r = shift(zr_ref[...], prow(P_MU_R), prev_ref.at[0])
    k = shift(zk_ref[...], prow(P_MU_K), prev_ref.at[1])
    v = shift(zv_ref[...], prow(P_MU_V), prev_ref.at[2])
    sl = shift(zl_ref[...], mul_ref[0:1, :], prevl_ref)

    bi = lax.broadcasted_iota(jnp.int32, (LANES, LANES), 0)
    bj = lax.broadcasted_iota(jnp.int32, (LANES, LANES), 1)
    same_head = jnp.where((bi >> 6) == (bj >> 6), 1.0, 0.0).astype(bf16)

    def seg(x):
        return jnp.concatenate(
            [_dot(x[:, pr * LANES:(pr + 1) * LANES].astype(bf16), same_head) for pr in range(npair)], axis=1)

    u = prow(P_W0) + _dot(jnp.tanh(sl).astype(bf16), w2_ref[...])
    ld = -EXP_NEG_HALF * _sigmoid(u)
    av = _sigmoid(prow(P_A0) + _dot(sl.astype(bf16), a2_ref[...]))

    kk = k * prow(P_KK)
    kk = kk * lax.rsqrt(jnp.maximum(seg(kk * kk), 1e-24))
    k2 = k * (1.0 + (av - 1.0) * prow(P_KA))

    ti = lax.broadcasted_iota(jnp.int32, (tb, tb), 0)
    tj = lax.broadcasted_iota(jnp.int32, (tb, tb), 1)
    tri = jnp.where(((ti >> 6) == (tj >> 6)) & (tj <= ti), 1.0, 0.0).astype(bf16)
    ld_hi, ld_lo = _split2(ld)
    cum = _dot(tri, ld_hi) + _dot(tri, ld_lo)

    e_pos = jnp.exp(cum)
    e_neg = jnp.exp(-cum)
    a_t = -kk * jnp.exp(cum - ld)
    b_t = kk * av * e_neg
    k_t = k2 * e_neg
    r_t = r * e_pos

    ri = lax.broadcasted_iota(jnp.int32, (LANES, LANES), 0)
    rj = lax.broadcasted_iota(jnp.int32, (LANES, LANES), 1)
    strict_lower = rj < ri
    eye = jnp.where(ri == rj, 1.0, 0.0)
    qi = lax.broadcasted_iota(jnp.int32, (CHUNK, LANES), 0)
    qj = lax.broadcasted_iota(jnp.int32, (CHUNK, LANES), 1)
    incl_lower = (qj & (CHUNK - 1)) <= qi
    chunk_head0 = qj < HEAD

    def stack(x):
        return jnp.concatenate([jnp.where(chunk_head0, x, 0.0), jnp.where(chunk_head0, 0.0, x)], axis=0)

    def level_mask(b):
        sh = b.bit_length()
        return ((ri >> sh) == (rj >> sh)) & ((ri & (2 * b - 1)) >= b) & ((rj & (2 * b - 1)) < b)

    nchunk = tb // CHUNK
    units = [(pr, c) for pr in range(npair) for c in range(nchunk)]
    rows = lambda c: slice(c * CHUNK, (c + 1) * CHUNK)
    lanes = lambda pr: slice(pr * LANES, (pr + 1) * LANES)
    cut = lambda x, u: x[rows(u[1]), lanes(u[0])]
    cend = {u: cum[(u[1] + 1) * CHUNK - 1:(u[1] + 1) * CHUNK, lanes(u[0])] for u in units}

    am = {u: stack(cut(a_t, u)) for u in units}
    vm = {u: stack(cut(v, u)) for u in units}
    x = {u: _dot_nt(jnp.concatenate([am[u], cut(r_t, u)], axis=0).astype(bf16),
                    jnp.concatenate([stack(cut(b_t, u)), stack(cut(k_t, u))], axis=0).astype(bf16))
         for u in units}
    l_ab = {u: jnp.where(strict_lower, x[u][0:LANES, 0:LANES], 0.0) for u in units}
    l_ak = {u: jnp.where(strict_lower, x[u][0:LANES, LANES:], 0.0) for u in units}
    l_r = {u: jnp.where(jnp.concatenate([incl_lower, incl_lower], axis=1), x[u][LANES:, :], 0.0).astype(bf16)
           for u in units}

    inv = {u: eye + jnp.where(level_mask(1), l_ab[u], 0.0) for u in units}
    b = 2
    while b < CHUNK:
        mask = level_mask(b)
        y_ = {u: _dot(jnp.where(mask, l_ab[u], 0.0).astype(bf16), inv[u].astype(bf16)) for u in units}
        inv = {u: inv[u] + _dot(inv[u].astype(bf16), y_[u].astype(bf16)) for u in units}
        b *= 2

    lv = {u: _dot(l_ak[u].astype(bf16), vm[u].astype(bf16)) for u in units}
    tg = {u: _dot(inv[u].astype(bf16), jnp.concatenate([am[u], lv[u]], axis=1).astype(bf16)) for u in units}
    zeros = jnp.zeros((LANES, LANES), f32)
    ly = {u: _dot(l_r[u], jnp.concatenate(
              [tg[u], jnp.concatenate([zeros, vm[u]], axis=1)], axis=0).astype(bf16)) for u in units}
    bh = {u: stack(cut(kk, u) * cut(av, u) * jnp.exp(cend[u] - cut(cum, u))).astype(bf16) for u in units}
    kh = {u: stack(cut(k2, u) * jnp.exp(cend[u] - cut(cum, u))).astype(bf16) for u in units}
    ag = {u: _dot_tn(tg[u].astype(bf16), bh[u]) for u in units}
    vk = {u: _dot_tn(vm[u].astype(bf16), kh[u]) for u in units}
    q_eff = {u: (cut(r_t, u) + ly[u][:, 0:LANES]).astype(bf16) for u in units}
    a_eff = {u: ag[u][0:LANES, :].astype(bf16) for u in units}
    g_eff = {u: ag[u][LANES:, :] + vk[u] for u in units}

    ys = [[None] * npair for _ in range(nchunk)]
    state = [s_ref[pr] for pr in range(npair)]
    for c in range(nchunk):
        for pr in range(npair):
            u = (pr, c)
            s_bf = state[pr].astype(bf16)
            ys[c][pr] = _dot_nt(q_eff[u], s_bf) + ly[u][:, LANES:]
            state[pr] = state[pr] * jnp.exp(cend[u]) + _dot(s_bf, a_eff[u]) + g_eff[u]
    for pr in range(npair):
        s_ref[pr] = state[pr]

    y = jnp.concatenate([jnp.concatenate(yc, axis=1) for yc in ys], axis=0)
    mean = seg(y) * (1.0 / HEAD)
    d = y - mean
    var = seg(d * d) * (1.0 / HEAD)
    yn = d * lax.rsqrt(var + GN_EPS) * prow(P_LNW) + prow(P_LNB)
    bonus = seg(r * k2 * prow(P_RK)) * v
    g = zg_ref[...]
    o_ref[...] = ((yn + bonus) * g * _sigmoid(g)).astype(o_ref.dtype)


def _rwkv(z, par, mul, w2p, a2p, *, tb, npair):
    T = z.shape[0]
    width = npair * LANES
    nb = WIDTH // width
    zspec = lambda off: pl.BlockSpec((tb, width), lambda g, t: (t, off + g))
    return pl.pallas_call(
        functools.partial(_rwkv_kernel, tb=tb, npair=npair),
        grid=(nb, T // tb),
        in_specs=[
            zspec(0), zspec(nb), zspec(2 * nb), zspec(3 * nb),
            pl.BlockSpec((tb, LANES), lambda g, t: (t, Z_LORA // LANES)),
            pl.BlockSpec((16, width), lambda g, t: (0, g)),
            pl.BlockSpec((8, LANES), lambda g, t: (0, 0)),
            pl.BlockSpec((LANES, width), lambda g, t: (0, g)),
            pl.BlockSpec((LANES, width), lambda g, t: (0, g)),
        ],
        out_specs=pl.BlockSpec((tb, width), lambda g, t: (t, g)),
        out_shape=jax.ShapeDtypeStruct((T, WIDTH), bf16),
        scratch_shapes=[pltpu.VMEM((npair, LANES, LANES), f32), pltpu.VMEM((3, 8, width), f32),
                        pltpu.VMEM((8, LANES), f32)],
        compiler_params=pltpu.CompilerParams(
            dimension_semantics=("parallel", "arbitrary"), vmem_limit_bytes=VMEM_LIMIT),
        name="rwkv7",
    )(z, z, z, z, z, par, mul, w2p, a2p)


def _out_kernel(perm_ref, x_ref, yr_ref, yf_ref, p_ref, wo_ref, wg_ref, wp_ref, g1_ref, g2_ref, o_ref, wfx_ref):
    @pl.when(pl.program_id(0) == 0)
    def _():
        for s in range(NHEADS):
            src = pl.multiple_of(WIDTH + perm_ref[s] * HEAD, HEAD)
            wfx_ref[s * HEAD:(s + 1) * HEAD, :] = wo_ref[pl.ds(src, HEAD), :]

    def rms(t, g):
        return t * lax.rsqrt(jnp.mean(t * t, axis=-1, keepdims=True) + RMS_EPS) * g

    m = _dot(yr_ref[...], wo_ref[0:WIDTH, :]) + _dot(yf_ref[...], wfx_ref[...])
    x1 = x_ref[...] + rms(m, g1_ref[...])
    gate = _sigmoid(_dot(rms(x1, g2_ref[...]).astype(bf16), wg_ref[...]))
    o_ref[...] = x1 + gate * _dot(p_ref[...].astype(bf16), wp_ref[...])


def _out_proj(perm, x, yr, yf, p, wo, wg, wp, g1, g2, *, tm):
    T = x.shape[0]
    const = lambda shape: pl.BlockSpec(shape, lambda i, perm: (0, 0))
    rows = lambda width: pl.BlockSpec((tm, width), lambda i, perm: (i, 0))
    grid_spec = pltpu.PrefetchScalarGridSpec(
        num_scalar_prefetch=1,
        grid=(T // tm,),
        in_specs=[
            rows(D_MODEL), rows(WIDTH), rows(WIDTH), rows(D_PLE),
            const((D_MODEL, D_MODEL)), const((D_MODEL, D_MODEL)), const((D_PLE, D_MODEL)),
            const((1, D_MODEL)), const((1, D_MODEL)),
        ],
        out_specs=rows(D_MODEL),
        scratch_shapes=[pltpu.VMEM((WIDTH, D_MODEL), bf16)],
    )
    return pl.pallas_call(
        _out_kernel,
        grid_spec=grid_spec,
        out_shape=jax.ShapeDtypeStruct((T, D_MODEL), f32),
        compiler_params=pltpu.CompilerParams(
            dimension_semantics=("arbitrary",), vmem_limit_bytes=VMEM_LIMIT),
        name="out_proj",
    )(perm, x, yr, yf, p, wo, wg, wp, g1, g2)


def _layer(x, p, pre_g, w_in, mu_r, mu_k, mu_v, mu_w, mu_a, w0, w2, a0, a2, k_k, k_a, r_k, ln_w, ln_b,
           b_f, q_g, k_g, w_out, post_g, ple_g, w_gate, w_ple):
    T = x.shape[0]
    rw_end = 4 * WIDTH
    lora_end = rw_end + 2 * LORA
    fx_end = lora_end + 4 * WIDTH
    perm = jnp.argsort(b_f).astype(jnp.int32)
    fx_half = lora_end // HEAD
    src_half = jnp.concatenate([
        jnp.arange(rw_end // HEAD, dtype=jnp.int32),
        (fx_half + NHEADS * jnp.arange(4, dtype=jnp.int32)[:, None] + perm[None, :]).reshape(-1),
        jnp.arange(rw_end // HEAD, lora_end // HEAD, dtype=jnp.int32),
        jnp.zeros((2,), jnp.int32)])
    w_in_t = w_in.T
    pick = (jnp.pad(perm, (0, LANES - NHEADS), constant_values=-1)[:, None] == jnp.arange(NHEADS, dtype=jnp.int32)[None, :])
    w_ff_t = jnp.dot(pick.astype(f32), w_in_t[fx_end:], precision=lax.Precision.HIGHEST)
    w = _relayout_w_in(src_half, w_in_t, w_ff_t)
    b_f = jnp.take(b_f, perm)
    z = _in_proj(x, pre_g.reshape(1, D_MODEL), w, tm=min(1024, T), tn=1408)

    gq = jnp.tile(q_g, NHEADS).reshape(1, WIDTH)
    gk = jnp.tile(k_g, NHEADS).reshape(1, WIDTH)
    bfp = jnp.pad(b_f, (0, LANES - NHEADS)).reshape(1, LANES)
    tb = min(256, T)
    qt, ka, vt, fstat = _fox_prep(z, gq, gk, bfp, tb=tb)
    qk_bound = 1.02 * HEAD ** 0.5 * LOG2E * jnp.max(jnp.abs(q_g)) * jnp.max(jnp.abs(k_g))
    y_fx = _attention(qt, ka, vt, z, fstat, qk_bound, tb=tb, tq=min(1024, T), tk=min(512, T))

    par = jnp.stack([mu_r, mu_k, mu_v, w0, a0, k_k, k_a, r_k.reshape(WIDTH), ln_w, ln_b])
    par = jnp.pad(par, ((0, 16 - par.shape[0]), (0, 0)))
    mul = jnp.broadcast_to(jnp.concatenate([mu_w, mu_a]).reshape(1, LANES), (8, LANES))
    zeros = jnp.zeros((LORA, WIDTH), f32)
    w2p = jnp.concatenate([w2, zeros], axis=0).astype(bf16)
    a2p = jnp.concatenate([zeros, a2], axis=0).astype(bf16)
    y_rw = _rwkv(z, par, mul, w2p, a2p, tb=min(256, T), npair=8)

    return _out_proj(perm, x, y_rw, y_fx, p, w_out.astype(bf16), w_gate.astype(bf16), w_ple.astype(bf16),
                     post_g.reshape(1, D_MODEL), ple_g.reshape(1, D_MODEL), tm=min(256, T))


def kernel(x, p, pre_norm_g, w_in, rw_mu_r, rw_mu_k, rw_mu_v, rw_mu_w, rw_mu_a, rw_w0, rw_w2, rw_a0, rw_a2,
           rw_k_k, rw_k_a, rw_r_k, rw_ln_w, rw_ln_b, fx_b_f, fx_q_g, fx_k_g, w_out, post_norm_g, ple_norm_g,
           w_ple_gate, w_ple):
    B = x.shape[0]
    outs = []
    for b in range(B):
        xb = x[b]
        for i in range(p.shape[0]):
            xb = _layer(xb, p[i, b], pre_norm_g[i], w_in[i], rw_mu_r[i], rw_mu_k[i], rw_mu_v[i], rw_mu_w[i],
                        rw_mu_a[i], rw_w0[i], rw_w2[i], rw_a0[i], rw_a2[i], rw_k_k[i], rw_k_a[i], rw_r_k[i],
                        rw_ln_w[i], rw_ln_b[i], fx_b_f[i], fx_q_g[i], fx_k_g[i], w_out[i], post_norm_g[i],
                        ple_norm_g[i], w_ple_gate[i], w_ple[i])
        outs.append(xb)
    return jnp.stack(outs)
```

```python
import functools

import numpy as np
import jax
import jax.numpy as jnp
from jax import lax
from jax.experimental import pallas as pl
from jax.experimental.pallas import tpu as pltpu

f32 = jnp.float32
bf16 = jnp.bfloat16

D_MODEL = 2048
D_PLE = 256
WIDTH = 1024
HEAD = 64
NHEADS = 16
NPAIRS = NHEADS // 2
LORA = 64
LANES = 128
RMS_EPS = 1e-6
GN_EPS = 64e-5
CHUNK = 64
LOG2E = 1.4426950408889634
EXP_NEG_HALF = 0.6065306597126334
NEG_BIG = -1e30

Z_RW = 0
Z_FX = 4096
Z_LORA = 8192
Z_FF = 8320
Z_COLS = 8448
VT_ROWS = 80
Q_COL = 256
SKIP_LOG2 = 40.0

VMEM_LIMIT = 56 * 1024 * 1024


def _dot(a, b):
    return jnp.dot(a, b, preferred_element_type=f32)


def _dot_nt(a, b):
    return lax.dot_general(a, b, (((1,), (1,)), ((), ())), preferred_element_type=f32)


def _dot_tn(a, b):
    return lax.dot_general(a, b, (((0,), (0,)), ((), ())), preferred_element_type=f32)


def _split2(x):
    hi = x.astype(bf16)
    lo = (x - hi.astype(f32)).astype(bf16)
    return hi, lo


def _split3(x):
    hi = x.astype(bf16)
    r = x - hi.astype(f32)
    mid = r.astype(bf16)
    lo = (r - mid.astype(f32)).astype(bf16)
    return hi, mid, lo


def _dot_sel(x, sel):
    hi, mid, lo = _split3(x)
    return _dot(hi, sel) + _dot(mid, sel) + _dot(lo, sel)


def _sel_dot(sel, x):
    hi, mid, lo = _split3(x)
    return _dot(sel, hi) + _dot(sel, mid) + _dot(sel, lo)


def _dot_x3(a, b):
    ah, al = _split2(a)
    bh, bl = _split2(b)
    return _dot(ah, bh) + _dot(al, bh) + _dot(ah, bl)


def _sigmoid(x):
    return 1.0 / (1.0 + jnp.exp(-x))


def _relayout_kernel(src_ref, a_ref, b_ref, ff_ref, o_ref):
    c = pl.program_id(0)
    rows = jnp.concatenate([a_ref[...], b_ref[...]], axis=0)
    rows = jnp.where(c == pl.num_programs(0) - 1, ff_ref[...], rows)
    o_ref[...] = rows.T.astype(bf16)


def _relayout_w_in(src_half, w_in_t, w_ff_t):
    nblk = Z_COLS // LANES
    grid_spec = pltpu.PrefetchScalarGridSpec(
        num_scalar_prefetch=1,
        grid=(nblk,),
        in_specs=[
            pl.BlockSpec((HEAD, D_MODEL), lambda c, src: (src[2 * c], 0)),
            pl.BlockSpec((HEAD, D_MODEL), lambda c, src: (src[2 * c + 1], 0)),
            pl.BlockSpec((LANES, D_MODEL), lambda c, src: (0, 0)),
        ],
        out_specs=pl.BlockSpec((D_MODEL, LANES), lambda c, src: (0, c)),
    )
    return pl.pallas_call(
        _relayout_kernel,
        grid_spec=grid_spec,
        out_shape=jax.ShapeDtypeStruct((D_MODEL, Z_COLS), bf16),
        compiler_params=pltpu.CompilerParams(
            dimension_semantics=("parallel",), vmem_limit_bytes=VMEM_LIMIT),
        name="w_in_relayout",
    )(src_half, w_in_t, w_in_t, w_ff_t)


def _in_proj_kernel(x_ref, g_ref, w_ref, z_ref, h_ref):
    @pl.when(pl.program_id(1) == 0)
    def _():
        x = x_ref[...]
        ms = jnp.mean(x * x, axis=-1, keepdims=True)
        h_ref[...] = (x * lax.rsqrt(ms + RMS_EPS) * g_ref[...]).astype(bf16)

    z_ref[...] = _dot(h_ref[...], w_ref[...])


def _in_proj(x, g, w, *, tm, tn):
    T = x.shape[0]
    return pl.pallas_call(
        _in_proj_kernel,
        grid=(T // tm, Z_COLS // tn),
        in_specs=[
            pl.BlockSpec((tm, D_MODEL), lambda i, j: (i, 0)),
            pl.BlockSpec((1, D_MODEL), lambda i, j: (0, 0)),
            pl.BlockSpec((D_MODEL, tn), lambda i, j: (0, j)),
        ],
        out_specs=pl.BlockSpec((tm, tn), lambda i, j: (i, j)),
        out_shape=jax.ShapeDtypeStruct((T, Z_COLS), f32),
        scratch_shapes=[pltpu.VMEM((tm, D_MODEL), bf16)],
        compiler_params=pltpu.CompilerParams(
            dimension_semantics=("parallel", "arbitrary"), vmem_limit_bytes=VMEM_LIMIT),
        name="in_proj",
    )(x, g, w)


def _fox_prep_kernel(fq_ref, fk_ref, fv_ref, ff_ref, gq_ref, gk_ref, bf_ref,
                     qt_ref, ka_ref, vt_ref, fs_ref, carry_ref, *, tb):
    @pl.when(pl.program_id(0) == 0)
    def _():
        carry_ref[...] = jnp.zeros_like(carry_ref)

    bi = lax.broadcasted_iota(jnp.int32, (LANES, LANES), 0)
    bj = lax.broadcasted_iota(jnp.int32, (LANES, LANES), 1)
    same_head = jnp.where((bi >> 6) == (bj >> 6), 1.0, 0.0).astype(bf16)

    def head_rms(x, gain):
        cols = []
        for c in range(WIDTH // LANES):
            xc = x[:, c * LANES:(c + 1) * LANES]
            ssq = _dot((xc * xc).astype(bf16), same_head)
            cols.append(xc * lax.rsqrt(ssq * (1.0 / HEAD) + RMS_EPS))
        return jnp.concatenate(cols, axis=1) * gain

    qn = head_rms(fq_ref[...], gq_ref[...]) * (HEAD ** -0.5 * LOG2E)
    kn = head_rms(fk_ref[...], gk_ref[...])

    xf = ff_ref[...] + bf_ref[...]
    logf = jnp.minimum(xf, 0.0) - jnp.log(1.0 + jnp.exp(-jnp.abs(xf)))
    ri = lax.broadcasted_iota(jnp.int32, (tb, tb), 0)
    rj = lax.broadcasted_iota(jnp.int32, (tb, tb), 1)
    tri = jnp.where(rj <= ri, 1.0, 0.0).astype(bf16)
    cum = _sel_dot(tri, logf) + carry_ref[0:1, :]
    carry_ref[...] = jnp.broadcast_to(cum[tb - 1:tb, :], carry_ref.shape)
    cum2 = cum * LOG2E
    c_hi, c_mid, c_lo = (t.astype(f32) for t in _split3(cum2))
    srow = lax.broadcasted_iota(jnp.int32, (8, LANES), 0)
    fs_ref[0] = jnp.where(srow == 0, jnp.max(cum2, axis=0, keepdims=True), jnp.min(cum2, axis=0, keepdims=True))

    lane = lax.broadcasted_iota(jnp.int32, (tb, LANES), 1)
    erow = lax.broadcasted_iota(jnp.int32, (HEAD, tb), 0)
    q_ones = jnp.where(erow < 6, 1.0, 0.0)
    qn_t, vt = qn.T, fv_ref[...].T
    hi_t, mid_t, lo_t = c_hi.T, c_mid.T, c_lo.T
    vpad = jnp.concatenate(
        [jnp.ones((1, tb), f32), jnp.zeros((VT_ROWS - HEAD - 1, tb), f32)], axis=0)
    for h in range(NHEADS):
        even = h % 2 == 0
        base = HEAD if even else 0
        q_bias = jnp.where(erow == 0, hi_t[h:h + 1, :], jnp.where(erow == 1, mid_t[h:h + 1, :],
                 jnp.where(erow == 2, lo_t[h:h + 1, :], q_ones)))
        q_rows = qn_t[h * HEAD:(h + 1) * HEAD, :]
        qt_ref[h] = jnp.concatenate([q_rows, q_bias] if even else [q_bias, q_rows], axis=0).astype(bf16)
        fh, fm, fl = c_hi[:, h:h + 1], c_mid[:, h:h + 1], c_lo[:, h:h + 1]
        k_ones = jnp.where((lane >= base) & (lane < base + 3), 1.0, 0.0)
        k_bias = jnp.where(lane == base + 3, -fh, jnp.where(lane == base + 4, -fm,
                 jnp.where(lane == base + 5, -fl, k_ones)))
        kc = kn[:, (h // 2) * LANES:(h // 2 + 1) * LANES]
        ka_ref[h] = jnp.where((lane < HEAD) == even, kc, k_bias).astype(bf16)
        vt_ref[h] = jnp.concatenate([vt[h * HEAD:(h + 1) * HEAD, :], vpad], axis=0).astype(bf16)


def _fox_prep(z, gq, gk, bfp, *, tb):
    T = z.shape[0]
    wb = Z_FX // WIDTH
    return pl.pallas_call(
        functools.partial(_fox_prep_kernel, tb=tb),
        grid=(T // tb,),
        in_specs=[
            pl.BlockSpec((tb, WIDTH), lambda i: (i, wb)),
            pl.BlockSpec((tb, WIDTH), lambda i: (i, wb + 1)),
            pl.BlockSpec((tb, WIDTH), lambda i: (i, wb + 2)),
            pl.BlockSpec((tb, LANES), lambda i: (i, Z_FF // LANES)),
            pl.BlockSpec((1, WIDTH), lambda i: (0, 0)),
            pl.BlockSpec((1, WIDTH), lambda i: (0, 0)),
            pl.BlockSpec((1, LANES), lambda i: (0, 0)),
        ],
        out_specs=[
            pl.BlockSpec((NHEADS, LANES, tb), lambda i: (0, 0, i)),
            pl.BlockSpec((NHEADS, tb, LANES), lambda i: (0, i, 0)),
            pl.BlockSpec((NHEADS, VT_ROWS, tb), lambda i: (0, 0, i)),
            pl.BlockSpec((1, 8, LANES), lambda i: (i, 0, 0)),
        ],
        out_shape=[
            jax.ShapeDtypeStruct((NHEADS, LANES, T), bf16),
            jax.ShapeDtypeStruct((NHEADS, T, LANES), bf16),
            jax.ShapeDtypeStruct((NHEADS, VT_ROWS, T), bf16),
            jax.ShapeDtypeStruct((T // tb, 8, LANES), f32),
        ],
        scratch_shapes=[pltpu.VMEM((8, LANES), f32)],
        compiler_params=pltpu.CompilerParams(
            dimension_semantics=("arbitrary",), vmem_limit_bytes=VMEM_LIMIT),
        name="fox_prep",
    )(z, z, z, z, gq, gk, bfp)


def _attn_kernel(cnt_ref, blk_ref, qt_ref, ka_hbm, vt_hbm, fg_ref, o_ref, m_ref, acc_ref, kbuf, vbuf, sem, par_ref,
                 *, tq, tk):
    g, i = pl.program_id(0), pl.program_id(1)
    nq = pl.num_programs(1)
    ndiag = tq // tk
    max_full = ndiag * nq
    step = g * nq + i
    n_full = cnt_ref[step]

    def block_id(step_, t, n_full_, i_):
        return jnp.where(t < n_full_, blk_ref[step_ * max_full + jnp.minimum(t, max_full - 1)], ndiag * i_ + t - n_full_)

    def copies(g_, j, slot):
        return (pltpu.make_async_copy(ka_hbm.at[pl.ds(2 * g_, 2), pl.ds(j * tk, tk), :], kbuf.at[slot], sem.at[0, slot]),
                pltpu.make_async_copy(vt_hbm.at[pl.ds(2 * g_, 2), :, pl.ds(j * tk, tk)], vbuf.at[slot], sem.at[1, slot]))

    def fetch(g_, j, slot):
        for c in copies(g_, j, slot):
            c.start()

    def wait(slot):
        for c in copies(g, 0, slot):
            c.wait()

    @pl.when(step == 0)
    def _():
        par_ref[0] = 0
        fetch(g, block_id(step, 0, n_full, i), 0)

    slot0 = par_ref[0]
    par_ref[0] = (slot0 + n_full + ndiag) % 2
    m_ref[...] = jnp.full(m_ref.shape, NEG_BIG, f32)
    acc_ref[...] = jnp.zeros_like(acc_ref)

    def update(d, slot):
        ka_ref, vt_ref = kbuf.at[slot], vbuf.at[slot]

        def n_keys(q_lo):
            return tk if d < 0 else max(0, min(tk, q_lo + Q_COL - d * tk))

        chains = [(hh, qs * Q_COL) for qs in range(tq // Q_COL) for hh in range(2) if n_keys(qs * Q_COL) > 0]

        def scores(hh, q_lo):
            nk = n_keys(q_lo)
            st = _dot(ka_ref[hh, 0:nk, :], qt_ref[hh, :, q_lo:q_lo + Q_COL])
            if d >= 0 and d * tk + nk - 1 > q_lo:
                kpos = d * tk + lax.broadcasted_iota(jnp.int32, st.shape, 0)
                qpos = q_lo + lax.broadcasted_iota(jnp.int32, st.shape, 1)
                st = jnp.where(kpos <= qpos, st, NEG_BIG)
            return st

        def consume(hh, q_lo, st):
            qsl = slice(q_lo, q_lo + Q_COL)
            m_old = m_ref[hh, :, qsl]
            m_new = jnp.maximum(m_old, jnp.max(st, axis=0, keepdims=True))
            m_ref[hh, :, qsl] = m_new
            p = jnp.exp2(st - m_new).astype(bf16)
            acc_ref[hh, :, qsl] = (jnp.exp2(m_old - m_new) * acc_ref[hh, :, qsl]
                                   + _dot(vt_ref[hh, :, 0:n_keys(q_lo)], p))

        ahead = 4
        pending = [scores(*c) for c in chains[:ahead]]
        for n, chain in enumerate(chains):
            st = pending.pop(0)
            if n + ahead < len(chains):
                pending.append(scores(*chains[n + ahead]))
            consume(*chain, st)

    def run_block(t, d):
        slot = (slot0 + t) % 2
        wait(slot)
        fetch(g, block_id(step, t + 1, n_full, i), 1 - slot)
        update(d, slot)

    def full_block(t, carry):
        run_block(t, -1)
        return carry

    lax.fori_loop(0, n_full, full_block, 0)
    for d in range(ndiag - 1):
        run_block(n_full + d, d)

    last_slot = (slot0 + n_full + ndiag - 1) % 2
    wait(last_slot)

    @pl.when(step + 1 < pl.num_programs(0) * nq)
    def _():
        nstep = step + 1
        fetch(nstep // nq, block_id(nstep, 0, cnt_ref[nstep], nstep % nq), 1 - last_slot)

    update(ndiag - 1, last_slot)

    outs = []
    for hh in range(2):
        acc = acc_ref[hh]
        outs.append(acc[0:HEAD, :] / acc[HEAD:HEAD + 1, :])
    y = jnp.concatenate(outs, axis=0).T
    gate = fg_ref[...]
    o_ref[...] = (y * gate * _sigmoid(gate)).astype(o_ref.dtype)


def _attention_work_list(fstat, qk_bound, *, T, tb, tq, tk):
    nq, nk = T // tq, T // tk
    fmax_q = fstat[:, 0, :NHEADS].reshape(nq, tq // tb, NHEADS).max(axis=1)
    fmin_k = fstat[:, 1, :NHEADS].reshape(nk, tk // tb, NHEADS).min(axis=1)
    need = (fmax_q[:, None, :] - fmin_k[None, :, :] + 2.0 * qk_bound) > -SKIP_LOG2
    need = jnp.transpose(need[:, :, 0::2] | need[:, :, 1::2], (2, 0, 1))
    before = np.arange(nk)[None, :] < (tq // tk) * np.arange(nq)[:, None]
    need = need & jnp.asarray(before)[None]
    count = jnp.sum(need, axis=2).astype(jnp.int32)
    ids = jnp.argsort(jnp.logical_not(need), axis=2, stable=True).astype(jnp.int32)
    return count.reshape(-1), ids.reshape(-1)


def _attention(qt, ka, vt, z, fstat, qk_bound, *, tb, tq, tk):
    T = ka.shape[1]
    nq = T // tq
    count, ids = _attention_work_list(fstat, qk_bound, T=T, tb=tb, tq=tq, tk=tk)
    fg_col = (Z_FX + 3 * WIDTH) // LANES
    grid_spec = pltpu.PrefetchScalarGridSpec(
        num_scalar_prefetch=2,
        grid=(NPAIRS, nq),
        in_specs=[
            pl.BlockSpec((2, LANES, tq), lambda g, i, cnt, blk: (g, 0, i)),
            pl.BlockSpec(memory_space=pl.ANY),
            pl.BlockSpec(memory_space=pl.ANY),
            pl.BlockSpec((tq, LANES), lambda g, i, cnt, blk: (i, fg_col + g)),
        ],
        out_specs=pl.BlockSpec((tq, LANES), lambda g, i, cnt, blk: (i, g)),
        scratch_shapes=[
            pltpu.VMEM((2, 1, tq), f32), pltpu.VMEM((2, VT_ROWS, tq), f32),
            pltpu.VMEM((2, 2, tk, LANES), bf16), pltpu.VMEM((2, 2, VT_ROWS, tk), bf16),
            pltpu.SemaphoreType.DMA((2, 2)), pltpu.SMEM((1,), jnp.int32),
        ],
    )
    return pl.pallas_call(
        functools.partial(_attn_kernel, tq=tq, tk=tk),
        grid_spec=grid_spec,
        out_shape=jax.ShapeDtypeStruct((T, WIDTH), bf16),
        compiler_params=pltpu.CompilerParams(
            dimension_semantics=("arbitrary", "arbitrary"), vmem_limit_bytes=VMEM_LIMIT),
        name="fox_attention",
    )(count, ids, qt, ka, vt, z)


P_MU_R, P_MU_K, P_MU_V, P_W0, P_A0, P_KK, P_KA, P_RK, P_LNW, P_LNB = range(10)


def _rwkv_kernel(zr_ref, zk_ref, zv_ref, zg_ref, zl_ref, par_ref, mul_ref, w2_ref, a2_ref,
                 o_ref, s_ref, prev_ref, prevl_ref, *, tb, npair):
    width = npair * LANES

    @pl.when(pl.program_id(1) == 0)
    def _():
        s_ref[...] = jnp.zeros_like(s_ref)
        prev_ref[...] = jnp.zeros_like(prev_ref)
        prevl_ref[...] = jnp.zeros_like(prevl_ref)

    par = par_ref[...]
    prow = lambda k: par[k:k + 1, :]

    def shift(x, mu, last_ref):
        row = lax.broadcasted_iota(jnp.int32, x.shape, 0)
        xp = pltpu.roll(x, 1, axis=0)
        xp = jnp.where(row == 0, last_ref[7:8, :], xp)
        last_ref[...] = x[tb - 8:tb, :]
        return x + (xp - x) * mu

    r = shift(zr_ref[...], prow(P_MU_R), prev_ref.at[0])
    k = shift(zk_ref[...], prow(P_MU_K), prev_ref.at[1])
    v = shift(zv_ref[...], prow(P_MU_V), prev_ref.at[2])
    sl = shift(zl_ref[...], mul_ref[0:1, :], prevl_ref)

    bi = lax.broadcasted_iota(jnp.int32, (LANES, LANES), 0)
    bj = lax.broadcasted_iota(jnp.int32, (LANES, LANES), 1)
    same_head = jnp.where((bi >> 6) == (bj >> 6), 1.0, 0.0).astype(bf16)

    def seg(x):
        return jnp.concatenate(
            [_dot(x[:, pr * LANES:(pr + 1) * LANES].astype(bf16), same_head) for pr in range(npair)], axis=1)

    u = prow(P_W0) + _dot(jnp.tanh(sl).astype(bf16), w2_ref[...])
    ld = -EXP_NEG_HALF * _sigmoid(u)
    av = _sigmoid(prow(P_A0) + _dot(sl.astype(bf16), a2_ref[...]))

    kk = k * prow(P_KK)
    kk = kk * lax.rsqrt(jnp.maximum(seg(kk * kk), 1e-24))
    k2 = k * (1.0 + (av - 1.0) * prow(P_KA))

    ti = lax.broadcasted_iota(jnp.int32, (tb, tb), 0)
    tj = lax.broadcasted_iota(jnp.int32, (tb, tb), 1)
    tri = jnp.where(((ti >> 6) == (tj >> 6)) & (tj <= ti), 1.0, 0.0).astype(bf16)
    ld_hi, ld_lo = _split2(ld)
    cum = _dot(tri, ld_hi) + _dot(tri, ld_lo)

    e_pos = jnp.exp(cum)
    e_neg = jnp.exp(-cum)
    a_t = -kk * jnp.exp(cum - ld)
    b_t = kk * av * e_neg
    k_t = k2 * e_neg
    r_t = r * e_pos

    ri = lax.broadcasted_iota(jnp.int32, (LANES, LANES), 0)
    rj = lax.broadcasted_iota(jnp.int32, (LANES, LANES), 1)
    strict_lower = rj < ri
    eye = jnp.where(ri == rj, 1.0, 0.0)
    qi = lax.broadcasted_iota(jnp.int32, (CHUNK, LANES), 0)
    qj = lax.broadcasted_iota(jnp.int32, (CHUNK, LANES), 1)
    incl_lower = (qj & (CHUNK - 1)) <= qi
    chunk_head0 = qj < HEAD

    def stack(x):
        return jnp.concatenate([jnp.where(chunk_head0, x, 0.0), jnp.where(chunk_head0, 0.0, x)], axis=0)

    def level_mask(b):
        sh = b.bit_length()
        return ((ri >> sh) == (rj >> sh)) & ((ri & (2 * b - 1)) >= b) & ((rj & (2 * b - 1)) < b)

    nchunk = tb // CHUNK
    units = [(pr, c) for pr in range(npair) for c in range(nchunk)]
    rows = lambda c: slice(c * CHUNK, (c + 1) * CHUNK)
    lanes = lambda pr: slice(pr * LANES, (pr + 1) * LANES)
    cut = lambda x, u: x[rows(u[1]), lanes(u[0])]
    cend = {u: cum[(u[1] + 1) * CHUNK - 1:(u[1] + 1) * CHUNK, lanes(u[0])] for u in units}

    am = {u: stack(cut(a_t, u)) for u in units}
    vm = {u: stack(cut(v, u)) for u in units}
    x = {u: _dot_nt(jnp.concatenate([am[u], cut(r_t, u)], axis=0).astype(bf16),
                    jnp.concatenate([stack(cut(b_t, u)), stack(cut(k_t, u))], axis=0).astype(bf16))
         for u in units}
    l_ab = {u: jnp.where(strict_lower, x[u][0:LANES, 0:LANES], 0.0) for u in units}
    l_ak = {u: jnp.where(strict_lower, x[u][0:LANES, LANES:], 0.0) for u in units}
    l_r = {u: jnp.where(jnp.concatenate([incl_lower, incl_lower], axis=1), x[u][LANES:, :], 0.0).astype(bf16)
           for u in units}

    inv = {u: eye + jnp.where(level_mask(1), l_ab[u], 0.0) for u in units}
    b = 2
    while b < CHUNK:
        mask = level_mask(b)
        y_ = {u: _dot(jnp.where(mask, l_ab[u], 0.0).astype(bf16), inv[u].astype(bf16)) for u in units}
        inv = {u: inv[u] + _dot(inv[u].astype(bf16), y_[u].astype(bf16)) for u in units}
        b *= 2

    lv = {u: _dot(l_ak[u].astype(bf16), vm[u].astype(bf16)) for u in units}
    tg = {u: _dot(inv[u].astype(bf16), jnp.concatenate([am[u], lv[u]], axis=1).astype(bf16)) for u in units}
    zeros = jnp.zeros((LANES, LANES), f32)
    ly = {u: _dot(l_r[u], jnp.concatenate(
              [tg[u], jnp.concatenate([zeros, vm[u]], axis=1)], axis=0).astype(bf16)) for u in units}
    bh = {u: stack(cut(kk, u) * cut(av, u) * jnp.exp(cend[u] - cut(cum, u))).astype(bf16) for u in units}
    kh = {u: stack(cut(k2, u) * jnp.exp(cend[u] - cut(cum, u))).astype(bf16) for u in units}
    ag = {u: _dot_tn(tg[u].astype(bf16), bh[u]) for u in units}
    vk = {u: _dot_tn(vm[u].astype(bf16), kh[u]) for u in units}
    q_eff = {u: (cut(r_t, u) + ly[u][:, 0:LANES]).astype(bf16) for u in units}
    a_eff = {u: ag[u][0:LANES, :].astype(bf16) for u in units}
    g_eff = {u: ag[u][LANES:, :] + vk[u] for u in units}

    ys = [[None] * npair for _ in range(nchunk)]
    state = [s_ref[pr] for pr in range(npair)]
    for c in range(nchunk):
        for pr in range(npair):
            u = (pr, c)
            s_bf = state[pr].astype(bf16)
            ys[c][pr] = _dot_nt(q_eff[u], s_bf) + ly[u][:, LANES:]
            state[pr] = state[pr] * jnp.exp(cend[u]) + _dot(s_bf, a_eff[u]) + g_eff[u]
    for pr in range(npair):
        s_ref[pr] = state[pr]

    y = jnp.concatenate([jnp.concatenate(yc, axis=1) for yc in ys], axis=0)
    mean = seg(y) * (1.0 / HEAD)
    d = y - mean
    var = seg(d * d) * (1.0 / HEAD)
    yn = d * lax.rsqrt(var + GN_EPS) * prow(P_LNW) + prow(P_LNB)
    bonus = seg(r * k2 * prow(P_RK)) * v
    g = zg_ref[...]
    o_ref[...] = ((yn + bonus) * g * _sigmoid(g)).astype(o_ref.dtype)


def _rwkv(z, par, mul, w2p, a2p, *, tb, npair):
    T = z.shape[0]
    width = npair * LANES
    nb = WIDTH // width
    zspec = lambda off: pl.BlockSpec((tb, width), lambda g, t: (t, off + g))
    return pl.pallas_call(
        functools.partial(_rwkv_kernel, tb=tb, npair=npair),
        grid=(nb, T // tb),
        in_specs=[
            zspec(0), zspec(nb), zspec(2 * nb), zspec(3 * nb),
            pl.BlockSpec((tb, LANES), lambda g, t: (t, Z_LORA // LANES)),
            pl.BlockSpec((16, width), lambda g, t: (0, g)),
            pl.BlockSpec((8, LANES), lambda g, t: (0, 0)),
            pl.BlockSpec((LANES, width), lambda g, t: (0, g)),
            pl.BlockSpec((LANES, width), lambda g, t: (0, g)),
        ],
        out_specs=pl.BlockSpec((tb, width), lambda g, t: (t, g)),
        out_shape=jax.ShapeDtypeStruct((T, WIDTH), bf16),
        scratch_shapes=[pltpu.VMEM((npair, LANES, LANES), f32), pltpu.VMEM((3, 8, width), f32),
                        pltpu.VMEM((8, LANES), f32)],
        compiler_params=pltpu.CompilerParams(
            dimension_semantics=("parallel", "arbitrary"), vmem_limit_bytes=VMEM_LIMIT),
        name="rwkv7",
    )(z, z, z, z, z, par, mul, w2p, a2p)


def _out_kernel(perm_ref, x_ref, yr_ref, yf_ref, p_ref, wo_ref, wg_ref, wp_ref, g1_ref, g2_ref, o_ref, wfx_ref):
    @pl.when(pl.program_id(0) == 0)
    def _():
        for s in range(NHEADS):
            src = pl.multiple_of(WIDTH + perm_ref[s] * HEAD, HEAD)
            wfx_ref[s * HEAD:(s + 1) * HEAD, :] = wo_ref[pl.ds(src, HEAD), :]

    def rms(t, g):
        return t * lax.rsqrt(jnp.mean(t * t, axis=-1, keepdims=True) + RMS_EPS) * g

    m = _dot(yr_ref[...], wo_ref[0:WIDTH, :]) + _dot(yf_ref[...], wfx_ref[...])
    x1 = x_ref[...] + rms(m, g1_ref[...])
    gate = _sigmoid(_dot(rms(x1, g2_ref[...]).astype(bf16), wg_ref[...]))
    o_ref[...] = x1 + gate * _dot(p_ref[...].astype(bf16), wp_ref[...])


def _out_proj(perm, x, yr, yf, p, wo, wg, wp, g1, g2, *, tm):
    T = x.shape[0]
    const = lambda shape: pl.BlockSpec(shape, lambda i, perm: (0, 0), pipeline_mode=pl.Buffered(1))
    rows = lambda width: pl.BlockSpec((tm, width), lambda i, perm: (i, 0))
    grid_spec = pltpu.PrefetchScalarGridSpec(
        num_scalar_prefetch=1,
        grid=(T // tm,),
        in_specs=[
            rows(D_MODEL), rows(WIDTH), rows(WIDTH), rows(D_PLE),
            const((D_MODEL, D_MODEL)), const((D_MODEL, D_MODEL)), const((D_PLE, D_MODEL)),
            const((1, D_MODEL)), const((1, D_MODEL)),
        ],
        out_specs=rows(D_MODEL),
        scratch_shapes=[pltpu.VMEM((WIDTH, D_MODEL), bf16)],
    )
    return pl.pallas_call(
        _out_kernel,
        grid_spec=grid_spec,
        out_shape=jax.ShapeDtypeStruct((T, D_MODEL), f32),
        compiler_params=pltpu.CompilerParams(
            dimension_semantics=("arbitrary",), vmem_limit_bytes=VMEM_LIMIT),
        name="out_proj",
    )(perm, x, yr, yf, p, wo, wg, wp, g1, g2)


def _layer(x, p, pre_g, w_in, mu_r, mu_k, mu_v, mu_w, mu_a, w0, w2, a0, a2, k_k, k_a, r_k, ln_w, ln_b,
           b_f, q_g, k_g, w_out, post_g, ple_g, w_gate, w_ple):
    T = x.shape[0]
    rw_end = 4 * WIDTH
    lora_end = rw_end + 2 * LORA
    fx_end = lora_end + 4 * WIDTH
    perm = jnp.argsort(b_f).astype(jnp.int32)
    fx_half = lora_end // HEAD
    src_half = jnp.concatenate([
        jnp.arange(rw_end // HEAD, dtype=jnp.int32),
        (fx_half + NHEADS * jnp.arange(4, dtype=jnp.int32)[:, None] + perm[None, :]).reshape(-1),
        jnp.arange(rw_end // HEAD, lora_end // HEAD, dtype=jnp.int32),
        jnp.zeros((2,), jnp.int32)])
    w_in_t = w_in.T
    pick = (jnp.pad(perm, (0, LANES - NHEADS), constant_values=-1)[:, None] == jnp.arange(NHEADS, dtype=jnp.int32)[None, :])
    w_ff_t = jnp.dot(pick.astype(f32), w_in_t[fx_end:], precision=lax.Precision.HIGHEST)
    w = _relayout_w_in(src_half, w_in_t, w_ff_t)
    b_f = jnp.take(b_f, perm)
    z = _in_proj(x, pre_g.reshape(1, D_MODEL), w, tm=min(1024, T), tn=1408)

    gq = jnp.tile(q_g, NHEADS).reshape(1, WIDTH)
    gk = jnp.tile(k_g, NHEADS).reshape(1, WIDTH)
    bfp = jnp.pad(b_f, (0, LANES - NHEADS)).reshape(1, LANES)
    tb = min(256, T)
    qt, ka, vt, fstat = _fox_prep(z, gq, gk, bfp, tb=tb)
    qk_bound = 1.02 * HEAD ** 0.5 * LOG2E * jnp.max(jnp.abs(q_g)) * jnp.max(jnp.abs(k_g))
    y_fx = _attention(qt, ka, vt, z, fstat, qk_bound, tb=tb, tq=min(1024, T), tk=min(512, T))

    par = jnp.stack([mu_r, mu_k, mu_v, w0, a0, k_k, k_a, r_k.reshape(WIDTH), ln_w, ln_b])
    par = jnp.pad(par, ((0, 16 - par.shape[0]), (0, 0)))
    mul = jnp.broadcast_to(jnp.concatenate([mu_w, mu_a]).reshape(1, LANES), (8, LANES))
    zeros = jnp.zeros((LORA, WIDTH), f32)
    w2p = jnp.concatenate([w2, zeros], axis=0).astype(bf16)
    a2p = jnp.concatenate([zeros, a2], axis=0).astype(bf16)
    y_rw = _rwkv(z, par, mul, w2p, a2p, tb=min(256, T), npair=8)

    return _out_proj(perm, x, y_rw, y_fx, p, w_out.astype(bf16), w_gate.astype(bf16), w_ple.astype(bf16),
                     post_g.reshape(1, D_MODEL), ple_g.reshape(1, D_MODEL), tm=min(512, T))


def kernel(x, p, pre_norm_g, w_in, rw_mu_r, rw_mu_k, rw_mu_v, rw_mu_w, rw_mu_a, rw_w0, rw_w2, rw_a0, rw_a2,
           rw_k_k, rw_k_a, rw_r_k, rw_ln_w, rw_ln_b, fx_b_f, fx_q_g, fx_k_g, w_out, post_norm_g, ple_norm_g,
           w_ple_gate, w_ple):
    B = x.shape[0]
    outs = []
    for b in range(B):
        xb = x[b]
        for i in range(p.shape[0]):
            xb = _layer(xb, p[i, b], pre_norm_g[i], w_in[i], rw_mu_r[i], rw_mu_k[i], rw_mu_v[i], rw_mu_w[i],
                        rw_mu_a[i], rw_w0[i], rw_w2[i], rw_a0[i], rw_a2[i], rw_k_k[i], rw_k_a[i], rw_r_k[i],
                        rw_ln_w[i], rw_ln_b[i], fx_b_f[i], fx_q_g[i], fx_k_g[i], w_out[i], post_norm_g[i],
                        ple_norm_g[i], w_ple_gate[i], w_ple[i])
        outs.append(xb)
    return jnp.stack(outs)
```

```python
import functools

import numpy as np
import jax
import jax.numpy as jnp
from jax import lax
from jax.experimental import pallas as pl
from jax.experimental.pallas import tpu as pltpu

f32 = jnp.float32
bf16 = jnp.bfloat16

D_MODEL = 2048
D_PLE = 256
WIDTH = 1024
HEAD = 64
NHEADS = 16
NPAIRS = NHEADS // 2
LORA = 64
LANES = 128
RMS_EPS = 1e-6
GN_EPS = 64e-5
CHUNK = 64
LOG2E = 1.4426950408889634
EXP_NEG_HALF = 0.6065306597126334
NEG_BIG = -1e30

Z_RW = 0
Z_FX = 4096
Z_LORA = 8192
Z_FF = 8320
Z_COLS = 8448
VT_ROWS = 80
Q_COL = 256
SKIP_LOG2 = 40.0

VMEM_LIMIT = 56 * 1024 * 1024


def _dot(a, b):
    return jnp.dot(a, b, preferred_element_type=f32)


def _dot_nt(a, b):
    return lax.dot_general(a, b, (((1,), (1,)), ((), ())), preferred_element_type=f32)


def _dot_tn(a, b):
    return lax.dot_general(a, b, (((0,), (0,)), ((), ())), preferred_element_type=f32)


def _split2(x):
    hi = x.astype(bf16)
    lo = (x - hi.astype(f32)).astype(bf16)
    return hi, lo


def _split3(x):
    hi = x.astype(bf16)
    r = x - hi.astype(f32)
    mid = r.astype(bf16)
    lo = (r - mid.astype(f32)).astype(bf16)
    return hi, mid, lo


def _dot_sel(x, sel):
    hi, mid, lo = _split3(x)
    return _dot(hi, sel) + _dot(mid, sel) + _dot(lo, sel)


def _sel_dot(sel, x):
    hi, mid, lo = _split3(x)
    return _dot(sel, hi) + _dot(sel, mid) + _dot(sel, lo)


def _dot_x3(a, b):
    ah, al = _split2(a)
    bh, bl = _split2(b)
    return _dot(ah, bh) + _dot(al, bh) + _dot(ah, bl)


def _sigmoid(x):
    return 1.0 / (1.0 + jnp.exp(-x))


RELAYOUT_SLABS = 4


def _relayout_kernel(src_ref, *refs):
    slab_refs, ff_ref, o_ref = refs[:RELAYOUT_SLABS], refs[RELAYOUT_SLABS], refs[RELAYOUT_SLABS + 1]
    slabs = [r[...] for r in slab_refs]
    is_last = pl.program_id(0) == pl.num_programs(0) - 1
    slabs[-2] = jnp.where(is_last, ff_ref[0:HEAD, :], slabs[-2])
    slabs[-1] = jnp.where(is_last, ff_ref[HEAD:, :], slabs[-1])
    o_ref[...] = jnp.concatenate(slabs, axis=0).T.astype(bf16)


def _relayout_w_in(src_half, w_in_t, w_ff_t):
    n = RELAYOUT_SLABS
    slab = lambda k: pl.BlockSpec((HEAD, D_MODEL), lambda c, src: (src[n * c + k], 0))
    grid_spec = pltpu.PrefetchScalarGridSpec(
        num_scalar_prefetch=1,
        grid=(Z_COLS // (n * HEAD),),
        in_specs=[slab(k) for k in range(n)] + [pl.BlockSpec((LANES, D_MODEL), lambda c, src: (0, 0))],
        out_specs=pl.BlockSpec((D_MODEL, n * HEAD), lambda c, src: (0, c)),
    )
    return pl.pallas_call(
        _relayout_kernel,
        grid_spec=grid_spec,
        out_shape=jax.ShapeDtypeStruct((D_MODEL, Z_COLS), bf16),
        compiler_params=pltpu.CompilerParams(
            dimension_semantics=("parallel",), vmem_limit_bytes=VMEM_LIMIT),
        name="w_in_relayout",
    )(src_half, *([w_in_t] * n), w_ff_t)


def _in_proj_kernel(x_ref, g_ref, w_ref, z_ref, h_ref):
    @pl.when(pl.program_id(1) == 0)
    def _():
        x = x_ref[...]
        ms = jnp.mean(x * x, axis=-1, keepdims=True)
        h_ref[...] = (x * lax.rsqrt(ms + RMS_EPS) * g_ref[...]).astype(bf16)

    z_ref[...] = _dot(h_ref[...], w_ref[...])


def _in_proj(x, g, w, *, tm, tn):
    T = x.shape[0]
    return pl.pallas_call(
        _in_proj_kernel,
        grid=(T // tm, Z_COLS // tn),
        in_specs=[
            pl.BlockSpec((tm, D_MODEL), lambda i, j: (i, 0)),
            pl.BlockSpec((1, D_MODEL), lambda i, j: (0, 0)),
            pl.BlockSpec((D_MODEL, tn), lambda i, j: (0, j)),
        ],
        out_specs=pl.BlockSpec((tm, tn), lambda i, j: (i, j)),
        out_shape=jax.ShapeDtypeStruct((T, Z_COLS), f32),
        scratch_shapes=[pltpu.VMEM((tm, D_MODEL), bf16)],
        compiler_params=pltpu.CompilerParams(
            dimension_semantics=("parallel", "arbitrary"), vmem_limit_bytes=VMEM_LIMIT),
        name="in_proj",
    )(x, g, w)


def _fox_prep_kernel(fq_ref, fk_ref, fv_ref, ff_ref, gq_ref, gk_ref, bf_ref,
                     qt_ref, ka_ref, vt_ref, fs_ref, os_ref, carry_ref, *, tb):
    @pl.when(pl.program_id(0) == 0)
    def _():
        carry_ref[...] = jnp.zeros_like(carry_ref)

    bi = lax.broadcasted_iota(jnp.int32, (LANES, LANES), 0)
    bj = lax.broadcasted_iota(jnp.int32, (LANES, LANES), 1)
    same_head = jnp.where((bi >> 6) == (bj >> 6), 1.0, 0.0).astype(bf16)

    def head_rms(x, gain):
        cols = []
        for c in range(WIDTH // LANES):
            xc = x[:, c * LANES:(c + 1) * LANES]
            ssq = _dot((xc * xc).astype(bf16), same_head)
            cols.append(xc * lax.rsqrt(ssq * (1.0 / HEAD) + RMS_EPS))
        return jnp.concatenate(cols, axis=1) * gain

    qn = head_rms(fq_ref[...], gq_ref[...]) * (HEAD ** -0.5 * LOG2E)
    kn = head_rms(fk_ref[...], gk_ref[...])

    xf = ff_ref[...] + bf_ref[...]
    logf = jnp.minimum(xf, 0.0) - jnp.log(1.0 + jnp.exp(-jnp.abs(xf)))
    ri = lax.broadcasted_iota(jnp.int32, (tb, tb), 0)
    rj = lax.broadcasted_iota(jnp.int32, (tb, tb), 1)
    tri = jnp.where(rj <= ri, 1.0, 0.0).astype(bf16)
    cum = _sel_dot(tri, logf) + carry_ref[0:1, :]
    carry_ref[...] = jnp.broadcast_to(cum[tb - 1:tb, :], carry_ref.shape)
    cum2 = cum * LOG2E
    c_hi, c_mid, c_lo = (t.astype(f32) for t in _split3(cum2))
    srow = lax.broadcasted_iota(jnp.int32, (8, LANES), 0)
    fs_ref[0] = jnp.where(srow == 0, jnp.max(cum2, axis=0, keepdims=True), jnp.min(cum2, axis=0, keepdims=True))
    own = qn * kn
    own = jnp.concatenate([_dot(own[:, c * LANES:(c + 1) * LANES].astype(bf16), same_head)
                           for c in range(WIDTH // LANES)], axis=1)
    os_ref[0] = jnp.broadcast_to(jnp.min(own, axis=0, keepdims=True), (8, WIDTH))

    lane = lax.broadcasted_iota(jnp.int32, (tb, LANES), 1)
    erow = lax.broadcasted_iota(jnp.int32, (HEAD, tb), 0)
    q_ones = jnp.where(erow < 6, 1.0, 0.0)
    qn_t, vt = qn.T, fv_ref[...].T
    hi_t, mid_t, lo_t = c_hi.T, c_mid.T, c_lo.T
    vpad = jnp.concatenate(
        [jnp.ones((1, tb), f32), jnp.zeros((VT_ROWS - HEAD - 1, tb), f32)], axis=0)
    for h in range(NHEADS):
        even = h % 2 == 0
        base = HEAD if even else 0
        q_bias = jnp.where(erow == 0, hi_t[h:h + 1, :], jnp.where(erow == 1, mid_t[h:h + 1, :],
                 jnp.where(erow == 2, lo_t[h:h + 1, :], q_ones)))
        q_rows = qn_t[h * HEAD:(h + 1) * HEAD, :]
        qt_ref[h] = jnp.concatenate([q_rows, q_bias] if even else [q_bias, q_rows], axis=0).astype(bf16)
        fh, fm, fl = c_hi[:, h:h + 1], c_mid[:, h:h + 1], c_lo[:, h:h + 1]
        k_ones = jnp.where((lane >= base) & (lane < base + 3), 1.0, 0.0)
        k_bias = jnp.where(lane == base + 3, -fh, jnp.where(lane == base + 4, -fm,
                 jnp.where(lane == base + 5, -fl, k_ones)))
        kc = kn[:, (h // 2) * LANES:(h // 2 + 1) * LANES]
        ka_ref[h] = jnp.where((lane < HEAD) == even, kc, k_bias).astype(bf16)
        vt_ref[h] = jnp.concatenate([vt[h * HEAD:(h + 1) * HEAD, :], vpad], axis=0).astype(bf16)


def _fox_prep(z, gq, gk, bfp, *, tb):
    T = z.shape[0]
    wb = Z_FX // WIDTH
    return pl.pallas_call(
        functools.partial(_fox_prep_kernel, tb=tb),
        grid=(T // tb,),
        in_specs=[
            pl.BlockSpec((tb, WIDTH), lambda i: (i, wb)),
            pl.BlockSpec((tb, WIDTH), lambda i: (i, wb + 1)),
            pl.BlockSpec((tb, WIDTH), lambda i: (i, wb + 2)),
            pl.BlockSpec((tb, LANES), lambda i: (i, Z_FF // LANES)),
            pl.BlockSpec((1, WIDTH), lambda i: (0, 0)),
            pl.BlockSpec((1, WIDTH), lambda i: (0, 0)),
            pl.BlockSpec((1, LANES), lambda i: (0, 0)),
        ],
        out_specs=[
            pl.BlockSpec((NHEADS, LANES, tb), lambda i: (0, 0, i)),
            pl.BlockSpec((NHEADS, tb, LANES), lambda i: (0, i, 0)),
            pl.BlockSpec((NHEADS, VT_ROWS, tb), lambda i: (0, 0, i)),
            pl.BlockSpec((1, 8, LANES), lambda i: (i, 0, 0)),
            pl.BlockSpec((1, 8, WIDTH), lambda i: (i, 0, 0)),
        ],
        out_shape=[
            jax.ShapeDtypeStruct((NHEADS, LANES, T), bf16),
            jax.ShapeDtypeStruct((NHEADS, T, LANES), bf16),
            jax.ShapeDtypeStruct((NHEADS, VT_ROWS, T), bf16),
            jax.ShapeDtypeStruct((T // tb, 8, LANES), f32),
            jax.ShapeDtypeStruct((T // tb, 8, WIDTH), f32),
        ],
        scratch_shapes=[pltpu.VMEM((8, LANES), f32)],
        compiler_params=pltpu.CompilerParams(
            dimension_semantics=("arbitrary",), vmem_limit_bytes=VMEM_LIMIT),
        name="fox_prep",
    )(z, z, z, z, gq, gk, bfp)


def _attn_kernel(cnt_ref, blk_ref, qt_ref, ka_hbm, vt_hbm, fg_ref, o_ref, m_ref, acc_ref, kbuf, vbuf, sem, par_ref,
                 *, tq, tk):
    g, i = pl.program_id(0), pl.program_id(1)
    nq = pl.num_programs(1)
    ndiag = tq // tk
    max_full = ndiag * nq
    step = g * nq + i
    n_full = cnt_ref[step]

    def block_id(step_, t, n_full_, i_):
        return jnp.where(t < n_full_, blk_ref[step_ * max_full + jnp.minimum(t, max_full - 1)], ndiag * i_ + t - n_full_)

    def copies(g_, j, slot):
        return (pltpu.make_async_copy(ka_hbm.at[pl.ds(2 * g_, 2), pl.ds(j * tk, tk), :], kbuf.at[slot], sem.at[0, slot]),
                pltpu.make_async_copy(vt_hbm.at[pl.ds(2 * g_, 2), :, pl.ds(j * tk, tk)], vbuf.at[slot], sem.at[1, slot]))

    def fetch(g_, j, slot):
        for c in copies(g_, j, slot):
            c.start()

    def wait(slot):
        for c in copies(g, 0, slot):
            c.wait()

    @pl.when(step == 0)
    def _():
        par_ref[0] = 0
        fetch(g, block_id(step, 0, n_full, i), 0)

    slot0 = par_ref[0]
    par_ref[0] = (slot0 + n_full + ndiag) % 2
    m_ref[...] = jnp.full(m_ref.shape, NEG_BIG, f32)
    acc_ref[...] = jnp.zeros_like(acc_ref)

    def update(d, slot):
        ka_ref, vt_ref = kbuf.at[slot], vbuf.at[slot]

        def n_keys(q_lo):
            return tk if d < 0 else max(0, min(tk, q_lo + Q_COL - d * tk))

        chains = [(hh, qs * Q_COL) for qs in range(tq // Q_COL) for hh in range(2) if n_keys(qs * Q_COL) > 0]

        def scores(hh, q_lo):
            nk = n_keys(q_lo)
            st = _dot(ka_ref[hh, 0:nk, :], qt_ref[hh, :, q_lo:q_lo + Q_COL])
            if d >= 0 and d * tk + nk - 1 > q_lo:
                kpos = d * tk + lax.broadcasted_iota(jnp.int32, st.shape, 0)
                qpos = q_lo + lax.broadcasted_iota(jnp.int32, st.shape, 1)
                st = jnp.where(kpos <= qpos, st, NEG_BIG)
            return st

        def consume(hh, q_lo, st):
            qsl = slice(q_lo, q_lo + Q_COL)
            m_old = m_ref[hh, :, qsl]
            m_new = jnp.maximum(m_old, jnp.max(st, axis=0, keepdims=True))
            m_ref[hh, :, qsl] = m_new
            p = jnp.exp2(st - m_new).astype(bf16)
            acc_ref[hh, :, qsl] = (jnp.exp2(m_old - m_new) * acc_ref[hh, :, qsl]
                                   + _dot(vt_ref[hh, :, 0:n_keys(q_lo)], p))

        ahead = 4
        pending = [scores(*c) for c in chains[:ahead]]
        for n, chain in enumerate(chains):
            st = pending.pop(0)
            if n + ahead < len(chains):
                pending.append(scores(*chains[n + ahead]))
            consume(*chain, st)

    def run_block(t, d):
        slot = (slot0 + t) % 2
        wait(slot)
        fetch(g, block_id(step, t + 1, n_full, i), 1 - slot)
        update(d, slot)

    def full_block(t, carry):
        run_block(t, -1)
        return carry

    lax.fori_loop(0, n_full, full_block, 0)
    for d in range(ndiag - 1):
        run_block(n_full + d, d)

    last_slot = (slot0 + n_full + ndiag - 1) % 2
    wait(last_slot)

    @pl.when(step + 1 < pl.num_programs(0) * nq)
    def _():
        nstep = step + 1
        fetch(nstep // nq, block_id(nstep, 0, cnt_ref[nstep], nstep % nq), 1 - last_slot)

    update(ndiag - 1, last_slot)

    outs = []
    for hh in range(2):
        acc = acc_ref[hh]
        outs.append(acc[0:HEAD, :] / acc[HEAD:HEAD + 1, :])
    y = jnp.concatenate(outs, axis=0).T
    gate = fg_ref[...]
    o_ref[...] = (y * gate * _sigmoid(gate)).astype(o_ref.dtype)


def _attention_work_list(fstat, ostat, qk_bound, *, T, tb, tq, tk):
    nq, nk = T // tq, T // tk
    fmax_q = fstat[:, 0, :NHEADS].reshape(nq, tq // tb, NHEADS).max(axis=1)
    fmin_k = fstat[:, 1, :NHEADS].reshape(nk, tk // tb, NHEADS).min(axis=1)
    own_q = ostat[:, 0, ::HEAD].reshape(nq, tq // tb, NHEADS).min(axis=1) - 0.05 * qk_bound
    need = (fmax_q[:, None, :] - fmin_k[None, :, :] + qk_bound - own_q[:, None, :]) > -SKIP_LOG2
    need = jnp.transpose(need[:, :, 0::2] | need[:, :, 1::2], (2, 0, 1))
    before = np.arange(nk)[None, :] < (tq // tk) * np.arange(nq)[:, None]
    need = need & jnp.asarray(before)[None]
    count = jnp.sum(need, axis=2).astype(jnp.int32)
    ids = jnp.argsort(jnp.logical_not(need), axis=2, stable=True).astype(jnp.int32)
    return count.reshape(-1), ids.reshape(-1)


def _attention(qt, ka, vt, z, fstat, ostat, qk_bound, *, tb, tq, tk):
    T = ka.shape[1]
    nq = T // tq
    count, ids = _attention_work_list(fstat, ostat, qk_bound, T=T, tb=tb, tq=tq, tk=tk)
    fg_col = (Z_FX + 3 * WIDTH) // LANES
    grid_spec = pltpu.PrefetchScalarGridSpec(
        num_scalar_prefetch=2,
        grid=(NPAIRS, nq),
        in_specs=[
            pl.BlockSpec((2, LANES, tq), lambda g, i, cnt, blk: (g, 0, i)),
            pl.BlockSpec(memory_space=pl.ANY),
            pl.BlockSpec(memory_space=pl.ANY),
            pl.BlockSpec((tq, LANES), lambda g, i, cnt, blk: (i, fg_col + g)),
        ],
        out_specs=pl.BlockSpec((tq, LANES), lambda g, i, cnt, blk: (i, g)),
        scratch_shapes=[
            pltpu.VMEM((2, 1, tq), f32), pltpu.VMEM((2, VT_ROWS, tq), f32),
            pltpu.VMEM((2, 2, tk, LANES), bf16), pltpu.VMEM((2, 2, VT_ROWS, tk), bf16),
            pltpu.SemaphoreType.DMA((2, 2)), pltpu.SMEM((1,), jnp.int32),
        ],
    )
    return pl.pallas_call(
        functools.partial(_attn_kernel, tq=tq, tk=tk),
        grid_spec=grid_spec,
        out_shape=jax.ShapeDtypeStruct((T, WIDTH), bf16),
        compiler_params=pltpu.CompilerParams(
            dimension_semantics=("arbitrary", "arbitrary"), vmem_limit_bytes=VMEM_LIMIT),
        name="fox_attention",
    )(count, ids, qt, ka, vt, z)


P_MU_R, P_MU_K, P_MU_V, P_W0, P_A0, P_KK, P_KA, P_RK, P_LNW, P_LNB = range(10)


def _rwkv_kernel(zr_ref, zk_ref, zv_ref, zg_ref, zl_ref, par_ref, mul_ref, w2_ref, a2_ref,
                 o_ref, s_ref, prev_ref, prevl_ref, *, tb, npair):
    width = npair * LANES

    @pl.when(pl.program_id(1) == 0)
    def _():
        s_ref[...] = jnp.zeros_like(s_ref)
        prev_ref[...] = jnp.zeros_like(prev_ref)
        prevl_ref[...] = jnp.zeros_like(prevl_ref)

    par = par_ref[...]
    prow = lambda k: par[k:k + 1, :]

    def shift(x, mu, last_ref):
        row = lax.broadcasted_iota(jnp.int32, x.shape, 0)
        xp = pltpu.roll(x, 1, axis=0)
        xp = jnp.where(row == 0, last_ref[7:8, :], xp)
        last_ref[...] = x[tb - 8:tb, :]
        return x + (xp - x) * mu

    r = shift(zr_ref[...], prow(P_MU_R), prev_ref.at[0])
    k = shift(zk_ref[...], prow(P_MU_K), prev_ref.at[1])
    v = shift(zv_ref[...], prow(P_MU_V), prev_ref.at[2])
    sl = shift(zl_ref[...], mul_ref[0:1, :], prevl_ref)

    bi = lax.broadcasted_iota(jnp.int32, (LANES, LANES), 0)
    bj = lax.broadcasted_iota(jnp.int32, (LANES, LANES), 1)
    same_head = jnp.where((bi >> 6) == (bj >> 6), 1.0, 0.0).astype(bf16)

    def seg(x):
        return jnp.concatenate(
            [_dot(x[:, pr * LANES:(pr + 1) * LANES].astype(bf16), same_head) for pr in range(npair)], axis=1)

    u = prow(P_W0) + _dot(jnp.tanh(sl).astype(bf16), w2_ref[...])
    ld = -EXP_NEG_HALF * _sigmoid(u)
    av = _sigmoid(prow(P_A0) + _dot(sl.astype(bf16), a2_ref[...]))

    kk = k * prow(P_KK)
    kk = kk * lax.rsqrt(jnp.maximum(seg(kk * kk), 1e-24))
    k2 = k * (1.0 + (av - 1.0) * prow(P_KA))

    ti = lax.broadcasted_iota(jnp.int32, (tb, tb), 0)
    tj = lax.broadcasted_iota(jnp.int32, (tb, tb), 1)
    tri = jnp.where(((ti >> 6) == (tj >> 6)) & (tj <= ti), 1.0, 0.0).astype(bf16)
    ld_hi, ld_lo = _split2(ld)
    cum = _dot(tri, ld_hi) + _dot(tri, ld_lo)

    e_pos = jnp.exp(cum)
    e_neg = jnp.exp(-cum)
    a_t = -kk * jnp.exp(cum - ld)
    b_t = kk * av * e_neg
    k_t = k2 * e_neg
    r_t = r * e_pos

    ri = lax.broadcasted_iota(jnp.int32, (LANES, LANES), 0)
    rj = lax.broadcasted_iota(jnp.int32, (LANES, LANES), 1)
    strict_lower = rj < ri
    eye = jnp.where(ri == rj, 1.0, 0.0)
    qi = lax.broadcasted_iota(jnp.int32, (CHUNK, LANES), 0)
    qj = lax.broadcasted_iota(jnp.int32, (CHUNK, LANES), 1)
    incl_lower = (qj & (CHUNK - 1)) <= qi
    chunk_head0 = qj < HEAD

    def stack(x):
        return jnp.concatenate([jnp.where(chunk_head0, x, 0.0), jnp.where(chunk_head0, 0.0, x)], axis=0)

    def level_mask(b):
        sh = b.bit_length()
        return ((ri >> sh) == (rj >> sh)) & ((ri & (2 * b - 1)) >= b) & ((rj & (2 * b - 1)) < b)

    nchunk = tb // CHUNK
    units = [(pr, c) for pr in range(npair) for c in range(nchunk)]
    rows = lambda c: slice(c * CHUNK, (c + 1) * CHUNK)
    lanes = lambda pr: slice(pr * LANES, (pr + 1) * LANES)
    cut = lambda x, u: x[rows(u[1]), lanes(u[0])]
    cend = {u: cum[(u[1] + 1) * CHUNK - 1:(u[1] + 1) * CHUNK, lanes(u[0])] for u in units}

    am = {u: stack(cut(a_t, u)) for u in units}
    vm = {u: stack(cut(v, u)) for u in units}
    x = {u: _dot_nt(jnp.concatenate([am[u], cut(r_t, u)], axis=0).astype(bf16),
                    jnp.concatenate([stack(cut(b_t, u)), stack(cut(k_t, u))], axis=0).astype(bf16))
         for u in units}
    l_ab = {u: jnp.where(strict_lower, x[u][0:LANES, 0:LANES], 0.0) for u in units}
    l_ak = {u: jnp.where(strict_lower, x[u][0:LANES, LANES:], 0.0) for u in units}
    l_r = {u: jnp.where(jnp.concatenate([incl_lower, incl_lower], axis=1), x[u][LANES:, :], 0.0).astype(bf16)
           for u in units}

    inv = {u: eye + jnp.where(level_mask(1), l_ab[u], 0.0) for u in units}
    b = 2
    while b < CHUNK:
        mask = level_mask(b)
        y_ = {u: _dot(jnp.where(mask, l_ab[u], 0.0).astype(bf16), inv[u].astype(bf16)) for u in units}
        inv = {u: inv[u] + _dot(inv[u].astype(bf16), y_[u].astype(bf16)) for u in units}
        b *= 2

    lv = {u: _dot(l_ak[u].astype(bf16), vm[u].astype(bf16)) for u in units}
    tg = {u: _dot(inv[u].astype(bf16), jnp.concatenate([am[u], lv[u]], axis=1).astype(bf16)) for u in units}
    zeros = jnp.zeros((LANES, LANES), f32)
    ly = {u: _dot(l_r[u], jnp.concatenate(
              [tg[u], jnp.concatenate([zeros, vm[u]], axis=1)], axis=0).astype(bf16)) for u in units}
    bh = {u: stack(cut(kk, u) * cut(av, u) * jnp.exp(cend[u] - cut(cum, u))).astype(bf16) for u in units}
    kh = {u: stack(cut(k2, u) * jnp.exp(cend[u] - cut(cum, u))).astype(bf16) for u in units}
    ag = {u: _dot_tn(tg[u].astype(bf16), bh[u]) for u in units}
    vk = {u: _dot_tn(vm[u].astype(bf16), kh[u]) for u in units}
    q_eff = {u: (cut(r_t, u) + ly[u][:, 0:LANES]).astype(bf16) for u in units}
    a_eff = {u: ag[u][0:LANES, :].astype(bf16) for u in units}
    g_eff = {u: ag[u][LANES:, :] + vk[u] for u in units}

    ys = [[None] * npair for _ in range(nchunk)]
    state = [s_ref[pr] for pr in range(npair)]
    for c in range(nchunk):
        for pr in range(npair):
            u = (pr, c)
            s_bf = state[pr].astype(bf16)
            ys[c][pr] = _dot_nt(q_eff[u], s_bf) + ly[u][:, LANES:]
            state[pr] = state[pr] * jnp.exp(cend[u]) + _dot(s_bf, a_eff[u]) + g_eff[u]
    for pr in range(npair):
        s_ref[pr] = state[pr]

    y = jnp.concatenate([jnp.concatenate(yc, axis=1) for yc in ys], axis=0)
    mean = seg(y) * (1.0 / HEAD)
    d = y - mean
    var = seg(d * d) * (1.0 / HEAD)
    yn = d * lax.rsqrt(var + GN_EPS) * prow(P_LNW) + prow(P_LNB)
    bonus = seg(r * k2 * prow(P_RK)) * v
    g = zg_ref[...]
    o_ref[...] = ((yn + bonus) * g * _sigmoid(g)).astype(o_ref.dtype)


def _rwkv(z, par, mul, w2p, a2p, *, tb, npair):
    T = z.shape[0]
    width = npair * LANES
    nb = WIDTH // width
    zspec = lambda off: pl.BlockSpec((tb, width), lambda g, t: (t, off + g))
    return pl.pallas_call(
        functools.partial(_rwkv_kernel, tb=tb, npair=npair),
        grid=(nb, T // tb),
        in_specs=[
            zspec(0), zspec(nb), zspec(2 * nb), zspec(3 * nb),
            pl.BlockSpec((tb, LANES), lambda g, t: (t, Z_LORA // LANES)),
            pl.BlockSpec((16, width), lambda g, t: (0, g)),
            pl.BlockSpec((8, LANES), lambda g, t: (0, 0)),
            pl.BlockSpec((LANES, width), lambda g, t: (0, g)),
            pl.BlockSpec((LANES, width), lambda g, t: (0, g)),
        ],
        out_specs=pl.BlockSpec((tb, width), lambda g, t: (t, g)),
        out_shape=jax.ShapeDtypeStruct((T, WIDTH), bf16),
        scratch_shapes=[pltpu.VMEM((npair, LANES, LANES), f32), pltpu.VMEM((3, 8, width), f32),
                        pltpu.VMEM((8, LANES), f32)],
        compiler_params=pltpu.CompilerParams(
            dimension_semantics=("parallel", "arbitrary"), vmem_limit_bytes=VMEM_LIMIT),
        name="rwkv7",
    )(z, z, z, z, z, par, mul, w2p, a2p)


def _out_kernel(perm_ref, x_ref, yr_ref, yf_ref, p_ref, wo_ref, wg_ref, wp_ref, g1_ref, g2_ref, o_ref, wfx_ref):
    @pl.when(pl.program_id(0) == 0)
    def _():
        for s in range(NHEADS):
            src = pl.multiple_of(WIDTH + perm_ref[s] * HEAD, HEAD)
            wfx_ref[s * HEAD:(s + 1) * HEAD, :] = wo_ref[pl.ds(src, HEAD), :]

    def rms(t, g):
        return t * lax.rsqrt(jnp.mean(t * t, axis=-1, keepdims=True) + RMS_EPS) * g

    m = _dot(yr_ref[...], wo_ref[0:WIDTH, :]) + _dot(yf_ref[...], wfx_ref[...])
    x1 = x_ref[...] + rms(m, g1_ref[...])
    gate = _sigmoid(_dot(rms(x1, g2_ref[...]).astype(bf16), wg_ref[...]))
    o_ref[...] = x1 + gate * _dot(p_ref[...].astype(bf16), wp_ref[...])


def _out_proj(perm, x, yr, yf, p, wo, wg, wp, g1, g2, *, tm):
    T = x.shape[0]
    const = lambda shape: pl.BlockSpec(shape, lambda i, perm: (0, 0), pipeline_mode=pl.Buffered(1))
    rows = lambda width: pl.BlockSpec((tm, width), lambda i, perm: (i, 0))
    grid_spec = pltpu.PrefetchScalarGridSpec(
        num_scalar_prefetch=1,
        grid=(T // tm,),
        in_specs=[
            rows(D_MODEL), rows(WIDTH), rows(WIDTH), rows(D_PLE),
            const((D_MODEL, D_MODEL)), const((D_MODEL, D_MODEL)), const((D_PLE, D_MODEL)),
            const((1, D_MODEL)), const((1, D_MODEL)),
        ],
        out_specs=rows(D_MODEL),
        scratch_shapes=[pltpu.VMEM((WIDTH, D_MODEL), bf16)],
    )
    return pl.pallas_call(
        _out_kernel,
        grid_spec=grid_spec,
        out_shape=jax.ShapeDtypeStruct((T, D_MODEL), f32),
        compiler_params=pltpu.CompilerParams(
            dimension_semantics=("arbitrary",), vmem_limit_bytes=VMEM_LIMIT),
        name="out_proj",
    )(perm, x, yr, yf, p, wo, wg, wp, g1, g2)


def _layer(x, p, pre_g, w_in, mu_r, mu_k, mu_v, mu_w, mu_a, w0, w2, a0, a2, k_k, k_a, r_k, ln_w, ln_b,
           b_f, q_g, k_g, w_out, post_g, ple_g, w_gate, w_ple):
    T = x.shape[0]
    rw_end = 4 * WIDTH
    lora_end = rw_end + 2 * LORA
    fx_end = lora_end + 4 * WIDTH
    perm = jnp.argsort(b_f).astype(jnp.int32)
    fx_half = lora_end // HEAD
    src_half = jnp.concatenate([
        jnp.arange(rw_end // HEAD, dtype=jnp.int32),
        (fx_half + NHEADS * jnp.arange(4, dtype=jnp.int32)[:, None] + perm[None, :]).reshape(-1),
        jnp.arange(rw_end // HEAD, lora_end // HEAD, dtype=jnp.int32),
        jnp.zeros((2,), jnp.int32)])
    w_in_t = w_in.T
    pick = (jnp.pad(perm, (0, LANES - NHEADS), constant_values=-1)[:, None] == jnp.arange(NHEADS, dtype=jnp.int32)[None, :])
    w_ff_t = jnp.dot(pick.astype(f32), w_in_t[fx_end:], precision=lax.Precision.HIGHEST)
    w = _relayout_w_in(src_half, w_in_t, w_ff_t)
    b_f = jnp.take(b_f, perm)
    z = _in_proj(x, pre_g.reshape(1, D_MODEL), w, tm=min(1024, T), tn=1408)

    gq = jnp.tile(q_g, NHEADS).reshape(1, WIDTH)
    gk = jnp.tile(k_g, NHEADS).reshape(1, WIDTH)
    bfp = jnp.pad(b_f, (0, LANES - NHEADS)).reshape(1, LANES)
    tb = min(256, T)
    qt, ka, vt, fstat, ostat = _fox_prep(z, gq, gk, bfp, tb=tb)
    qk_bound = 1.02 * HEAD ** 0.5 * LOG2E * jnp.max(jnp.abs(q_g)) * jnp.max(jnp.abs(k_g))
    y_fx = _attention(qt, ka, vt, z, fstat, ostat, qk_bound, tb=tb, tq=min(1024, T), tk=min(512, T))

    par = jnp.stack([mu_r, mu_k, mu_v, w0, a0, k_k, k_a, r_k.reshape(WIDTH), ln_w, ln_b])
    par = jnp.pad(par, ((0, 16 - par.shape[0]), (0, 0)))
    mul = jnp.broadcast_to(jnp.concatenate([mu_w, mu_a]).reshape(1, LANES), (8, LANES))
    zeros = jnp.zeros((LORA, WIDTH), f32)
    w2p = jnp.concatenate([w2, zeros], axis=0).astype(bf16)
    a2p = jnp.concatenate([zeros, a2], axis=0).astype(bf16)
    y_rw = _rwkv(z, par, mul, w2p, a2p, tb=min(256, T), npair=8)

    return _out_proj(perm, x, y_rw, y_fx, p, w_out.astype(bf16), w_gate.astype(bf16), w_ple.astype(bf16),
                     post_g.reshape(1, D_MODEL), ple_g.reshape(1, D_MODEL), tm=min(512, T))


def kernel(x, p, pre_norm_g, w_in, rw_mu_r, rw_mu_k, rw_mu_v, rw_mu_w, rw_mu_a, rw_w0, rw_w2, rw_a0, rw_a2,
           rw_k_k, rw_k_a, rw_r_k, rw_ln_w, rw_ln_b, fx_b_f, fx_q_g, fx_k_g, w_out, post_norm_g, ple_norm_g,
           w_ple_gate, w_ple):
    B = x.shape[0]
    outs = []
    for b in range(B):
        xb = x[b]
        for i in range(p.shape[0]):
            xb = _layer(xb, p[i, b], pre_norm_g[i], w_in[i], rw_mu_r[i], rw_mu_k[i], rw_mu_v[i], rw_mu_w[i],
                        rw_mu_a[i], rw_w0[i], rw_w2[i], rw_a0[i], rw_a2[i], rw_k_k[i], rw_k_a[i], rw_r_k[i],
                        rw_ln_w[i], rw_ln_b[i], fx_b_f[i], fx_q_g[i], fx_k_g[i], w_out[i], post_norm_g[i],
                        ple_norm_g[i], w_ple_gate[i], w_ple[i])
        outs.append(xb)
    return jnp.stack(outs)
```

```python
import functools

import numpy as np
import jax
import jax.numpy as jnp
from jax import lax
from jax.experimental import pallas as pl
from jax.experimental.pallas import tpu as pltpu

f32 = jnp.float32
bf16 = jnp.bfloat16

D_MODEL = 2048
D_PLE = 256
WIDTH = 1024
HEAD = 64
NHEADS = 16
NPAIRS = NHEADS // 2
LORA = 64
LANES = 128
RMS_EPS = 1e-6
GN_EPS = 64e-5
CHUNK = 64
LOG2E = 1.4426950408889634
EXP_NEG_HALF = 0.6065306597126334
NEG_BIG = -1e30

Z_RW = 0
Z_FX = 4096
Z_LORA = 8192
Z_FF = 8320
Z_COLS = 8448
VT_ROWS = 80
Q_COL = 256
SKIP_LOG2 = 40.0

VMEM_LIMIT = 56 * 1024 * 1024


def _dot(a, b):
    return jnp.dot(a, b, preferred_element_type=f32)


def _dot_nt(a, b):
    return lax.dot_general(a, b, (((1,), (1,)), ((), ())), preferred_element_type=f32)


def _dot_tn(a, b):
    return lax.dot_general(a, b, (((0,), (0,)), ((), ())), preferred_element_type=f32)


def _split2(x):
    hi = x.astype(bf16)
    lo = (x - hi.astype(f32)).astype(bf16)
    return hi, lo


def _split3(x):
    hi = x.astype(bf16)
    r = x - hi.astype(f32)
    mid = r.astype(bf16)
    lo = (r - mid.astype(f32)).astype(bf16)
    return hi, mid, lo


def _dot_sel(x, sel):
    hi, mid, lo = _split3(x)
    return _dot(hi, sel) + _dot(mid, sel) + _dot(lo, sel)


def _sel_dot(sel, x):
    hi, mid, lo = _split3(x)
    return _dot(sel, hi) + _dot(sel, mid) + _dot(sel, lo)


def _dot_x3(a, b):
    ah, al = _split2(a)
    bh, bl = _split2(b)
    return _dot(ah, bh) + _dot(al, bh) + _dot(ah, bl)


def _sigmoid(x):
    return 1.0 / (1.0 + jnp.exp(-x))


RELAYOUT_SLABS = 4


def _relayout_kernel(src_ref, *refs):
    slab_refs, ff_ref, o_ref = refs[:RELAYOUT_SLABS], refs[RELAYOUT_SLABS], refs[RELAYOUT_SLABS + 1]
    slabs = [r[...] for r in slab_refs]
    is_last = pl.program_id(0) == pl.num_programs(0) - 1
    slabs[-2] = jnp.where(is_last, ff_ref[0:HEAD, :], slabs[-2])
    slabs[-1] = jnp.where(is_last, ff_ref[HEAD:, :], slabs[-1])
    o_ref[...] = jnp.concatenate(slabs, axis=0).T.astype(bf16)


def _relayout_w_in(src_half, w_in_t, w_ff_t):
    n = RELAYOUT_SLABS
    slab = lambda k: pl.BlockSpec((HEAD, D_MODEL), lambda c, src: (src[n * c + k], 0))
    grid_spec = pltpu.PrefetchScalarGridSpec(
        num_scalar_prefetch=1,
        grid=(Z_COLS // (n * HEAD),),
        in_specs=[slab(k) for k in range(n)] + [pl.BlockSpec((LANES, D_MODEL), lambda c, src: (0, 0))],
        out_specs=pl.BlockSpec((D_MODEL, n * HEAD), lambda c, src: (0, c)),
    )
    return pl.pallas_call(
        _relayout_kernel,
        grid_spec=grid_spec,
        out_shape=jax.ShapeDtypeStruct((D_MODEL, Z_COLS), bf16),
        compiler_params=pltpu.CompilerParams(
            dimension_semantics=("parallel",), vmem_limit_bytes=VMEM_LIMIT),
        name="w_in_relayout",
    )(src_half, *([w_in_t] * n), w_ff_t)


def _in_proj_kernel(x_ref, g_ref, w_ref, z_ref, h_ref):
    @pl.when(pl.program_id(1) == 0)
    def _():
        x = x_ref[...]
        ms = jnp.mean(x * x, axis=-1, keepdims=True)
        h_ref[...] = (x * lax.rsqrt(ms + RMS_EPS) * g_ref[...]).astype(bf16)

    z_ref[...] = _dot(h_ref[...], w_ref[...])


def _in_proj(x, g, w, *, tm, tn):
    T = x.shape[0]
    return pl.pallas_call(
        _in_proj_kernel,
        grid=(T // tm, Z_COLS // tn),
        in_specs=[
            pl.BlockSpec((tm, D_MODEL), lambda i, j: (i, 0)),
            pl.BlockSpec((1, D_MODEL), lambda i, j: (0, 0)),
            pl.BlockSpec((D_MODEL, tn), lambda i, j: (0, j)),
        ],
        out_specs=pl.BlockSpec((tm, tn), lambda i, j: (i, j)),
        out_shape=jax.ShapeDtypeStruct((T, Z_COLS), f32),
        scratch_shapes=[pltpu.VMEM((tm, D_MODEL), bf16)],
        compiler_params=pltpu.CompilerParams(
            dimension_semantics=("parallel", "arbitrary"), vmem_limit_bytes=VMEM_LIMIT),
        name="in_proj",
    )(x, g, w)


def _fox_prep_kernel(fq_ref, fk_ref, fv_ref, ff_ref, gq_ref, gk_ref, bf_ref,
                     qt_ref, ka_ref, vt_ref, fs_ref, os_ref, carry_ref, *, tb):
    @pl.when(pl.program_id(0) == 0)
    def _():
        carry_ref[...] = jnp.zeros_like(carry_ref)

    bi = lax.broadcasted_iota(jnp.int32, (LANES, LANES), 0)
    bj = lax.broadcasted_iota(jnp.int32, (LANES, LANES), 1)
    same_head = jnp.where((bi >> 6) == (bj >> 6), 1.0, 0.0).astype(bf16)

    def head_rms(x, gain):
        cols = []
        for c in range(WIDTH // LANES):
            xc = x[:, c * LANES:(c + 1) * LANES]
            ssq = _dot((xc * xc).astype(bf16), same_head)
            cols.append(xc * lax.rsqrt(ssq * (1.0 / HEAD) + RMS_EPS))
        return jnp.concatenate(cols, axis=1) * gain

    qn = head_rms(fq_ref[...], gq_ref[...]) * (HEAD ** -0.5 * LOG2E)
    kn = head_rms(fk_ref[...], gk_ref[...])

    xf = ff_ref[...] + bf_ref[...]
    logf = jnp.minimum(xf, 0.0) - jnp.log(1.0 + jnp.exp(-jnp.abs(xf)))
    ri = lax.broadcasted_iota(jnp.int32, (tb, tb), 0)
    rj = lax.broadcasted_iota(jnp.int32, (tb, tb), 1)
    tri = jnp.where(rj <= ri, 1.0, 0.0).astype(bf16)
    cum = _sel_dot(tri, logf) + carry_ref[0:1, :]
    carry_ref[...] = jnp.broadcast_to(cum[tb - 1:tb, :], carry_ref.shape)
    cum2 = cum * LOG2E
    c_hi, c_mid, c_lo = (t.astype(f32) for t in _split3(cum2))
    srow = lax.broadcasted_iota(jnp.int32, (8, LANES), 0)
    fs_ref[0] = jnp.where(srow == 0, jnp.max(cum2, axis=0, keepdims=True), jnp.min(cum2, axis=0, keepdims=True))
    own = qn * kn
    own = jnp.concatenate([_dot(own[:, c * LANES:(c + 1) * LANES].astype(bf16), same_head)
                           for c in range(WIDTH // LANES)], axis=1)
    os_ref[0] = jnp.broadcast_to(jnp.min(own, axis=0, keepdims=True), (8, WIDTH))

    lane = lax.broadcasted_iota(jnp.int32, (tb, LANES), 1)
    erow = lax.broadcasted_iota(jnp.int32, (HEAD, tb), 0)
    q_ones = jnp.where(erow < 6, 1.0, 0.0)
    qn_t, vt = qn.T, fv_ref[...].T
    hi_t, mid_t, lo_t = c_hi.T, c_mid.T, c_lo.T
    vpad = jnp.concatenate(
        [jnp.ones((1, tb), f32), jnp.zeros((VT_ROWS - HEAD - 1, tb), f32)], axis=0)
    for h in range(NHEADS):
        even = h % 2 == 0
        base = HEAD if even else 0
        q_bias = jnp.where(erow == 0, hi_t[h:h + 1, :], jnp.where(erow == 1, mid_t[h:h + 1, :],
                 jnp.where(erow == 2, lo_t[h:h + 1, :], q_ones)))
        q_rows = qn_t[h * HEAD:(h + 1) * HEAD, :]
        qt_ref[h] = jnp.concatenate([q_rows, q_bias] if even else [q_bias, q_rows], axis=0).astype(bf16)
        fh, fm, fl = c_hi[:, h:h + 1], c_mid[:, h:h + 1], c_lo[:, h:h + 1]
        k_ones = jnp.where((lane >= base) & (lane < base + 3), 1.0, 0.0)
        k_bias = jnp.where(lane == base + 3, -fh, jnp.where(lane == base + 4, -fm,
                 jnp.where(lane == base + 5, -fl, k_ones)))
        kc = kn[:, (h // 2) * LANES:(h // 2 + 1) * LANES]
        ka_ref[h] = jnp.where((lane < HEAD) == even, kc, k_bias).astype(bf16)
        vt_ref[h] = jnp.concatenate([vt[h * HEAD:(h + 1) * HEAD, :], vpad], axis=0).astype(bf16)


def _fox_prep(z, gq, gk, bfp, *, tb):
    T = z.shape[0]
    wb = Z_FX // WIDTH
    return pl.pallas_call(
        functools.partial(_fox_prep_kernel, tb=tb),
        grid=(T // tb,),
        in_specs=[
            pl.BlockSpec((tb, WIDTH), lambda i: (i, wb)),
            pl.BlockSpec((tb, WIDTH), lambda i: (i, wb + 1)),
            pl.BlockSpec((tb, WIDTH), lambda i: (i, wb + 2)),
            pl.BlockSpec((tb, LANES), lambda i: (i, Z_FF // LANES)),
            pl.BlockSpec((1, WIDTH), lambda i: (0, 0)),
            pl.BlockSpec((1, WIDTH), lambda i: (0, 0)),
            pl.BlockSpec((1, LANES), lambda i: (0, 0)),
        ],
        out_specs=[
            pl.BlockSpec((NHEADS, LANES, tb), lambda i: (0, 0, i)),
            pl.BlockSpec((NHEADS, tb, LANES), lambda i: (0, i, 0)),
            pl.BlockSpec((NHEADS, VT_ROWS, tb), lambda i: (0, 0, i)),
            pl.BlockSpec((1, 8, LANES), lambda i: (i, 0, 0)),
            pl.BlockSpec((1, 8, WIDTH), lambda i: (i, 0, 0)),
        ],
        out_shape=[
            jax.ShapeDtypeStruct((NHEADS, LANES, T), bf16),
            jax.ShapeDtypeStruct((NHEADS, T, LANES), bf16),
            jax.ShapeDtypeStruct((NHEADS, VT_ROWS, T), bf16),
            jax.ShapeDtypeStruct((T // tb, 8, LANES), f32),
            jax.ShapeDtypeStruct((T // tb, 8, WIDTH), f32),
        ],
        scratch_shapes=[pltpu.VMEM((8, LANES), f32)],
        compiler_params=pltpu.CompilerParams(
            dimension_semantics=("arbitrary",), vmem_limit_bytes=VMEM_LIMIT),
        name="fox_prep",
    )(z, z, z, z, gq, gk, bfp)


def _attn_kernel(cnt_ref, blk_ref, qt_ref, ka_hbm, vt_hbm, fg_ref, o_ref, m_ref, acc_ref, kbuf, vbuf, sem, par_ref,
                 *, tq, tk):
    g, i = pl.program_id(0), pl.program_id(1)
    nq = pl.num_programs(1)
    ndiag = tq // tk
    max_full = ndiag * nq
    step = g * nq + i
    n_full = cnt_ref[step]

    def block_id(step_, t, n_full_, i_):
        return jnp.where(t < n_full_, blk_ref[step_ * max_full + jnp.minimum(t, max_full - 1)], ndiag * i_ + t - n_full_)

    def copies(g_, j, slot):
        return (pltpu.make_async_copy(ka_hbm.at[pl.ds(2 * g_, 2), pl.ds(j * tk, tk), :], kbuf.at[slot], sem.at[0, slot]),
                pltpu.make_async_copy(vt_hbm.at[pl.ds(2 * g_, 2), :, pl.ds(j * tk, tk)], vbuf.at[slot], sem.at[1, slot]))

    def fetch(g_, j, slot):
        for c in copies(g_, j, slot):
            c.start()

    def wait(slot):
        for c in copies(g, 0, slot):
            c.wait()

    @pl.when(step == 0)
    def _():
        par_ref[0] = 0
        fetch(g, block_id(step, 0, n_full, i), 0)

    slot0 = par_ref[0]
    par_ref[0] = (slot0 + n_full + ndiag) % 2
    m_ref[...] = jnp.full(m_ref.shape, NEG_BIG, f32)
    acc_ref[...] = jnp.zeros_like(acc_ref)

    def update(d, slot):
        ka_ref, vt_ref = kbuf.at[slot], vbuf.at[slot]

        def n_keys(q_lo):
            return tk if d < 0 else max(0, min(tk, q_lo + Q_COL - d * tk))

        chains = [(hh, qs * Q_COL) for qs in range(tq // Q_COL) for hh in range(2) if n_keys(qs * Q_COL) > 0]

        def scores(hh, q_lo):
            nk = n_keys(q_lo)
            st = _dot(ka_ref[hh, 0:nk, :], qt_ref[hh, :, q_lo:q_lo + Q_COL])
            if d >= 0 and d * tk + nk - 1 > q_lo:
                kpos = d * tk + lax.broadcasted_iota(jnp.int32, st.shape, 0)
                qpos = q_lo + lax.broadcasted_iota(jnp.int32, st.shape, 1)
                st = jnp.where(kpos <= qpos, st, NEG_BIG)
            return st

        def consume(hh, q_lo, st):
            qsl = slice(q_lo, q_lo + Q_COL)
            m_old = m_ref[hh, :, qsl]
            m_new = jnp.maximum(m_old, jnp.max(st, axis=0, keepdims=True))
            m_ref[hh, :, qsl] = m_new
            p = jnp.exp2(st - m_new).astype(bf16)
            acc_ref[hh, :, qsl] = (jnp.exp2(m_old - m_new) * acc_ref[hh, :, qsl]
                                   + _dot(vt_ref[hh, :, 0:n_keys(q_lo)], p))

        ahead = 4
        pending = [scores(*c) for c in chains[:ahead]]
        for n, chain in enumerate(chains):
            st = pending.pop(0)
            if n + ahead < len(chains):
                pending.append(scores(*chains[n + ahead]))
            consume(*chain, st)

    def run_block(t, d):
        slot = (slot0 + t) % 2
        wait(slot)
        fetch(g, block_id(step, t + 1, n_full, i), 1 - slot)
        update(d, slot)

    def full_block(t, carry):
        run_block(t, -1)
        return carry

    lax.fori_loop(0, n_full, full_block, 0)
    for d in range(ndiag - 1):
        run_block(n_full + d, d)

    last_slot = (slot0 + n_full + ndiag - 1) % 2
    wait(last_slot)

    @pl.when(step + 1 < pl.num_programs(0) * nq)
    def _():
        nstep = step + 1
        fetch(nstep // nq, block_id(nstep, 0, cnt_ref[nstep], nstep % nq), 1 - last_slot)

    update(ndiag - 1, last_slot)

    outs = []
    for hh in range(2):
        acc = acc_ref[hh]
        outs.append(acc[0:HEAD, :] / acc[HEAD:HEAD + 1, :])
    y = jnp.concatenate(outs, axis=0).T
    gate = fg_ref[...]
    o_ref[...] = (y * gate * _sigmoid(gate)).astype(o_ref.dtype)


def _attention_work_list(fstat, ostat, qk_bound, *, T, tb, tq, tk):
    nq, nk = T // tq, T // tk
    fmax_q = fstat[:, 0, :NHEADS].reshape(nq, tq // tb, NHEADS).max(axis=1)
    fmin_k = fstat[:, 1, :NHEADS].reshape(nk, tk // tb, NHEADS).min(axis=1)
    own_q = ostat[:, 0, ::HEAD].reshape(nq, tq // tb, NHEADS).min(axis=1) - 0.05 * qk_bound
    need = (fmax_q[:, None, :] - fmin_k[None, :, :] + qk_bound - own_q[:, None, :]) > -SKIP_LOG2
    need = jnp.transpose(need[:, :, 0::2] | need[:, :, 1::2], (2, 0, 1))
    before = np.arange(nk)[None, :] < (tq // tk) * np.arange(nq)[:, None]
    need = need & jnp.asarray(before)[None]
    count = jnp.sum(need, axis=2).astype(jnp.int32)
    ids = jnp.argsort(jnp.logical_not(need), axis=2, stable=True).astype(jnp.int32)
    return count.reshape(-1), ids.reshape(-1)


def _attention(qt, ka, vt, z, fstat, ostat, qk_bound, *, tb, tq, tk):
    T = ka.shape[1]
    nq = T // tq
    count, ids = _attention_work_list(fstat, ostat, qk_bound, T=T, tb=tb, tq=tq, tk=tk)
    fg_col = (Z_FX + 3 * WIDTH) // LANES
    grid_spec = pltpu.PrefetchScalarGridSpec(
        num_scalar_prefetch=2,
        grid=(NPAIRS, nq),
        in_specs=[
            pl.BlockSpec((2, LANES, tq), lambda g, i, cnt, blk: (g, 0, i)),
            pl.BlockSpec(memory_space=pl.ANY),
            pl.BlockSpec(memory_space=pl.ANY),
            pl.BlockSpec((tq, LANES), lambda g, i, cnt, blk: (i, fg_col + g)),
        ],
        out_specs=pl.BlockSpec((tq, LANES), lambda g, i, cnt, blk: (i, g)),
        scratch_shapes=[
            pltpu.VMEM((2, 1, tq), f32), pltpu.VMEM((2, VT_ROWS, tq), f32),
            pltpu.VMEM((2, 2, tk, LANES), bf16), pltpu.VMEM((2, 2, VT_ROWS, tk), bf16),
            pltpu.SemaphoreType.DMA((2, 2)), pltpu.SMEM((1,), jnp.int32),
        ],
    )
    return pl.pallas_call(
        functools.partial(_attn_kernel, tq=tq, tk=tk),
        grid_spec=grid_spec,
        out_shape=jax.ShapeDtypeStruct((T, WIDTH), bf16),
        compiler_params=pltpu.CompilerParams(
            dimension_semantics=("arbitrary", "arbitrary"), vmem_limit_bytes=VMEM_LIMIT),
        name="fox_attention",
    )(count, ids, qt, ka, vt, z)


P_MU_R, P_MU_K, P_MU_V, P_W0, P_A0, P_KK, P_KA, P_RK, P_LNW, P_LNB = range(10)


def _rwkv_kernel(zr_ref, zk_ref, zv_ref, zg_ref, zl_ref, par_ref, mul_ref, w2_ref, a2_ref,
                 o_ref, s_ref, prev_ref, prevl_ref, *, tb, npair):
    width = npair * LANES

    @pl.when(pl.program_id(1) == 0)
    def _():
        s_ref[...] = jnp.zeros_like(s_ref)
        prev_ref[...] = jnp.zeros_like(prev_ref)
        prevl_ref[...] = jnp.zeros_like(prevl_ref)

    par = par_ref[...]
    prow = lambda k: par[k:k + 1, :]

    def shift(x, mu, last_ref):
        row = lax.broadcasted_iota(jnp.int32, x.shape, 0)
        xp = pltpu.roll(x, 1, axis=0)
        xp = jnp.where(row == 0, last_ref[7:8, :], xp)
        last_ref[...] = x[tb - 8:tb, :]
        return x + (xp - x) * mu

    r = shift(zr_ref[...], prow(P_MU_R), prev_ref.at[0])
    k = shift(zk_ref[...], prow(P_MU_K), prev_ref.at[1])
    v = shift(zv_ref[...], prow(P_MU_V), prev_ref.at[2])
    sl = shift(zl_ref[...], mul_ref[0:1, :], prevl_ref)

    bi = lax.broadcasted_iota(jnp.int32, (LANES, LANES), 0)
    bj = lax.broadcasted_iota(jnp.int32, (LANES, LANES), 1)
    same_head = jnp.where((bi >> 6) == (bj >> 6), 1.0, 0.0).astype(bf16)

    def seg(x):
        return jnp.concatenate(
            [_dot(x[:, pr * LANES:(pr + 1) * LANES].astype(bf16), same_head) for pr in range(npair)], axis=1)

    u = prow(P_W0) + _dot(jnp.tanh(sl).astype(bf16), w2_ref[...])
    ld = -EXP_NEG_HALF * _sigmoid(u)
    av = _sigmoid(prow(P_A0) + _dot(sl.astype(bf16), a2_ref[...]))

    kk = k * prow(P_KK)
    kk = kk * lax.rsqrt(jnp.maximum(seg(kk * kk), 1e-24))
    k2 = k * (1.0 + (av - 1.0) * prow(P_KA))

    ti = lax.broadcasted_iota(jnp.int32, (tb, tb), 0)
    tj = lax.broadcasted_iota(jnp.int32, (tb, tb), 1)
    tri = jnp.where(((ti >> 6) == (tj >> 6)) & (tj <= ti), 1.0, 0.0).astype(bf16)
    ld_hi, ld_lo = _split2(ld)
    cum = _dot(tri, ld_hi) + _dot(tri, ld_lo)

    e_pos = jnp.exp(cum)
    e_neg = jnp.exp(-cum)
    a_t = -kk * jnp.exp(cum - ld)
    b_t = kk * av * e_neg
    k_t = k2 * e_neg
    r_t = r * e_pos

    ri = lax.broadcasted_iota(jnp.int32, (LANES, LANES), 0)
    rj = lax.broadcasted_iota(jnp.int32, (LANES, LANES), 1)
    strict_lower = rj < ri
    eye = jnp.where(ri == rj, 1.0, 0.0)
    qi = lax.broadcasted_iota(jnp.int32, (CHUNK, LANES), 0)
    qj = lax.broadcasted_iota(jnp.int32, (CHUNK, LANES), 1)
    incl_lower = (qj & (CHUNK - 1)) <= qi
    chunk_head0 = qj < HEAD

    def stack(x):
        return jnp.concatenate([jnp.where(chunk_head0, x, 0.0), jnp.where(chunk_head0, 0.0, x)], axis=0)

    def level_mask(b):
        sh = b.bit_length()
        return ((ri >> sh) == (rj >> sh)) & ((ri & (2 * b - 1)) >= b) & ((rj & (2 * b - 1)) < b)

    nchunk = tb // CHUNK
    units = [(pr, c) for pr in range(npair) for c in range(nchunk)]
    rows = lambda c: slice(c * CHUNK, (c + 1) * CHUNK)
    lanes = lambda pr: slice(pr * LANES, (pr + 1) * LANES)
    cut = lambda x, u: x[rows(u[1]), lanes(u[0])]
    cend = {u: cum[(u[1] + 1) * CHUNK - 1:(u[1] + 1) * CHUNK, lanes(u[0])] for u in units}

    am = {u: stack(cut(a_t, u)) for u in units}
    vm = {u: stack(cut(v, u)) for u in units}
    x = {u: _dot_nt(jnp.concatenate([am[u], cut(r_t, u)], axis=0).astype(bf16),
                    jnp.concatenate([stack(cut(b_t, u)), stack(cut(k_t, u))], axis=0).astype(bf16))
         for u in units}
    l_ab = {u: jnp.where(strict_lower, x[u][0:LANES, 0:LANES], 0.0) for u in units}
    l_ak = {u: jnp.where(strict_lower, x[u][0:LANES, LANES:], 0.0) for u in units}
    l_r = {u: jnp.where(jnp.concatenate([incl_lower, incl_lower], axis=1), x[u][LANES:, :], 0.0).astype(bf16)
           for u in units}

    inv = {u: eye + jnp.where(level_mask(1), l_ab[u], 0.0) for u in units}
    b = 2
    while b < CHUNK:
        mask = level_mask(b)
        y_ = {u: _dot(jnp.where(mask, l_ab[u], 0.0).astype(bf16), inv[u].astype(bf16)) for u in units}
        inv = {u: inv[u] + _dot(inv[u].astype(bf16), y_[u].astype(bf16)) for u in units}
        b *= 2

    lv = {u: _dot(l_ak[u].astype(bf16), vm[u].astype(bf16)) for u in units}
    tg = {u: _dot(inv[u].astype(bf16), jnp.concatenate([am[u], lv[u]], axis=1).astype(bf16)) for u in units}
    zeros = jnp.zeros((LANES, LANES), f32)
    ly = {u: _dot(l_r[u], jnp.concatenate(
              [tg[u], jnp.concatenate([zeros, vm[u]], axis=1)], axis=0).astype(bf16)) for u in units}
    bh = {u: stack(cut(kk, u) * cut(av, u) * jnp.exp(cend[u] - cut(cum, u))).astype(bf16) for u in units}
    kh = {u: stack(cut(k2, u) * jnp.exp(cend[u] - cut(cum, u))).astype(bf16) for u in units}
    ag = {u: _dot_tn(tg[u].astype(bf16), bh[u]) for u in units}
    vk = {u: _dot_tn(vm[u].astype(bf16), kh[u]) for u in units}
    q_eff = {u: (cut(r_t, u) + ly[u][:, 0:LANES]).astype(bf16) for u in units}
    a_eff = {u: ag[u][0:LANES, :].astype(bf16) for u in units}
    g_eff = {u: ag[u][LANES:, :] + vk[u] for u in units}

    ys = [[None] * npair for _ in range(nchunk)]
    state = [s_ref[pr] for pr in range(npair)]
    for c in range(nchunk):
        for pr in range(npair):
            u = (pr, c)
            s_bf = state[pr].astype(bf16)
            ys[c][pr] = _dot_nt(q_eff[u], s_bf) + ly[u][:, LANES:]
            state[pr] = state[pr] * jnp.exp(cend[u]) + _dot(s_bf, a_eff[u]) + g_eff[u]
    for pr in range(npair):
        s_ref[pr] = state[pr]

    y = jnp.concatenate([jnp.concatenate(yc, axis=1) for yc in ys], axis=0)
    mean = seg(y) * (1.0 / HEAD)
    d = y - mean
    var = seg(d * d) * (1.0 / HEAD)
    yn = d * lax.rsqrt(var + GN_EPS) * prow(P_LNW) + prow(P_LNB)
    bonus = seg(r * k2 * prow(P_RK)) * v
    g = zg_ref[...]
    o_ref[...] = ((yn + bonus) * g * _sigmoid(g)).astype(o_ref.dtype)


def _rwkv(z, par, mul, w2p, a2p, *, tb, npair):
    T = z.shape[0]
    width = npair * LANES
    nb = WIDTH // width
    zspec = lambda off: pl.BlockSpec((tb, width), lambda g, t: (t, off + g))
    return pl.pallas_call(
        functools.partial(_rwkv_kernel, tb=tb, npair=npair),
        grid=(nb, T // tb),
        in_specs=[
            zspec(0), zspec(nb), zspec(2 * nb), zspec(3 * nb),
            pl.BlockSpec((tb, LANES), lambda g, t: (t, Z_LORA // LANES)),
            pl.BlockSpec((16, width), lambda g, t: (0, g)),
            pl.BlockSpec((8, LANES), lambda g, t: (0, 0)),
            pl.BlockSpec((LANES, width), lambda g, t: (0, g)),
            pl.BlockSpec((LANES, width), lambda g, t: (0, g)),
        ],
        out_specs=pl.BlockSpec((tb, width), lambda g, t: (t, g)),
        out_shape=jax.ShapeDtypeStruct((T, WIDTH), bf16),
        scratch_shapes=[pltpu.VMEM((npair, LANES, LANES), f32), pltpu.VMEM((3, 8, width), f32),
                        pltpu.VMEM((8, LANES), f32)],
        compiler_params=pltpu.CompilerParams(
            dimension_semantics=("parallel", "arbitrary"), vmem_limit_bytes=VMEM_LIMIT),
        name="rwkv7",
    )(z, z, z, z, z, par, mul, w2p, a2p)


def _out_kernel(perm_ref, x_ref, yr_ref, yf_ref, p_ref, wo_ref, wg_ref, wp_ref, g1_ref, g2_ref, o_ref, wfx_ref):
    @pl.when(pl.program_id(0) == 0)
    def _():
        for s in range(NHEADS):
            src = pl.multiple_of(WIDTH + perm_ref[s] * HEAD, HEAD)
            wfx_ref[s * HEAD:(s + 1) * HEAD, :] = wo_ref[pl.ds(src, HEAD), :]

    def rms(t, g):
        return t * lax.rsqrt(jnp.mean(t * t, axis=-1, keepdims=True) + RMS_EPS) * g

    m = _dot(yr_ref[...], wo_ref[0:WIDTH, :]) + _dot(yf_ref[...], wfx_ref[...])
    x1 = x_ref[...] + rms(m, g1_ref[...])
    gate = _sigmoid(_dot(rms(x1, g2_ref[...]).astype(bf16), wg_ref[...]))
    o_ref[...] = x1 + gate * _dot(p_ref[...].astype(bf16), wp_ref[...])


def _out_proj(perm, x, yr, yf, p, wo, wg, wp, g1, g2, *, tm):
    T = x.shape[0]
    const = lambda shape: pl.BlockSpec(shape, lambda i, perm: (0, 0), pipeline_mode=pl.Buffered(1))
    rows = lambda width: pl.BlockSpec((tm, width), lambda i, perm: (i, 0))
    grid_spec = pltpu.PrefetchScalarGridSpec(
        num_scalar_prefetch=1,
        grid=(T // tm,),
        in_specs=[
            rows(D_MODEL), rows(WIDTH), rows(WIDTH), rows(D_PLE),
            const((D_MODEL, D_MODEL)), const((D_MODEL, D_MODEL)), const((D_PLE, D_MODEL)),
            const((1, D_MODEL)), const((1, D_MODEL)),
        ],
        out_specs=rows(D_MODEL),
        scratch_shapes=[pltpu.VMEM((WIDTH, D_MODEL), bf16)],
    )
    return pl.pallas_call(
        _out_kernel,
        grid_spec=grid_spec,
        out_shape=jax.ShapeDtypeStruct((T, D_MODEL), f32),
        compiler_params=pltpu.CompilerParams(
            dimension_semantics=("arbitrary",), vmem_limit_bytes=VMEM_LIMIT),
        name="out_proj",
    )(perm, x, yr, yf, p, wo, wg, wp, g1, g2)


def _layer(x, p, pre_g, w_in, mu_r, mu_k, mu_v, mu_w, mu_a, w0, w2, a0, a2, k_k, k_a, r_k, ln_w, ln_b,
           b_f, q_g, k_g, w_out, post_g, ple_g, w_gate, w_ple):
    T = x.shape[0]
    rw_end = 4 * WIDTH
    lora_end = rw_end + 2 * LORA
    fx_end = lora_end + 4 * WIDTH
    perm = jnp.argsort(b_f).astype(jnp.int32)
    fx_half = lora_end // HEAD
    src_half = jnp.concatenate([
        jnp.arange(rw_end // HEAD, dtype=jnp.int32),
        (fx_half + NHEADS * jnp.arange(4, dtype=jnp.int32)[:, None] + perm[None, :]).reshape(-1),
        jnp.arange(rw_end // HEAD, lora_end // HEAD, dtype=jnp.int32),
        jnp.zeros((2,), jnp.int32)])
    w_in_t = w_in.T
    pick = (jnp.pad(perm, (0, LANES - NHEADS), constant_values=-1)[:, None] == jnp.arange(NHEADS, dtype=jnp.int32)[None, :])
    w_ff_t = jnp.dot(pick.astype(f32), w_in_t[fx_end:], precision=lax.Precision.HIGHEST)
    w = _relayout_w_in(src_half, w_in_t, w_ff_t)
    b_f = jnp.take(b_f, perm)
    z = _in_proj(x, pre_g.reshape(1, D_MODEL), w, tm=min(1024, T), tn=1408)

    gq = jnp.tile(q_g, NHEADS).reshape(1, WIDTH)
    gk = jnp.tile(k_g, NHEADS).reshape(1, WIDTH)
    bfp = jnp.pad(b_f, (0, LANES - NHEADS)).reshape(1, LANES)
    tb = min(512, T)
    qt, ka, vt, fstat, ostat = _fox_prep(z, gq, gk, bfp, tb=tb)
    qk_bound = 1.02 * HEAD ** 0.5 * LOG2E * jnp.max(jnp.abs(q_g)) * jnp.max(jnp.abs(k_g))
    y_fx = _attention(qt, ka, vt, z, fstat, ostat, qk_bound, tb=tb, tq=min(1024, T), tk=min(512, T))

    par = jnp.stack([mu_r, mu_k, mu_v, w0, a0, k_k, k_a, r_k.reshape(WIDTH), ln_w, ln_b])
    par = jnp.pad(par, ((0, 16 - par.shape[0]), (0, 0)))
    mul = jnp.broadcast_to(jnp.concatenate([mu_w, mu_a]).reshape(1, LANES), (8, LANES))
    zeros = jnp.zeros((LORA, WIDTH), f32)
    w2p = jnp.concatenate([w2, zeros], axis=0).astype(bf16)
    a2p = jnp.concatenate([zeros, a2], axis=0).astype(bf16)
    y_rw = _rwkv(z, par, mul, w2p, a2p, tb=min(256, T), npair=8)

    return _out_proj(perm, x, y_rw, y_fx, p, w_out.astype(bf16), w_gate.astype(bf16), w_ple.astype(bf16),
                     post_g.reshape(1, D_MODEL), ple_g.reshape(1, D_MODEL), tm=min(512, T))


def kernel(x, p, pre_norm_g, w_in, rw_mu_r, rw_mu_k, rw_mu_v, rw_mu_w, rw_mu_a, rw_w0, rw_w2, rw_a0, rw_a2,
           rw_k_k, rw_k_a, rw_r_k, rw_ln_w, rw_ln_b, fx_b_f, fx_q_g, fx_k_g, w_out, post_norm_g, ple_norm_g,
           w_ple_gate, w_ple):
    B = x.shape[0]
    outs = []
    for b in range(B):
        xb = x[b]
        for i in range(p.shape[0]):
            xb = _layer(xb, p[i, b], pre_norm_g[i], w_in[i], rw_mu_r[i], rw_mu_k[i], rw_mu_v[i], rw_mu_w[i],
                        rw_mu_a[i], rw_w0[i], rw_w2[i], rw_a0[i], rw_a2[i], rw_k_k[i], rw_k_a[i], rw_r_k[i],
                        rw_ln_w[i], rw_ln_b[i], fx_b_f[i], fx_q_g[i], fx_k_g[i], w_out[i], post_norm_g[i],
                        ple_norm_g[i], w_ple_gate[i], w_ple[i])
        outs.append(xb)
    return jnp.stack(outs)
```

```python
import functools

import numpy as np
import jax
import jax.numpy as jnp
from jax import lax
from jax.experimental import pallas as pl
from jax.experimental.pallas import tpu as pltpu

f32 = jnp.float32
bf16 = jnp.bfloat16

D_MODEL = 2048
D_PLE = 256
WIDTH = 1024
HEAD = 64
NHEADS = 16
NPAIRS = NHEADS // 2
LORA = 64
LANES = 128
RMS_EPS = 1e-6
GN_EPS = 64e-5
CHUNK = 64
LOG2E = 1.4426950408889634
EXP_NEG_HALF = 0.6065306597126334
NEG_BIG = -1e30

Z_RW = 0
Z_LORA = 4096
Z_FF = 4224
FX_Q, FX_K, FX_V, FX_G = 4352, 5376, 6400, 7424
Z_COLS = 8448
FX_BLOCKS = 3
VT_ROWS = 80
Q_COL = 256
SKIP_LOG2 = 40.0

VMEM_LIMIT = 56 * 1024 * 1024


def _dot(a, b):
    return jnp.dot(a, b, preferred_element_type=f32)


def _dot_nt(a, b):
    return lax.dot_general(a, b, (((1,), (1,)), ((), ())), preferred_element_type=f32)


def _dot_tn(a, b):
    return lax.dot_general(a, b, (((0,), (0,)), ((), ())), preferred_element_type=f32)


def _split2(x):
    hi = x.astype(bf16)
    lo = (x - hi.astype(f32)).astype(bf16)
    return hi, lo


def _split3(x):
    hi = x.astype(bf16)
    r = x - hi.astype(f32)
    mid = r.astype(bf16)
    lo = (r - mid.astype(f32)).astype(bf16)
    return hi, mid, lo


def _dot_sel(x, sel):
    hi, mid, lo = _split3(x)
    return _dot(hi, sel) + _dot(mid, sel) + _dot(lo, sel)


def _sel_dot(sel, x):
    hi, mid, lo = _split3(x)
    return _dot(sel, hi) + _dot(sel, mid) + _dot(sel, lo)


def _dot_x3(a, b):
    ah, al = _split2(a)
    bh, bl = _split2(b)
    return _dot(ah, bh) + _dot(al, bh) + _dot(ah, bl)


def _sigmoid(x):
    return 1.0 / (1.0 + jnp.exp(-x))


RELAYOUT_SLABS = 4


def _relayout_kernel(src_ref, *refs):
    slab_refs, ff_ref, o_ref = refs[:RELAYOUT_SLABS], refs[RELAYOUT_SLABS], refs[RELAYOUT_SLABS + 1]
    slabs = [r[...] for r in slab_refs]
    is_ff = pl.program_id(0) == Z_FF // (RELAYOUT_SLABS * HEAD)
    slabs[-2] = jnp.where(is_ff, ff_ref[0:HEAD, :], slabs[-2])
    slabs[-1] = jnp.where(is_ff, ff_ref[HEAD:, :], slabs[-1])
    o_ref[...] = jnp.concatenate(slabs, axis=0).T.astype(bf16)


def _relayout_w_in(src_half, w_in_t, w_ff_t):
    n = RELAYOUT_SLABS
    slab = lambda k: pl.BlockSpec((HEAD, D_MODEL), lambda c, src: (src[n * c + k], 0))
    grid_spec = pltpu.PrefetchScalarGridSpec(
        num_scalar_prefetch=1,
        grid=(Z_COLS // (n * HEAD),),
        in_specs=[slab(k) for k in range(n)] + [pl.BlockSpec((LANES, D_MODEL), lambda c, src: (0, 0))],
        out_specs=pl.BlockSpec((D_MODEL, n * HEAD), lambda c, src: (0, c)),
    )
    return pl.pallas_call(
        _relayout_kernel,
        grid_spec=grid_spec,
        out_shape=jax.ShapeDtypeStruct((D_MODEL, Z_COLS), bf16),
        compiler_params=pltpu.CompilerParams(
            dimension_semantics=("parallel",), vmem_limit_bytes=VMEM_LIMIT),
        name="w_in_relayout",
    )(src_half, *([w_in_t] * n), w_ff_t)


def _in_proj_kernel(x_ref, g_ref, w_ref, gq_ref, gk_ref, bf_ref,
                    z_ref, qt_ref, ka_ref, vt_ref, fs_ref, os_ref,
                    h_ref, zfx_ref, cum_ref, qn_ref, carry_ref, *, tm, tn):
    i, j = pl.program_id(0), pl.program_id(1)
    first_fx = pl.num_programs(1) - FX_BLOCKS

    @pl.when(j == 0)
    def _():
        x = x_ref[...]
        ms = jnp.mean(x * x, axis=-1, keepdims=True)
        h_ref[...] = (x * lax.rsqrt(ms + RMS_EPS) * g_ref[...]).astype(bf16)

    @pl.when((i == 0) & (j == 0))
    def _():
        carry_ref[...] = jnp.zeros_like(carry_ref)

    def project(keep):
        zt = _dot(h_ref[...], w_ref[...])
        z_ref[...] = zt
        if keep is not None:
            zfx_ref[:, keep * tn:(keep + 1) * tn] = zt

    bi = lax.broadcasted_iota(jnp.int32, (LANES, LANES), 0)
    bj = lax.broadcasted_iota(jnp.int32, (LANES, LANES), 1)
    same_head = jnp.where((bi >> 6) == (bj >> 6), 1.0, 0.0).astype(bf16)
    lq, lk, lv = FX_Q - Z_FF, FX_K - Z_FF, FX_V - Z_FF

    def head_rms(x, gain):
        cols = []
        for c in range(WIDTH // LANES):
            xc = x[:, c * LANES:(c + 1) * LANES]
            ssq = _dot((xc * xc).astype(bf16), same_head)
            cols.append(xc * lax.rsqrt(ssq * (1.0 / HEAD) + RMS_EPS))
        return jnp.concatenate(cols, axis=1) * gain

    def bias_and_queries():
        xf = zfx_ref[:, 0:LANES] + bf_ref[...]
        logf = jnp.minimum(xf, 0.0) - jnp.log(1.0 + jnp.exp(-jnp.abs(xf)))
        ri = lax.broadcasted_iota(jnp.int32, (tm, tm), 0)
        rj = lax.broadcasted_iota(jnp.int32, (tm, tm), 1)
        tri = jnp.where(rj <= ri, 1.0, 0.0).astype(bf16)
        cum = _sel_dot(tri, logf) + carry_ref[0:1, :]
        carry_ref[...] = jnp.broadcast_to(cum[tm - 1:tm, :], carry_ref.shape)
        cum2 = cum * LOG2E
        cum_ref[...] = cum2
        srow = lax.broadcasted_iota(jnp.int32, (8, LANES), 0)
        fs_ref[0] = jnp.where(srow == 0, jnp.max(cum2, axis=0, keepdims=True), jnp.min(cum2, axis=0, keepdims=True))
        qn = head_rms(zfx_ref[:, lq:lq + WIDTH], gq_ref[...]) * (HEAD ** -0.5 * LOG2E)
        qn_ref[...] = qn
        hi_t, mid_t, lo_t = (t.astype(f32).T for t in _split3(cum2))
        erow = lax.broadcasted_iota(jnp.int32, (HEAD, tm), 0)
        q_ones = jnp.where(erow < 6, 1.0, 0.0)
        qn_t = qn.T
        for h in range(NHEADS):
            q_bias = jnp.where(erow == 0, hi_t[h:h + 1, :], jnp.where(erow == 1, mid_t[h:h + 1, :],
                     jnp.where(erow == 2, lo_t[h:h + 1, :], q_ones)))
            q_rows = qn_t[h * HEAD:(h + 1) * HEAD, :]
            qt_ref[h] = jnp.concatenate([q_rows, q_bias] if h % 2 == 0 else [q_bias, q_rows], axis=0).astype(bf16)

    def keys():
        kn = head_rms(zfx_ref[:, lk:lk + WIDTH], gk_ref[...])
        own = qn_ref[...] * kn
        own = jnp.concatenate([_dot(own[:, c * LANES:(c + 1) * LANES].astype(bf16), same_head)
                               for c in range(WIDTH // LANES)], axis=1)
        os_ref[0] = jnp.broadcast_to(jnp.min(own, axis=0, keepdims=True), (8, WIDTH))
        c_hi, c_mid, c_lo = (t.astype(f32) for t in _split3(cum_ref[...]))
        lane = lax.broadcasted_iota(jnp.int32, (tm, LANES), 1)
        for h in range(NHEADS):
            even = h % 2 == 0
            base = HEAD if even else 0
            fh, fm, fl = c_hi[:, h:h + 1], c_mid[:, h:h + 1], c_lo[:, h:h + 1]
            k_ones = jnp.where((lane >= base) & (lane < base + 3), 1.0, 0.0)
            k_bias = jnp.where(lane == base + 3, -fh, jnp.where(lane == base + 4, -fm,
                     jnp.where(lane == base + 5, -fl, k_ones)))
            kc = kn[:, (h // 2) * LANES:(h // 2 + 1) * LANES]
            ka_ref[h] = jnp.where((lane < HEAD) == even, kc, k_bias).astype(bf16)

    def values():
        vt = zfx_ref[:, lv:lv + WIDTH].T
        vpad = jnp.concatenate(
            [jnp.ones((1, tm), f32), jnp.zeros((VT_ROWS - HEAD - 1, tm), f32)], axis=0)
        for h in range(NHEADS):
            vt_ref[h] = jnp.concatenate([vt[h * HEAD:(h + 1) * HEAD, :], vpad], axis=0).astype(bf16)

    @pl.when(j < first_fx)
    def _():
        project(None)

    @pl.when(j == first_fx)
    def _():
        project(0)

    @pl.when(j == first_fx + 1)
    def _():
        bias_and_queries()
        project(1)

    @pl.when(j == first_fx + 2)
    def _():
        keys()
        project(2)
        values()


def _in_proj(x, g, w, gq, gk, bfp, *, tm, tn):
    T = x.shape[0]
    nj = Z_COLS // tn
    assert FX_BLOCKS * tn == Z_COLS - Z_FF and FX_K + WIDTH <= Z_FF + 2 * tn and FX_Q + WIDTH <= Z_FF + tn
    const = lambda shape: pl.BlockSpec(shape, lambda i, j: (0, 0))
    return pl.pallas_call(
        functools.partial(_in_proj_kernel, tm=tm, tn=tn),
        grid=(T // tm, nj),
        in_specs=[
            pl.BlockSpec((tm, D_MODEL), lambda i, j: (i, 0)),
            const((1, D_MODEL)),
            pl.BlockSpec((D_MODEL, tn), lambda i, j: (0, j)),
            const((1, WIDTH)), const((1, WIDTH)), const((1, LANES)),
        ],
        out_specs=[
            pl.BlockSpec((tm, tn), lambda i, j: (i, j)),
            pl.BlockSpec((NHEADS, LANES, tm), lambda i, j: (0, 0, i)),
            pl.BlockSpec((NHEADS, tm, LANES), lambda i, j: (0, i, 0)),
            pl.BlockSpec((NHEADS, VT_ROWS, tm), lambda i, j: (0, 0, i)),
            pl.BlockSpec((1, 8, LANES), lambda i, j: (i, 0, 0)),
            pl.BlockSpec((1, 8, WIDTH), lambda i, j: (i, 0, 0)),
        ],
        out_shape=[
            jax.ShapeDtypeStruct((T, Z_COLS), f32),
            jax.ShapeDtypeStruct((NHEADS, LANES, T), bf16),
            jax.ShapeDtypeStruct((NHEADS, T, LANES), bf16),
            jax.ShapeDtypeStruct((NHEADS, VT_ROWS, T), bf16),
            jax.ShapeDtypeStruct((T // tm, 8, LANES), f32),
            jax.ShapeDtypeStruct((T // tm, 8, WIDTH), f32),
        ],
        scratch_shapes=[
            pltpu.VMEM((tm, D_MODEL), bf16), pltpu.VMEM((tm, FX_BLOCKS * tn), f32),
            pltpu.VMEM((tm, LANES), f32), pltpu.VMEM((tm, WIDTH), f32), pltpu.VMEM((8, LANES), f32),
        ],
        compiler_params=pltpu.CompilerParams(
            dimension_semantics=("arbitrary", "arbitrary"), vmem_limit_bytes=VMEM_LIMIT),
        name="in_proj",
    )(x, g, w, gq, gk, bfp)


def _attn_kernel(cnt_ref, blk_ref, qt_ref, ka_hbm, vt_hbm, fg_ref, o_ref, m_ref, acc_ref, kbuf, vbuf, sem, par_ref,
                 *, tq, tk):
    g, i = pl.program_id(0), pl.program_id(1)
    nq = pl.num_programs(1)
    ndiag = tq // tk
    max_full = ndiag * nq
    step = g * nq + i
    n_full = cnt_ref[step]

    def block_id(step_, t, n_full_, i_):
        return jnp.where(t < n_full_, blk_ref[step_ * max_full + jnp.minimum(t, max_full - 1)], ndiag * i_ + t - n_full_)

    def copies(g_, j, slot):
        return (pltpu.make_async_copy(ka_hbm.at[pl.ds(2 * g_, 2), pl.ds(j * tk, tk), :], kbuf.at[slot], sem.at[0, slot]),
                pltpu.make_async_copy(vt_hbm.at[pl.ds(2 * g_, 2), :, pl.ds(j * tk, tk)], vbuf.at[slot], sem.at[1, slot]))

    def fetch(g_, j, slot):
        for c in copies(g_, j, slot):
            c.start()

    def wait(slot):
        for c in copies(g, 0, slot):
            c.wait()

    @pl.when(step == 0)
    def _():
        par_ref[0] = 0
        fetch(g, block_id(step, 0, n_full, i), 0)

    slot0 = par_ref[0]
    par_ref[0] = (slot0 + n_full + ndiag) % 2
    m_ref[...] = jnp.full(m_ref.shape, NEG_BIG, f32)
    acc_ref[...] = jnp.zeros_like(acc_ref)

    def update(d, slot):
        ka_ref, vt_ref = kbuf.at[slot], vbuf.at[slot]

        def n_keys(q_lo):
            return tk if d < 0 else max(0, min(tk, q_lo + Q_COL - d * tk))

        chains = [(hh, qs * Q_COL) for qs in range(tq // Q_COL) for hh in range(2) if n_keys(qs * Q_COL) > 0]

        def scores(hh, q_lo):
            nk = n_keys(q_lo)
            st = _dot(ka_ref[hh, 0:nk, :], qt_ref[hh, :, q_lo:q_lo + Q_COL])
            if d >= 0 and d * tk + nk - 1 > q_lo:
                kpos = d * tk + lax.broadcasted_iota(jnp.int32, st.shape, 0)
                qpos = q_lo + lax.broadcasted_iota(jnp.int32, st.shape, 1)
                st = jnp.where(kpos <= qpos, st, NEG_BIG)
            return st

        def consume(hh, q_lo, st):
            qsl = slice(q_lo, q_lo + Q_COL)
            m_old = m_ref[hh, :, qsl]
            m_new = jnp.maximum(m_old, jnp.max(st, axis=0, keepdims=True))
            m_ref[hh, :, qsl] = m_new
            p = jnp.exp2(st - m_new).astype(bf16)
            acc_ref[hh, :, qsl] = (jnp.exp2(m_old - m_new) * acc_ref[hh, :, qsl]
                                   + _dot(vt_ref[hh, :, 0:n_keys(q_lo)], p))

        ahead = 4
        pending = [scores(*c) for c in chains[:ahead]]
        for n, chain in enumerate(chains):
            st = pending.pop(0)
            if n + ahead < len(chains):
                pending.append(scores(*chains[n + ahead]))
            consume(*chain, st)

    def run_block(t, d):
        slot = (slot0 + t) % 2
        wait(slot)
        fetch(g, block_id(step, t + 1, n_full, i), 1 - slot)
        update(d, slot)

    def full_block(t, carry):
        run_block(t, -1)
        return carry

    lax.fori_loop(0, n_full, full_block, 0)
    for d in range(ndiag - 1):
        run_block(n_full + d, d)

    last_slot = (slot0 + n_full + ndiag - 1) % 2
    wait(last_slot)

    @pl.when(step + 1 < pl.num_programs(0) * nq)
    def _():
        nstep = step + 1
        fetch(nstep // nq, block_id(nstep, 0, cnt_ref[nstep], nstep % nq), 1 - last_slot)

    update(ndiag - 1, last_slot)

    outs = []
    for hh in range(2):
        acc = acc_ref[hh]
        outs.append(acc[0:HEAD, :] / acc[HEAD:HEAD + 1, :])
    y = jnp.concatenate(outs, axis=0).T
    gate = fg_ref[...]
    o_ref[...] = (y * gate * _sigmoid(gate)).astype(o_ref.dtype)


def _attention_work_list(fstat, ostat, qk_bound, *, T, tb, tq, tk):
    nq, nk = T // tq, T // tk
    fmax_q = fstat[:, 0, :NHEADS].reshape(nq, tq // tb, NHEADS).max(axis=1)
    fmin_k = fstat[:, 1, :NHEADS].reshape(nk, tk // tb, NHEADS).min(axis=1)
    own_q = ostat[:, 0, ::HEAD].reshape(nq, tq // tb, NHEADS).min(axis=1) - 0.05 * qk_bound
    need = (fmax_q[:, None, :] - fmin_k[None, :, :] + qk_bound - own_q[:, None, :]) > -SKIP_LOG2
    need = jnp.transpose(need[:, :, 0::2] | need[:, :, 1::2], (2, 0, 1))
    before = np.arange(nk)[None, :] < (tq // tk) * np.arange(nq)[:, None]
    need = need & jnp.asarray(before)[None]
    count = jnp.sum(need, axis=2).astype(jnp.int32)
    ids = jnp.argsort(jnp.logical_not(need), axis=2, stable=True).astype(jnp.int32)
    return count.reshape(-1), ids.reshape(-1)


def _attention(qt, ka, vt, z, fstat, ostat, qk_bound, *, tb, tq, tk):
    T = ka.shape[1]
    nq = T // tq
    count, ids = _attention_work_list(fstat, ostat, qk_bound, T=T, tb=tb, tq=tq, tk=tk)
    fg_col = FX_G // LANES
    grid_spec = pltpu.PrefetchScalarGridSpec(
        num_scalar_prefetch=2,
        grid=(NPAIRS, nq),
        in_specs=[
            pl.BlockSpec((2, LANES, tq), lambda g, i, cnt, blk: (g, 0, i)),
            pl.BlockSpec(memory_space=pl.ANY),
            pl.BlockSpec(memory_space=pl.ANY),
            pl.BlockSpec((tq, LANES), lambda g, i, cnt, blk: (i, fg_col + g)),
        ],
        out_specs=pl.BlockSpec((tq, LANES), lambda g, i, cnt, blk: (i, g)),
        scratch_shapes=[
            pltpu.VMEM((2, 1, tq), f32), pltpu.VMEM((2, VT_ROWS, tq), f32),
            pltpu.VMEM((2, 2, tk, LANES), bf16), pltpu.VMEM((2, 2, VT_ROWS, tk), bf16),
            pltpu.SemaphoreType.DMA((2, 2)), pltpu.SMEM((1,), jnp.int32),
        ],
    )
    return pl.pallas_call(
        functools.partial(_attn_kernel, tq=tq, tk=tk),
        grid_spec=grid_spec,
        out_shape=jax.ShapeDtypeStruct((T, WIDTH), bf16),
        compiler_params=pltpu.CompilerParams(
            dimension_semantics=("arbitrary", "arbitrary"), vmem_limit_bytes=VMEM_LIMIT),
        name="fox_attention",
    )(count, ids, qt, ka, vt, z)


P_MU_R, P_MU_K, P_MU_V, P_W0, P_A0, P_KK, P_KA, P_RK, P_LNW, P_LNB = range(10)


def _rwkv_kernel(zr_ref, zk_ref, zv_ref, zg_ref, zl_ref, par_ref, mul_ref, w2_ref, a2_ref,
                 o_ref, s_ref, prev_ref, prevl_ref, *, tb, npair):
    width = npair * LANES

    @pl.when(pl.program_id(1) == 0)
    def _():
        s_ref[...] = jnp.zeros_like(s_ref)
        prev_ref[...] = jnp.zeros_like(prev_ref)
        prevl_ref[...] = jnp.zeros_like(prevl_ref)

    par = par_ref[...]
    prow = lambda k: par[k:k + 1, :]

    def shift(x, mu, last_ref):
        row = lax.broadcasted_iota(jnp.int32, x.shape, 0)
        xp = pltpu.roll(x, 1, axis=0)
        xp = jnp.where(row == 0, last_ref[7:8, :], xp)
        last_ref[...] = x[tb - 8:tb, :]
        return x + (xp - x) * mu

    r = shift(zr_ref[...], prow(P_MU_R), prev_ref.at[0])
    k = shift(zk_ref[...], prow(P_MU_K), prev_ref.at[1])
    v = shift(zv_ref[...], prow(P_MU_V), prev_ref.at[2])
    sl = shift(zl_ref[...], mul_ref[0:1, :], prevl_ref)

    bi = lax.broadcasted_iota(jnp.int32, (LANES, LANES), 0)
    bj = lax.broadcasted_iota(jnp.int32, (LANES, LANES), 1)
    same_head = jnp.where((bi >> 6) == (bj >> 6), 1.0, 0.0).astype(bf16)

    def seg(x):
        return jnp.concatenate(
            [_dot(x[:, pr * LANES:(pr + 1) * LANES].astype(bf16), same_head) for pr in range(npair)], axis=1)

    u = prow(P_W0) + _dot(jnp.tanh(sl).astype(bf16), w2_ref[...])
    ld = -EXP_NEG_HALF * _sigmoid(u)
    av = _sigmoid(prow(P_A0) + _dot(sl.astype(bf16), a2_ref[...]))

    kk = k * prow(P_KK)
    kk = kk * lax.rsqrt(jnp.maximum(seg(kk * kk), 1e-24))
    k2 = k * (1.0 + (av - 1.0) * prow(P_KA))

    ti = lax.broadcasted_iota(jnp.int32, (tb, tb), 0)
    tj = lax.broadcasted_iota(jnp.int32, (tb, tb), 1)
    tri = jnp.where(((ti >> 6) == (tj >> 6)) & (tj <= ti), 1.0, 0.0).astype(bf16)
    ld_hi, ld_lo = _split2(ld)
    cum = _dot(tri, ld_hi) + _dot(tri, ld_lo)

    e_pos = jnp.exp(cum)
    e_neg = jnp.exp(-cum)
    a_t = -kk * jnp.exp(cum - ld)
    b_t = kk * av * e_neg
    k_t = k2 * e_neg
    r_t = r * e_pos

    ri = lax.broadcasted_iota(jnp.int32, (LANES, LANES), 0)
    rj = lax.broadcasted_iota(jnp.int32, (LANES, LANES), 1)
    strict_lower = rj < ri
    eye = jnp.where(ri == rj, 1.0, 0.0)
    qi = lax.broadcasted_iota(jnp.int32, (CHUNK, LANES), 0)
    qj = lax.broadcasted_iota(jnp.int32, (CHUNK, LANES), 1)
    incl_lower = (qj & (CHUNK - 1)) <= qi
    chunk_head0 = qj < HEAD

    def stack(x):
        return jnp.concatenate([jnp.where(chunk_head0, x, 0.0), jnp.where(chunk_head0, 0.0, x)], axis=0)

    def level_mask(b):
        sh = b.bit_length()
        return ((ri >> sh) == (rj >> sh)) & ((ri & (2 * b - 1)) >= b) & ((rj & (2 * b - 1)) < b)

    nchunk = tb // CHUNK
    units = [(pr, c) for pr in range(npair) for c in range(nchunk)]
    rows = lambda c: slice(c * CHUNK, (c + 1) * CHUNK)
    lanes = lambda pr: slice(pr * LANES, (pr + 1) * LANES)
    cut = lambda x, u: x[rows(u[1]), lanes(u[0])]
    cend = {u: cum[(u[1] + 1) * CHUNK - 1:(u[1] + 1) * CHUNK, lanes(u[0])] for u in units}

    am = {u: stack(cut(a_t, u)) for u in units}
    vm = {u: stack(cut(v, u)) for u in units}
    x = {u: _dot_nt(jnp.concatenate([am[u], cut(r_t, u)], axis=0).astype(bf16),
                    jnp.concatenate([stack(cut(b_t, u)), stack(cut(k_t, u))], axis=0).astype(bf16))
         for u in units}
    l_ab = {u: jnp.where(strict_lower, x[u][0:LANES, 0:LANES], 0.0) for u in units}
    l_ak = {u: jnp.where(strict_lower, x[u][0:LANES, LANES:], 0.0) for u in units}
    l_r = {u: jnp.where(jnp.concatenate([incl_lower, incl_lower], axis=1), x[u][LANES:, :], 0.0).astype(bf16)
           for u in units}

    inv = {u: eye + jnp.where(level_mask(1), l_ab[u], 0.0) for u in units}
    b = 2
    while b < CHUNK:
        mask = level_mask(b)
        y_ = {u: _dot(jnp.where(mask, l_ab[u], 0.0).astype(bf16), inv[u].astype(bf16)) for u in units}
        inv = {u: inv[u] + _dot(inv[u].astype(bf16), y_[u].astype(bf16)) for u in units}
        b *= 2

    lv = {u: _dot(l_ak[u].astype(bf16), vm[u].astype(bf16)) for u in units}
    tg = {u: _dot(inv[u].astype(bf16), jnp.concatenate([am[u], lv[u]], axis=1).astype(bf16)) for u in units}
    zeros = jnp.zeros((LANES, LANES), f32)
    ly = {u: _dot(l_r[u], jnp.concatenate(
              [tg[u], jnp.concatenate([zeros, vm[u]], axis=1)], axis=0).astype(bf16)) for u in units}
    bh = {u: stack(cut(kk, u) * cut(av, u) * jnp.exp(cend[u] - cut(cum, u))).astype(bf16) for u in units}
    kh = {u: stack(cut(k2, u) * jnp.exp(cend[u] - cut(cum, u))).astype(bf16) for u in units}
    ag = {u: _dot_tn(tg[u].astype(bf16), bh[u]) for u in units}
    vk = {u: _dot_tn(vm[u].astype(bf16), kh[u]) for u in units}
    q_eff = {u: (cut(r_t, u) + ly[u][:, 0:LANES]).astype(bf16) for u in units}
    a_eff = {u: ag[u][0:LANES, :].astype(bf16) for u in units}
    g_eff = {u: ag[u][LANES:, :] + vk[u] for u in units}

    ys = [[None] * npair for _ in range(nchunk)]
    state = [s_ref[pr] for pr in range(npair)]
    for c in range(nchunk):
        for pr in range(npair):
            u = (pr, c)
            s_bf = state[pr].astype(bf16)
            ys[c][pr] = _dot_nt(q_eff[u], s_bf) + ly[u][:, LANES:]
            state[pr] = state[pr] * jnp.exp(cend[u]) + _dot(s_bf, a_eff[u]) + g_eff[u]
    for pr in range(npair):
        s_ref[pr] = state[pr]

    y = jnp.concatenate([jnp.concatenate(yc, axis=1) for yc in ys], axis=0)
    mean = seg(y) * (1.0 / HEAD)
    d = y - mean
    var = seg(d * d) * (1.0 / HEAD)
    yn = d * lax.rsqrt(var + GN_EPS) * prow(P_LNW) + prow(P_LNB)
    bonus = seg(r * k2 * prow(P_RK)) * v
    g = zg_ref[...]
    o_ref[...] = ((yn + bonus) * g * _sigmoid(g)).astype(o_ref.dtype)


def _rwkv(z, par, mul, w2p, a2p, *, tb, npair):
    T = z.shape[0]
    width = npair * LANES
    nb = WIDTH // width
    zspec = lambda off: pl.BlockSpec((tb, width), lambda g, t: (t, off + g))
    return pl.pallas_call(
        functools.partial(_rwkv_kernel, tb=tb, npair=npair),
        grid=(nb, T // tb),
        in_specs=[
            zspec(0), zspec(nb), zspec(2 * nb), zspec(3 * nb),
            pl.BlockSpec((tb, LANES), lambda g, t: (t, Z_LORA // LANES)),
            pl.BlockSpec((16, width), lambda g, t: (0, g)),
            pl.BlockSpec((8, LANES), lambda g, t: (0, 0)),
            pl.BlockSpec((LANES, width), lambda g, t: (0, g)),
            pl.BlockSpec((LANES, width), lambda g, t: (0, g)),
        ],
        out_specs=pl.BlockSpec((tb, width), lambda g, t: (t, g)),
        out_shape=jax.ShapeDtypeStruct((T, WIDTH), bf16),
        scratch_shapes=[pltpu.VMEM((npair, LANES, LANES), f32), pltpu.VMEM((3, 8, width), f32),
                        pltpu.VMEM((8, LANES), f32)],
        compiler_params=pltpu.CompilerParams(
            dimension_semantics=("parallel", "arbitrary"), vmem_limit_bytes=VMEM_LIMIT),
        name="rwkv7",
    )(z, z, z, z, z, par, mul, w2p, a2p)


def _out_kernel(perm_ref, x_ref, yr_ref, yf_ref, p_ref, wo_ref, wg_ref, wp_ref, g1_ref, g2_ref, o_ref, wfx_ref):
    @pl.when(pl.program_id(0) == 0)
    def _():
        for s in range(NHEADS):
            src = pl.multiple_of(WIDTH + perm_ref[s] * HEAD, HEAD)
            wfx_ref[s * HEAD:(s + 1) * HEAD, :] = wo_ref[pl.ds(src, HEAD), :]

    def rms(t, g):
        return t * lax.rsqrt(jnp.mean(t * t, axis=-1, keepdims=True) + RMS_EPS) * g

    m = _dot(yr_ref[...], wo_ref[0:WIDTH, :]) + _dot(yf_ref[...], wfx_ref[...])
    x1 = x_ref[...] + rms(m, g1_ref[...])
    gate = _sigmoid(_dot(rms(x1, g2_ref[...]).astype(bf16), wg_ref[...]))
    o_ref[...] = x1 + gate * _dot(p_ref[...].astype(bf16), wp_ref[...])


def _out_proj(perm, x, yr, yf, p, wo, wg, wp, g1, g2, *, tm):
    T = x.shape[0]
    const = lambda shape: pl.BlockSpec(shape, lambda i, perm: (0, 0), pipeline_mode=pl.Buffered(1))
    rows = lambda width: pl.BlockSpec((tm, width), lambda i, perm: (i, 0))
    grid_spec = pltpu.PrefetchScalarGridSpec(
        num_scalar_prefetch=1,
        grid=(T // tm,),
        in_specs=[
            rows(D_MODEL), rows(WIDTH), rows(WIDTH), rows(D_PLE),
            const((D_MODEL, D_MODEL)), const((D_MODEL, D_MODEL)), const((D_PLE, D_MODEL)),
            const((1, D_MODEL)), const((1, D_MODEL)),
        ],
        out_specs=rows(D_MODEL),
        scratch_shapes=[pltpu.VMEM((WIDTH, D_MODEL), bf16)],
    )
    return pl.pallas_call(
        _out_kernel,
        grid_spec=grid_spec,
        out_shape=jax.ShapeDtypeStruct((T, D_MODEL), f32),
        compiler_params=pltpu.CompilerParams(
            dimension_semantics=("arbitrary",), vmem_limit_bytes=VMEM_LIMIT),
        name="out_proj",
    )(perm, x, yr, yf, p, wo, wg, wp, g1, g2)


def _layer(x, p, pre_g, w_in, mu_r, mu_k, mu_v, mu_w, mu_a, w0, w2, a0, a2, k_k, k_a, r_k, ln_w, ln_b,
           b_f, q_g, k_g, w_out, post_g, ple_g, w_gate, w_ple):
    T = x.shape[0]
    rw_end = 4 * WIDTH
    lora_end = rw_end + 2 * LORA
    fx_end = lora_end + 4 * WIDTH
    perm = jnp.argsort(b_f).astype(jnp.int32)
    fx_half = lora_end // HEAD
    src_half = jnp.concatenate([
        jnp.arange(lora_end // HEAD, dtype=jnp.int32),
        jnp.zeros((2,), jnp.int32),
        (fx_half + NHEADS * jnp.arange(4, dtype=jnp.int32)[:, None] + perm[None, :]).reshape(-1)])
    w_in_t = w_in.T
    pick = (jnp.pad(perm, (0, LANES - NHEADS), constant_values=-1)[:, None] == jnp.arange(NHEADS, dtype=jnp.int32)[None, :])
    w_ff_t = jnp.dot(pick.astype(f32), w_in_t[fx_end:], precision=lax.Precision.HIGHEST)
    w = _relayout_w_in(src_half, w_in_t, w_ff_t)
    b_f = jnp.take(b_f, perm)
    gq = jnp.tile(q_g, NHEADS).reshape(1, WIDTH)
    gk = jnp.tile(k_g, NHEADS).reshape(1, WIDTH)
    bfp = jnp.pad(b_f, (0, LANES - NHEADS)).reshape(1, LANES)
    tb = min(512, T)
    z, qt, ka, vt, fstat, ostat = _in_proj(x, pre_g.reshape(1, D_MODEL), w, gq, gk, bfp, tm=tb, tn=1408)
    qk_bound = 1.02 * HEAD ** 0.5 * LOG2E * jnp.max(jnp.abs(q_g)) * jnp.max(jnp.abs(k_g))
    y_fx = _attention(qt, ka, vt, z, fstat, ostat, qk_bound, tb=tb, tq=min(1024, T), tk=min(512, T))

    par = jnp.stack([mu_r, mu_k, mu_v, w0, a0, k_k, k_a, r_k.reshape(WIDTH), ln_w, ln_b])
    par = jnp.pad(par, ((0, 16 - par.shape[0]), (0, 0)))
    mul = jnp.broadcast_to(jnp.concatenate([mu_w, mu_a]).reshape(1, LANES), (8, LANES))
    zeros = jnp.zeros((LORA, WIDTH), f32)
    w2p = jnp.concatenate([w2, zeros], axis=0).astype(bf16)
    a2p = jnp.concatenate([zeros, a2], axis=0).astype(bf16)
    y_rw = _rwkv(z, par, mul, w2p, a2p, tb=min(256, T), npair=8)

    return _out_proj(perm, x, y_rw, y_fx, p, w_out.astype(bf16), w_gate.astype(bf16), w_ple.astype(bf16),
                     post_g.reshape(1, D_MODEL), ple_g.reshape(1, D_MODEL), tm=min(512, T))


def kernel(x, p, pre_norm_g, w_in, rw_mu_r, rw_mu_k, rw_mu_v, rw_mu_w, rw_mu_a, rw_w0, rw_w2, rw_a0, rw_a2,
           rw_k_k, rw_k_a, rw_r_k, rw_ln_w, rw_ln_b, fx_b_f, fx_q_g, fx_k_g, w_out, post_norm_g, ple_norm_g,
           w_ple_gate, w_ple):
    B = x.shape[0]
    outs = []
    for b in range(B):
        xb = x[b]
        for i in range(p.shape[0]):
            xb = _layer(xb, p[i, b], pre_norm_g[i], w_in[i], rw_mu_r[i], rw_mu_k[i], rw_mu_v[i], rw_mu_w[i],
                        rw_mu_a[i], rw_w0[i], rw_w2[i], rw_a0[i], rw_a2[i], rw_k_k[i], rw_k_a[i], rw_r_k[i],
                        rw_ln_w[i], rw_ln_b[i], fx_b_f[i], fx_q_g[i], fx_k_g[i], w_out[i], post_norm_g[i],
                        ple_norm_g[i], w_ple_gate[i], w_ple[i])
        outs.append(xb)
    return jnp.stack(outs)
```

```python
import functools

import numpy as np
import jax
import jax.numpy as jnp
from jax import lax
from jax.experimental import pallas as pl
from jax.experimental.pallas import tpu as pltpu

f32 = jnp.float32
bf16 = jnp.bfloat16

D_MODEL = 2048
D_PLE = 256
WIDTH = 1024
HEAD = 64
NHEADS = 16
NPAIRS = NHEADS // 2
LORA = 64
LANES = 128
RMS_EPS = 1e-6
GN_EPS = 64e-5
CHUNK = 64
LOG2E = 1.4426950408889634
EXP_NEG_HALF = 0.6065306597126334
NEG_BIG = -1e30

Z_FX = 4096
Z_LORA = 8192
Z_FF = 8320
Z_COLS = 8448
VT_ROWS = 80
Q_COL = 256
SKIP_LOG2 = 40.0


VMEM_LIMIT = 56 * 1024 * 1024
PROJ_ROWS, PROJ_COLS = 1024, 1408
PREP_ROWS = 512
ATTN_Q, ATTN_K = 1024, 512
RWKV_ROWS, RWKV_PAIRS = 256, 8
OUT_ROWS = 512


def _dot(a, b):
    return jnp.dot(a, b, preferred_element_type=f32)


def _dot_nt(a, b):
    return lax.dot_general(a, b, (((1,), (1,)), ((), ())), preferred_element_type=f32)


def _dot_tn(a, b):
    return lax.dot_general(a, b, (((0,), (0,)), ((), ())), preferred_element_type=f32)


def _split2(x):
    hi = x.astype(bf16)
    lo = (x - hi.astype(f32)).astype(bf16)
    return hi, lo


def _split3(x):
    hi = x.astype(bf16)
    r = x - hi.astype(f32)
    mid = r.astype(bf16)
    lo = (r - mid.astype(f32)).astype(bf16)
    return hi, mid, lo


def _sel_dot(sel, x):
    hi, mid, lo = _split3(x)
    return _dot(sel, hi) + _dot(sel, mid) + _dot(sel, lo)


def _sigmoid(x):
    return 1.0 / (1.0 + jnp.exp(-x))


RELAYOUT_SLABS = 4


def _relayout_kernel(src_ref, *refs):
    slab_refs, ff_ref, o_ref = refs[:RELAYOUT_SLABS], refs[RELAYOUT_SLABS], refs[RELAYOUT_SLABS + 1]
    slabs = [r[...] for r in slab_refs]
    is_last = pl.program_id(0) == pl.num_programs(0) - 1
    slabs[-2] = jnp.where(is_last, ff_ref[0:HEAD, :], slabs[-2])
    slabs[-1] = jnp.where(is_last, ff_ref[HEAD:, :], slabs[-1])
    o_ref[...] = jnp.concatenate(slabs, axis=0).T.astype(bf16)


def _relayout_w_in(src_half, w_in_t, w_ff_t):
    n = RELAYOUT_SLABS
    slab = lambda k: pl.BlockSpec((HEAD, D_MODEL), lambda c, src: (src[n * c + k], 0))
    grid_spec = pltpu.PrefetchScalarGridSpec(
        num_scalar_prefetch=1,
        grid=(Z_COLS // (n * HEAD),),
        in_specs=[slab(k) for k in range(n)] + [pl.BlockSpec((LANES, D_MODEL), lambda c, src: (0, 0))],
        out_specs=pl.BlockSpec((D_MODEL, n * HEAD), lambda c, src: (0, c)),
    )
    return pl.pallas_call(
        _relayout_kernel,
        grid_spec=grid_spec,
        out_shape=jax.ShapeDtypeStruct((D_MODEL, Z_COLS), bf16),
        compiler_params=pltpu.CompilerParams(
            dimension_semantics=("parallel",), vmem_limit_bytes=VMEM_LIMIT),
        name="w_in_relayout",
    )(src_half, *([w_in_t] * n), w_ff_t)


def _in_proj_kernel(x_ref, g_ref, w_ref, z_ref, h_ref):
    @pl.when(pl.program_id(1) == 0)
    def _():
        x = x_ref[...]
        ms = jnp.mean(x * x, axis=-1, keepdims=True)
        h_ref[...] = (x * lax.rsqrt(ms + RMS_EPS) * g_ref[...]).astype(bf16)

    z_ref[...] = _dot(h_ref[...], w_ref[...])


def _in_proj(x, g, w, *, tm, tn):
    T = x.shape[0]
    return pl.pallas_call(
        _in_proj_kernel,
        grid=(T // tm, Z_COLS // tn),
        in_specs=[
            pl.BlockSpec((tm, D_MODEL), lambda i, j: (i, 0)),
            pl.BlockSpec((1, D_MODEL), lambda i, j: (0, 0)),
            pl.BlockSpec((D_MODEL, tn), lambda i, j: (0, j)),
        ],
        out_specs=pl.BlockSpec((tm, tn), lambda i, j: (i, j)),
        out_shape=jax.ShapeDtypeStruct((T, Z_COLS), f32),
        scratch_shapes=[pltpu.VMEM((tm, D_MODEL), bf16)],
        compiler_params=pltpu.CompilerParams(
            dimension_semantics=("parallel", "arbitrary"), vmem_limit_bytes=VMEM_LIMIT),
        name="in_proj",
    )(x, g, w)


def _fox_prep_kernel(fq_ref, fk_ref, fv_ref, ff_ref, gq_ref, gk_ref, bf_ref,
                     qt_ref, ka_ref, vt_ref, fs_ref, os_ref, carry_ref, *, tb):
    @pl.when(pl.program_id(0) == 0)
    def _():
        carry_ref[...] = jnp.zeros_like(carry_ref)

    bi = lax.broadcasted_iota(jnp.int32, (LANES, LANES), 0)
    bj = lax.broadcasted_iota(jnp.int32, (LANES, LANES), 1)
    same_head = jnp.where((bi >> 6) == (bj >> 6), 1.0, 0.0).astype(bf16)

    def head_rms(x, gain):
        cols = []
        for c in range(WIDTH // LANES):
            xc = x[:, c * LANES:(c + 1) * LANES]
            ssq = _dot((xc * xc).astype(bf16), same_head)
            cols.append(xc * lax.rsqrt(ssq * (1.0 / HEAD) + RMS_EPS))
        return jnp.concatenate(cols, axis=1) * gain

    qn = head_rms(fq_ref[...], gq_ref[...]) * (HEAD ** -0.5 * LOG2E)
    kn = head_rms(fk_ref[...], gk_ref[...])

    xf = ff_ref[...] + bf_ref[...]
    logf = jnp.minimum(xf, 0.0) - jnp.log(1.0 + jnp.exp(-jnp.abs(xf)))
    ri = lax.broadcasted_iota(jnp.int32, (tb, tb), 0)
    rj = lax.broadcasted_iota(jnp.int32, (tb, tb), 1)
    tri = jnp.where(rj <= ri, 1.0, 0.0).astype(bf16)
    cum = _sel_dot(tri, logf) + carry_ref[0:1, :]
    carry_ref[...] = jnp.broadcast_to(cum[tb - 1:tb, :], carry_ref.shape)
    cum2 = cum * LOG2E
    c_hi, c_mid, c_lo = (t.astype(f32) for t in _split3(cum2))
    srow = lax.broadcasted_iota(jnp.int32, (8, LANES), 0)
    fs_ref[0] = jnp.where(srow == 0, jnp.max(cum2, axis=0, keepdims=True), jnp.min(cum2, axis=0, keepdims=True))
    own = qn * kn
    own = jnp.concatenate([_dot(own[:, c * LANES:(c + 1) * LANES].astype(bf16), same_head)
                           for c in range(WIDTH // LANES)], axis=1)
    os_ref[0] = jnp.broadcast_to(jnp.min(own, axis=0, keepdims=True), (8, WIDTH))

    lane = lax.broadcasted_iota(jnp.int32, (tb, LANES), 1)
    erow = lax.broadcasted_iota(jnp.int32, (HEAD, tb), 0)
    q_ones = jnp.where(erow < 6, 1.0, 0.0)
    qn_t, vt = qn.T, fv_ref[...].T
    hi_t, mid_t, lo_t = c_hi.T, c_mid.T, c_lo.T
    vpad = jnp.concatenate(
        [jnp.ones((1, tb), f32), jnp.zeros((VT_ROWS - HEAD - 1, tb), f32)], axis=0)
    for h in range(NHEADS):
        even = h % 2 == 0
        base = HEAD if even else 0
        q_bias = jnp.where(erow == 0, hi_t[h:h + 1, :], jnp.where(erow == 1, mid_t[h:h + 1, :],
                 jnp.where(erow == 2, lo_t[h:h + 1, :], q_ones)))
        q_rows = qn_t[h * HEAD:(h + 1) * HEAD, :]
        qt_ref[h] = jnp.concatenate([q_rows, q_bias] if even else [q_bias, q_rows], axis=0).astype(bf16)
        fh, fm, fl = c_hi[:, h:h + 1], c_mid[:, h:h + 1], c_lo[:, h:h + 1]
        k_ones = jnp.where((lane >= base) & (lane < base + 3), 1.0, 0.0)
        k_bias = jnp.where(lane == base + 3, -fh, jnp.where(lane == base + 4, -fm,
                 jnp.where(lane == base + 5, -fl, k_ones)))
        kc = kn[:, (h // 2) * LANES:(h // 2 + 1) * LANES]
        ka_ref[h] = jnp.where((lane < HEAD) == even, kc, k_bias).astype(bf16)
        vt_ref[h] = jnp.concatenate([vt[h * HEAD:(h + 1) * HEAD, :], vpad], axis=0).astype(bf16)


def _fox_prep(z, gq, gk, bfp, *, tb):
    T = z.shape[0]
    wb = Z_FX // WIDTH
    return pl.pallas_call(
        functools.partial(_fox_prep_kernel, tb=tb),
        grid=(T // tb,),
        in_specs=[
            pl.BlockSpec((tb, WIDTH), lambda i: (i, wb)),
            pl.BlockSpec((tb, WIDTH), lambda i: (i, wb + 1)),
            pl.BlockSpec((tb, WIDTH), lambda i: (i, wb + 2)),
            pl.BlockSpec((tb, LANES), lambda i: (i, Z_FF // LANES)),
            pl.BlockSpec((1, WIDTH), lambda i: (0, 0)),
            pl.BlockSpec((1, WIDTH), lambda i: (0, 0)),
            pl.BlockSpec((1, LANES), lambda i: (0, 0)),
        ],
        out_specs=[
            pl.BlockSpec((NHEADS, LANES, tb), lambda i: (0, 0, i)),
            pl.BlockSpec((NHEADS, tb, LANES), lambda i: (0, i, 0)),
            pl.BlockSpec((NHEADS, VT_ROWS, tb), lambda i: (0, 0, i)),
            pl.BlockSpec((1, 8, LANES), lambda i: (i, 0, 0)),
            pl.BlockSpec((1, 8, WIDTH), lambda i: (i, 0, 0)),
        ],
        out_shape=[
            jax.ShapeDtypeStruct((NHEADS, LANES, T), bf16),
            jax.ShapeDtypeStruct((NHEADS, T, LANES), bf16),
            jax.ShapeDtypeStruct((NHEADS, VT_ROWS, T), bf16),
            jax.ShapeDtypeStruct((T // tb, 8, LANES), f32),
            jax.ShapeDtypeStruct((T // tb, 8, WIDTH), f32),
        ],
        scratch_shapes=[pltpu.VMEM((8, LANES), f32)],
        compiler_params=pltpu.CompilerParams(
            dimension_semantics=("arbitrary",), vmem_limit_bytes=VMEM_LIMIT),
        name="fox_prep",
    )(z, z, z, z, gq, gk, bfp)


def _attn_kernel(cnt_ref, blk_ref, qt_ref, ka_hbm, vt_hbm, fg_ref, o_ref, m_ref, acc_ref, kbuf, vbuf, sem, par_ref,
                 *, tq, tk):
    g, i = pl.program_id(0), pl.program_id(1)
    nq = pl.num_programs(1)
    ndiag = tq // tk
    max_full = ndiag * nq
    step = g * nq + i
    n_full = cnt_ref[step]

    def block_id(step_, t, n_full_, i_):
        return jnp.where(t < n_full_, blk_ref[step_ * max_full + jnp.minimum(t, max_full - 1)], ndiag * i_ + t - n_full_)

    def copies(g_, j, slot):
        return (pltpu.make_async_copy(ka_hbm.at[pl.ds(2 * g_, 2), pl.ds(j * tk, tk), :], kbuf.at[slot], sem.at[0, slot]),
                pltpu.make_async_copy(vt_hbm.at[pl.ds(2 * g_, 2), :, pl.ds(j * tk, tk)], vbuf.at[slot], sem.at[1, slot]))

    def fetch(g_, j, slot):
        for c in copies(g_, j, slot):
            c.start()

    def wait(slot):
        for c in copies(g, 0, slot):
            c.wait()

    @pl.when(step == 0)
    def _():
        par_ref[0] = 0
        fetch(g, block_id(step, 0, n_full, i), 0)

    slot0 = par_ref[0]
    par_ref[0] = (slot0 + n_full + ndiag) % 2
    m_ref[...] = jnp.full(m_ref.shape, NEG_BIG, f32)
    acc_ref[...] = jnp.zeros_like(acc_ref)

    def update(d, slot):
        ka_ref, vt_ref = kbuf.at[slot], vbuf.at[slot]

        def n_keys(q_lo):
            return tk if d < 0 else max(0, min(tk, q_lo + Q_COL - d * tk))

        chains = [(hh, qs * Q_COL) for qs in range(tq // Q_COL) for hh in range(2) if n_keys(qs * Q_COL) > 0]

        def scores(hh, q_lo):
            nk = n_keys(q_lo)
            st = _dot(ka_ref[hh, 0:nk, :], qt_ref[hh, :, q_lo:q_lo + Q_COL])
            if d >= 0 and d * tk + nk - 1 > q_lo:
                kpos = d * tk + lax.broadcasted_iota(jnp.int32, st.shape, 0)
                qpos = q_lo + lax.broadcasted_iota(jnp.int32, st.shape, 1)
                st = jnp.where(kpos <= qpos, st, NEG_BIG)
            return st

        def consume(hh, q_lo, st):
            qsl = slice(q_lo, q_lo + Q_COL)
            m_old = m_ref[hh, :, qsl]
            m_new = jnp.maximum(m_old, jnp.max(st, axis=0, keepdims=True))
            m_ref[hh, :, qsl] = m_new
            p = jnp.exp2(st - m_new).astype(bf16)
            acc_ref[hh, :, qsl] = (jnp.exp2(m_old - m_new) * acc_ref[hh, :, qsl]
                                   + _dot(vt_ref[hh, :, 0:n_keys(q_lo)], p))

        ahead = 4
        pending = [scores(*c) for c in chains[:ahead]]
        for n, chain in enumerate(chains):
            st = pending.pop(0)
            if n + ahead < len(chains):
                pending.append(scores(*chains[n + ahead]))
            consume(*chain, st)

    def run_block(t, d):
        slot = (slot0 + t) % 2
        wait(slot)
        fetch(g, block_id(step, t + 1, n_full, i), 1 - slot)
        update(d, slot)

    def full_block(t, carry):
        run_block(t, -1)
        return carry

    lax.fori_loop(0, n_full, full_block, 0)
    for d in range(ndiag - 1):
        run_block(n_full + d, d)

    last_slot = (slot0 + n_full + ndiag - 1) % 2
    wait(last_slot)

    @pl.when(step + 1 < pl.num_programs(0) * nq)
    def _():
        nstep = step + 1
        fetch(nstep // nq, block_id(nstep, 0, cnt_ref[nstep], nstep % nq), 1 - last_slot)

    update(ndiag - 1, last_slot)

    outs = []
    for hh in range(2):
        acc = acc_ref[hh]
        outs.append(acc[0:HEAD, :] / acc[HEAD:HEAD + 1, :])
    y = jnp.concatenate(outs, axis=0).T
    gate = fg_ref[...]
    o_ref[...] = (y * gate * _sigmoid(gate)).astype(o_ref.dtype)


def _attention_work_list(fstat, ostat, qk_bound, *, T, tb, tq, tk):
    nq, nk = T // tq, T // tk
    fmax_q = fstat[:, 0, :NHEADS].reshape(nq, tq // tb, NHEADS).max(axis=1)
    fmin_k = fstat[:, 1, :NHEADS].reshape(nk, tk // tb, NHEADS).min(axis=1)
    own_q = ostat[:, 0, ::HEAD].reshape(nq, tq // tb, NHEADS).min(axis=1) - 0.05 * qk_bound
    need = (fmax_q[:, None, :] - fmin_k[None, :, :] + qk_bound - own_q[:, None, :]) > -SKIP_LOG2
    need = jnp.transpose(need[:, :, 0::2] | need[:, :, 1::2], (2, 0, 1))
    before = np.arange(nk)[None, :] < (tq // tk) * np.arange(nq)[:, None]
    need = need & jnp.asarray(before)[None]
    count = jnp.sum(need, axis=2).astype(jnp.int32)
    ids = jnp.argsort(jnp.logical_not(need), axis=2, stable=True).astype(jnp.int32)
    return count.reshape(-1), ids.reshape(-1)


def _attention(qt, ka, vt, z, fstat, ostat, qk_bound, *, tb, tq, tk):
    T = ka.shape[1]
    nq = T // tq
    count, ids = _attention_work_list(fstat, ostat, qk_bound, T=T, tb=tb, tq=tq, tk=tk)
    fg_col = (Z_FX + 3 * WIDTH) // LANES
    grid_spec = pltpu.PrefetchScalarGridSpec(
        num_scalar_prefetch=2,
        grid=(NPAIRS, nq),
        in_specs=[
            pl.BlockSpec((2, LANES, tq), lambda g, i, cnt, blk: (g, 0, i)),
            pl.BlockSpec(memory_space=pl.ANY),
            pl.BlockSpec(memory_space=pl.ANY),
            pl.BlockSpec((tq, LANES), lambda g, i, cnt, blk: (i, fg_col + g)),
        ],
        out_specs=pl.BlockSpec((tq, LANES), lambda g, i, cnt, blk: (i, g)),
        scratch_shapes=[
            pltpu.VMEM((2, 1, tq), f32), pltpu.VMEM((2, VT_ROWS, tq), f32),
            pltpu.VMEM((2, 2, tk, LANES), bf16), pltpu.VMEM((2, 2, VT_ROWS, tk), bf16),
            pltpu.SemaphoreType.DMA((2, 2)), pltpu.SMEM((1,), jnp.int32),
        ],
    )
    return pl.pallas_call(
        functools.partial(_attn_kernel, tq=tq, tk=tk),
        grid_spec=grid_spec,
        out_shape=jax.ShapeDtypeStruct((T, WIDTH), bf16),
        compiler_params=pltpu.CompilerParams(
            dimension_semantics=("arbitrary", "arbitrary"), vmem_limit_bytes=VMEM_LIMIT),
        name="fox_attention",
    )(count, ids, qt, ka, vt, z)


P_MU_R, P_MU_K, P_MU_V, P_W0, P_A0, P_KK, P_KA, P_RK, P_LNW, P_LNB = range(10)


def _rwkv_kernel(zr_ref, zk_ref, zv_ref, zg_ref, zl_ref, par_ref, mul_ref, w2_ref, a2_ref,
                 o_ref, s_ref, prev_ref, prevl_ref, *, tb, npair):
    width = npair * LANES

    @pl.when(pl.program_id(1) == 0)
    def _():
        s_ref[...] = jnp.zeros_like(s_ref)
        prev_ref[...] = jnp.zeros_like(prev_ref)
        prevl_ref[...] = jnp.zeros_like(prevl_ref)

    par = par_ref[...]
    prow = lambda k: par[k:k + 1, :]

    def shift(x, mu, last_ref):
        row = lax.broadcasted_iota(jnp.int32, x.shape, 0)
        xp = pltpu.roll(x, 1, axis=0)
        xp = jnp.where(row == 0, last_ref[7:8, :], xp)
        last_ref[...] = x[tb - 8:tb, :]
        return x + (xp - x) * mu

    r = shift(zr_ref[...], prow(P_MU_R), prev_ref.at[0])
    k = shift(zk_ref[...], prow(P_MU_K), prev_ref.at[1])
    v = shift(zv_ref[...], prow(P_MU_V), prev_ref.at[2])
    sl = shift(zl_ref[...], mul_ref[0:1, :], prevl_ref)

    bi = lax.broadcasted_iota(jnp.int32, (LANES, LANES), 0)
    bj = lax.broadcasted_iota(jnp.int32, (LANES, LANES), 1)
    same_head = jnp.where((bi >> 6) == (bj >> 6), 1.0, 0.0).astype(bf16)

    def seg(x):
        return jnp.concatenate(
            [_dot(x[:, pr * LANES:(pr + 1) * LANES].astype(bf16), same_head) for pr in range(npair)], axis=1)

    u = prow(P_W0) + _dot(jnp.tanh(sl).astype(bf16), w2_ref[...])
    ld = -EXP_NEG_HALF * _sigmoid(u)
    av = _sigmoid(prow(P_A0) + _dot(sl.astype(bf16), a2_ref[...]))

    kk = k * prow(P_KK)
    kk = kk * lax.rsqrt(jnp.maximum(seg(kk * kk), 1e-24))
    k2 = k * (1.0 + (av - 1.0) * prow(P_KA))

    ti = lax.broadcasted_iota(jnp.int32, (tb, tb), 0)
    tj = lax.broadcasted_iota(jnp.int32, (tb, tb), 1)
    tri = jnp.where(((ti >> 6) == (tj >> 6)) & (tj <= ti), 1.0, 0.0).astype(bf16)
    ld_hi, ld_lo = _split2(ld)
    cum = _dot(tri, ld_hi) + _dot(tri, ld_lo)

    e_pos = jnp.exp(cum)
    e_neg = jnp.exp(-cum)
    a_t = -kk * jnp.exp(cum - ld)
    b_t = kk * av * e_neg
    k_t = k2 * e_neg
    r_t = r * e_pos

    ri = lax.broadcasted_iota(jnp.int32, (LANES, LANES), 0)
    rj = lax.broadcasted_iota(jnp.int32, (LANES, LANES), 1)
    strict_lower = rj < ri
    eye = jnp.where(ri == rj, 1.0, 0.0)
    qi = lax.broadcasted_iota(jnp.int32, (CHUNK, LANES), 0)
    qj = lax.broadcasted_iota(jnp.int32, (CHUNK, LANES), 1)
    incl_lower = (qj & (CHUNK - 1)) <= qi
    chunk_head0 = qj < HEAD

    def stack(x):
        return jnp.concatenate([jnp.where(chunk_head0, x, 0.0), jnp.where(chunk_head0, 0.0, x)], axis=0)

    def level_mask(b):
        sh = b.bit_length()
        return ((ri >> sh) == (rj >> sh)) & ((ri & (2 * b - 1)) >= b) & ((rj & (2 * b - 1)) < b)

    nchunk = tb // CHUNK
    units = [(pr, c) for pr in range(npair) for c in range(nchunk)]
    rows = lambda c: slice(c * CHUNK, (c + 1) * CHUNK)
    lanes = lambda pr: slice(pr * LANES, (pr + 1) * LANES)
    cut = lambda x, u: x[rows(u[1]), lanes(u[0])]
    cend = {u: cum[(u[1] + 1) * CHUNK - 1:(u[1] + 1) * CHUNK, lanes(u[0])] for u in units}

    am = {u: stack(cut(a_t, u)) for u in units}
    vm = {u: stack(cut(v, u)) for u in units}
    x = {u: _dot_nt(jnp.concatenate([am[u], cut(r_t, u)], axis=0).astype(bf16),
                    jnp.concatenate([stack(cut(b_t, u)), stack(cut(k_t, u))], axis=0).astype(bf16))
         for u in units}
    l_ab = {u: jnp.where(strict_lower, x[u][0:LANES, 0:LANES], 0.0) for u in units}
    l_ak = {u: jnp.where(strict_lower, x[u][0:LANES, LANES:], 0.0) for u in units}
    l_r = {u: jnp.where(jnp.concatenate([incl_lower, incl_lower], axis=1), x[u][LANES:, :], 0.0).astype(bf16)
           for u in units}

    inv = {u: eye + jnp.where(level_mask(1), l_ab[u], 0.0) for u in units}
    b = 2
    while b < CHUNK:
        mask = level_mask(b)
        y_ = {u: _dot(jnp.where(mask, l_ab[u], 0.0).astype(bf16), inv[u].astype(bf16)) for u in units}
        inv = {u: inv[u] + _dot(inv[u].astype(bf16), y_[u].astype(bf16)) for u in units}
        b *= 2

    lv = {u: _dot(l_ak[u].astype(bf16), vm[u].astype(bf16)) for u in units}
    tg = {u: _dot(inv[u].astype(bf16), jnp.concatenate([am[u], lv[u]], axis=1).astype(bf16)) for u in units}
    zeros = jnp.zeros((LANES, LANES), f32)
    ly = {u: _dot(l_r[u], jnp.concatenate(
              [tg[u], jnp.concatenate([zeros, vm[u]], axis=1)], axis=0).astype(bf16)) for u in units}
    bh = {u: stack(cut(kk, u) * cut(av, u) * jnp.exp(cend[u] - cut(cum, u))).astype(bf16) for u in units}
    kh = {u: stack(cut(k2, u) * jnp.exp(cend[u] - cut(cum, u))).astype(bf16) for u in units}
    ag = {u: _dot_tn(tg[u].astype(bf16), bh[u]) for u in units}
    vk = {u: _dot_tn(vm[u].astype(bf16), kh[u]) for u in units}
    q_eff = {u: (cut(r_t, u) + ly[u][:, 0:LANES]).astype(bf16) for u in units}
    a_eff = {u: ag[u][0:LANES, :].astype(bf16) for u in units}
    g_eff = {u: ag[u][LANES:, :] + vk[u] for u in units}

    ys = [[None] * npair for _ in range(nchunk)]
    state = [s_ref[pr] for pr in range(npair)]
    for c in range(nchunk):
        for pr in range(npair):
            u = (pr, c)
            s_bf = state[pr].astype(bf16)
            ys[c][pr] = _dot_nt(q_eff[u], s_bf) + ly[u][:, LANES:]
            state[pr] = state[pr] * jnp.exp(cend[u]) + _dot(s_bf, a_eff[u]) + g_eff[u]
    for pr in range(npair):
        s_ref[pr] = state[pr]

    y = jnp.concatenate([jnp.concatenate(yc, axis=1) for yc in ys], axis=0)
    mean = seg(y) * (1.0 / HEAD)
    d = y - mean
    var = seg(d * d) * (1.0 / HEAD)
    yn = d * lax.rsqrt(var + GN_EPS) * prow(P_LNW) + prow(P_LNB)
    bonus = seg(r * k2 * prow(P_RK)) * v
    g = zg_ref[...]
    o_ref[...] = ((yn + bonus) * g * _sigmoid(g)).astype(o_ref.dtype)


def _rwkv(z, par, mul, w2p, a2p, *, tb, npair):
    T = z.shape[0]
    width = npair * LANES
    nb = WIDTH // width
    zspec = lambda off: pl.BlockSpec((tb, width), lambda g, t: (t, off + g))
    return pl.pallas_call(
        functools.partial(_rwkv_kernel, tb=tb, npair=npair),
        grid=(nb, T // tb),
        in_specs=[
            zspec(0), zspec(nb), zspec(2 * nb), zspec(3 * nb),
            pl.BlockSpec((tb, LANES), lambda g, t: (t, Z_LORA // LANES)),
            pl.BlockSpec((16, width), lambda g, t: (0, g)),
            pl.BlockSpec((8, LANES), lambda g, t: (0, 0)),
            pl.BlockSpec((LANES, width), lambda g, t: (0, g)),
            pl.BlockSpec((LANES, width), lambda g, t: (0, g)),
        ],
        out_specs=pl.BlockSpec((tb, width), lambda g, t: (t, g)),
        out_shape=jax.ShapeDtypeStruct((T, WIDTH), bf16),
        scratch_shapes=[pltpu.VMEM((npair, LANES, LANES), f32), pltpu.VMEM((3, 8, width), f32),
                        pltpu.VMEM((8, LANES), f32)],
        compiler_params=pltpu.CompilerParams(
            dimension_semantics=("parallel", "arbitrary"), vmem_limit_bytes=VMEM_LIMIT),
        name="rwkv7",
    )(z, z, z, z, z, par, mul, w2p, a2p)


def _out_kernel(perm_ref, x_ref, yr_ref, yf_ref, p_ref, wo_ref, wg_ref, wp_ref, g1_ref, g2_ref, o_ref, wfx_ref):
    @pl.when(pl.program_id(0) == 0)
    def _():
        for s in range(NHEADS):
            src = pl.multiple_of(WIDTH + perm_ref[s] * HEAD, HEAD)
            wfx_ref[s * HEAD:(s + 1) * HEAD, :] = wo_ref[pl.ds(src, HEAD), :]

    def rms(t, g):
        return t * lax.rsqrt(jnp.mean(t * t, axis=-1, keepdims=True) + RMS_EPS) * g

    m = _dot(yr_ref[...], wo_ref[0:WIDTH, :]) + _dot(yf_ref[...], wfx_ref[...])
    x1 = x_ref[...] + rms(m, g1_ref[...])
    gate = _sigmoid(_dot(rms(x1, g2_ref[...]).astype(bf16), wg_ref[...]))
    o_ref[...] = x1 + gate * _dot(p_ref[...].astype(bf16), wp_ref[...])


def _out_proj(perm, x, yr, yf, p, wo, wg, wp, g1, g2, *, tm):
    T = x.shape[0]
    const = lambda shape: pl.BlockSpec(shape, lambda i, perm: (0, 0), pipeline_mode=pl.Buffered(1))
    rows = lambda width: pl.BlockSpec((tm, width), lambda i, perm: (i, 0))
    grid_spec = pltpu.PrefetchScalarGridSpec(
        num_scalar_prefetch=1,
        grid=(T // tm,),
        in_specs=[
            rows(D_MODEL), rows(WIDTH), rows(WIDTH), rows(D_PLE),
            const((D_MODEL, D_MODEL)), const((D_MODEL, D_MODEL)), const((D_PLE, D_MODEL)),
            const((1, D_MODEL)), const((1, D_MODEL)),
        ],
        out_specs=rows(D_MODEL),
        scratch_shapes=[pltpu.VMEM((WIDTH, D_MODEL), bf16)],
    )
    return pl.pallas_call(
        _out_kernel,
        grid_spec=grid_spec,
        out_shape=jax.ShapeDtypeStruct((T, D_MODEL), f32),
        compiler_params=pltpu.CompilerParams(
            dimension_semantics=("arbitrary",), vmem_limit_bytes=VMEM_LIMIT),
        name="out_proj",
    )(perm, x, yr, yf, p, wo, wg, wp, g1, g2)


def _layer(x, p, pre_g, w_in, mu_r, mu_k, mu_v, mu_w, mu_a, w0, w2, a0, a2, k_k, k_a, r_k, ln_w, ln_b,
           b_f, q_g, k_g, w_out, post_g, ple_g, w_gate, w_ple):
    T = x.shape[0]
    rw_end = 4 * WIDTH
    lora_end = rw_end + 2 * LORA
    fx_end = lora_end + 4 * WIDTH
    assert x.shape == (T, D_MODEL) and w_in.shape == (D_MODEL, fx_end + NHEADS), (x.shape, w_in.shape)
    assert all(T % min(t, T) == 0 for t in (PROJ_ROWS, PREP_ROWS, ATTN_Q, ATTN_K, RWKV_ROWS, OUT_ROWS)), T
    perm = jnp.argsort(b_f).astype(jnp.int32)
    fx_half = lora_end // HEAD
    src_half = jnp.concatenate([
        jnp.arange(rw_end // HEAD, dtype=jnp.int32),
        (fx_half + NHEADS * jnp.arange(4, dtype=jnp.int32)[:, None] + perm[None, :]).reshape(-1),
        jnp.arange(rw_end // HEAD, lora_end // HEAD, dtype=jnp.int32),
        jnp.zeros((2,), jnp.int32)])
    w_in_t = w_in.T
    pick = (jnp.pad(perm, (0, LANES - NHEADS), constant_values=-1)[:, None] == jnp.arange(NHEADS, dtype=jnp.int32)[None, :])
    w_ff_t = jnp.dot(pick.astype(f32), w_in_t[fx_end:], precision=lax.Precision.HIGHEST)
    w = _relayout_w_in(src_half, w_in_t, w_ff_t)
    b_f = jnp.take(b_f, perm)
    z = _in_proj(x, pre_g.reshape(1, D_MODEL), w, tm=min(PROJ_ROWS, T), tn=PROJ_COLS)

    gq = jnp.tile(q_g, NHEADS).reshape(1, WIDTH)
    gk = jnp.tile(k_g, NHEADS).reshape(1, WIDTH)
    bfp = jnp.pad(b_f, (0, LANES - NHEADS)).reshape(1, LANES)
    tb = min(PREP_ROWS, T)
    qt, ka, vt, fstat, ostat = _fox_prep(z, gq, gk, bfp, tb=tb)
    qk_bound = 1.02 * HEAD ** 0.5 * LOG2E * jnp.max(jnp.abs(q_g)) * jnp.max(jnp.abs(k_g))
    y_fx = _attention(qt, ka, vt, z, fstat, ostat, qk_bound, tb=tb, tq=min(ATTN_Q, T), tk=min(ATTN_K, T))

    par = jnp.stack([mu_r, mu_k, mu_v, w0, a0, k_k, k_a, r_k.reshape(WIDTH), ln_w, ln_b])
    par = jnp.pad(par, ((0, 16 - par.shape[0]), (0, 0)))
    mul = jnp.broadcast_to(jnp.concatenate([mu_w, mu_a]).reshape(1, LANES), (8, LANES))
    zeros = jnp.zeros((LORA, WIDTH), f32)
    w2p = jnp.concatenate([w2, zeros], axis=0).astype(bf16)
    a2p = jnp.concatenate([zeros, a2], axis=0).astype(bf16)
    y_rw = _rwkv(z, par, mul, w2p, a2p, tb=min(RWKV_ROWS, T), npair=RWKV_PAIRS)

    return _out_proj(perm, x, y_rw, y_fx, p, w_out.astype(bf16), w_gate.astype(bf16), w_ple.astype(bf16),
                     post_g.reshape(1, D_MODEL), ple_g.reshape(1, D_MODEL), tm=min(OUT_ROWS, T))


def kernel(x, p, pre_norm_g, w_in, rw_mu_r, rw_mu_k, rw_mu_v, rw_mu_w, rw_mu_a, rw_w0, rw_w2, rw_a0, rw_a2,
           rw_k_k, rw_k_a, rw_r_k, rw_ln_w, rw_ln_b, fx_b_f, fx_q_g, fx_k_g, w_out, post_norm_g, ple_norm_g,
           w_ple_gate, w_ple):
    B = x.shape[0]
    outs = []
    for b in range(B):
        xb = x[b]
        for i in range(p.shape[0]):
            xb = _layer(xb, p[i, b], pre_norm_g[i], w_in[i], rw_mu_r[i], rw_mu_k[i], rw_mu_v[i], rw_mu_w[i],
                        rw_mu_a[i], rw_w0[i], rw_w2[i], rw_a0[i], rw_a2[i], rw_k_k[i], rw_k_a[i], rw_r_k[i],
                        rw_ln_w[i], rw_ln_b[i], fx_b_f[i], fx_q_g[i], fx_k_g[i], w_out[i], post_norm_g[i],
                        ple_norm_g[i], w_ple_gate[i], w_ple[i])
        outs.append(xb)
    return jnp.stack(outs)
```

```python
import functools

import numpy as np
import jax
import jax.numpy as jnp
from jax import lax
from jax.experimental import pallas as pl
from jax.experimental.pallas import tpu as pltpu

f32 = jnp.float32
bf16 = jnp.bfloat16

D_MODEL = 2048
D_PLE = 256
WIDTH = 1024
HEAD = 64
NHEADS = 16
NPAIRS = NHEADS // 2
LORA = 64
LANES = 128
RMS_EPS = 1e-6
GN_EPS = 64e-5
CHUNK = 64
LOG2E = 1.4426950408889634
EXP_NEG_HALF = 0.6065306597126334
NEG_BIG = -1e30

Z_FX = 4096
Z_LORA = 8192
Z_FF = 8320
Z_COLS = 8448
VT_ROWS = 80
Q_COL = 256
SKIP_LOG2 = 40.0


VMEM_LIMIT = 56 * 1024 * 1024
PROJ_ROWS, PROJ_COLS = 1024, 1408
PREP_ROWS = 512
ATTN_Q, ATTN_K = 1024, 512
RWKV_ROWS, RWKV_PAIRS = 256, 8
OUT_ROWS = 512


def _dot(a, b):
    return jnp.dot(a, b, preferred_element_type=f32)


def _dot_nt(a, b):
    return lax.dot_general(a, b, (((1,), (1,)), ((), ())), preferred_element_type=f32)


def _dot_tn(a, b):
    return lax.dot_general(a, b, (((0,), (0,)), ((), ())), preferred_element_type=f32)


def _split2(x):
    hi = x.astype(bf16)
    lo = (x - hi.astype(f32)).astype(bf16)
    return hi, lo


def _split3(x):
    hi = x.astype(bf16)
    r = x - hi.astype(f32)
    mid = r.astype(bf16)
    lo = (r - mid.astype(f32)).astype(bf16)
    return hi, mid, lo


def _sel_dot(sel, x):
    hi, mid, lo = _split3(x)
    return _dot(sel, hi) + _dot(sel, mid) + _dot(sel, lo)


def _sigmoid(x):
    return 1.0 / (1.0 + jnp.exp(-x))


RELAYOUT_SLABS = 4


def _relayout_kernel(src_ref, *refs):
    slab_refs, ff_ref, o_ref = refs[:RELAYOUT_SLABS], refs[RELAYOUT_SLABS], refs[RELAYOUT_SLABS + 1]
    slabs = [r[...] for r in slab_refs]
    is_last = pl.program_id(0) == pl.num_programs(0) - 1
    slabs[-2] = jnp.where(is_last, ff_ref[0:HEAD, :], slabs[-2])
    slabs[-1] = jnp.where(is_last, ff_ref[HEAD:, :], slabs[-1])
    o_ref[...] = jnp.concatenate(slabs, axis=0).T.astype(bf16)


def _relayout_w_in(src_half, w_in_t, w_ff_t):
    n = RELAYOUT_SLABS
    slab = lambda k: pl.BlockSpec((HEAD, D_MODEL), lambda c, src: (src[n * c + k], 0))
    grid_spec = pltpu.PrefetchScalarGridSpec(
        num_scalar_prefetch=1,
        grid=(Z_COLS // (n * HEAD),),
        in_specs=[slab(k) for k in range(n)] + [pl.BlockSpec((LANES, D_MODEL), lambda c, src: (0, 0))],
        out_specs=pl.BlockSpec((D_MODEL, n * HEAD), lambda c, src: (0, c)),
    )
    return pl.pallas_call(
        _relayout_kernel,
        grid_spec=grid_spec,
        out_shape=jax.ShapeDtypeStruct((D_MODEL, Z_COLS), bf16),
        compiler_params=pltpu.CompilerParams(
            dimension_semantics=("parallel",), vmem_limit_bytes=VMEM_LIMIT),
        name="w_in_relayout",
    )(src_half, *([w_in_t] * n), w_ff_t)


def _in_proj_kernel(x_ref, g_ref, w_ref, z_ref, h_ref):
    @pl.when(pl.program_id(1) == 0)
    def _():
        x = x_ref[...]
        ms = jnp.mean(x * x, axis=-1, keepdims=True)
        h_ref[...] = (x * lax.rsqrt(ms + RMS_EPS) * g_ref[...]).astype(bf16)

    z_ref[...] = _dot(h_ref[...], w_ref[...])


def _in_proj(x, g, w, *, tm, tn):
    T = x.shape[0]
    return pl.pallas_call(
        _in_proj_kernel,
        grid=(T // tm, Z_COLS // tn),
        in_specs=[
            pl.BlockSpec((tm, D_MODEL), lambda i, j: (i, 0)),
            pl.BlockSpec((1, D_MODEL), lambda i, j: (0, 0)),
            pl.BlockSpec((D_MODEL, tn), lambda i, j: (0, j)),
        ],
        out_specs=pl.BlockSpec((tm, tn), lambda i, j: (i, j)),
        out_shape=jax.ShapeDtypeStruct((T, Z_COLS), f32),
        scratch_shapes=[pltpu.VMEM((tm, D_MODEL), bf16)],
        compiler_params=pltpu.CompilerParams(
            dimension_semantics=("parallel", "arbitrary"), vmem_limit_bytes=VMEM_LIMIT),
        name="in_proj",
    )(x, g, w)


def _fox_prep_kernel(fq_ref, fk_ref, fv_ref, ff_ref, gq_ref, gk_ref, bf_ref,
                     qt_ref, ka_ref, vt_ref, fs_ref, os_ref, carry_ref, *, tb):
    @pl.when(pl.program_id(0) == 0)
    def _():
        carry_ref[...] = jnp.zeros_like(carry_ref)

    bi = lax.broadcasted_iota(jnp.int32, (LANES, LANES), 0)
    bj = lax.broadcasted_iota(jnp.int32, (LANES, LANES), 1)
    same_head = jnp.where((bi >> 6) == (bj >> 6), 1.0, 0.0).astype(bf16)

    def head_rms(x, gain):
        cols = []
        for c in range(WIDTH // LANES):
            xc = x[:, c * LANES:(c + 1) * LANES]
            ssq = _dot((xc * xc).astype(bf16), same_head)
            cols.append(xc * lax.rsqrt(ssq * (1.0 / HEAD) + RMS_EPS))
        return jnp.concatenate(cols, axis=1) * gain

    qn = head_rms(fq_ref[...], gq_ref[...]) * (HEAD ** -0.5 * LOG2E)
    kn = head_rms(fk_ref[...], gk_ref[...])

    xf = ff_ref[...] + bf_ref[...]
    logf = jnp.minimum(xf, 0.0) - jnp.log(1.0 + jnp.exp(-jnp.abs(xf)))
    ri = lax.broadcasted_iota(jnp.int32, (tb, tb), 0)
    rj = lax.broadcasted_iota(jnp.int32, (tb, tb), 1)
    tri = jnp.where(rj <= ri, 1.0, 0.0).astype(bf16)
    cum = _sel_dot(tri, logf) + carry_ref[0:1, :]
    carry_ref[...] = jnp.broadcast_to(cum[tb - 1:tb, :], carry_ref.shape)
    cum2 = cum * LOG2E
    c_hi, c_mid, c_lo = (t.astype(f32) for t in _split3(cum2))
    srow = lax.broadcasted_iota(jnp.int32, (8, LANES), 0)
    fs_ref[0] = jnp.where(srow == 0, jnp.max(cum2, axis=0, keepdims=True), jnp.min(cum2, axis=0, keepdims=True))
    own = qn * kn
    own = jnp.concatenate([_dot(own[:, c * LANES:(c + 1) * LANES].astype(bf16), same_head)
                           for c in range(WIDTH // LANES)], axis=1)
    os_ref[0] = jnp.broadcast_to(jnp.min(own, axis=0, keepdims=True), (8, WIDTH))

    lane = lax.broadcasted_iota(jnp.int32, (tb, LANES), 1)
    erow = lax.broadcasted_iota(jnp.int32, (HEAD, tb), 0)
    q_ones = jnp.where(erow < 6, 1.0, 0.0)
    qn_t, vt = qn.T, fv_ref[...].T
    hi_t, mid_t, lo_t = c_hi.T, c_mid.T, c_lo.T
    vpad = jnp.concatenate(
        [jnp.ones((1, tb), f32), jnp.zeros((VT_ROWS - HEAD - 1, tb), f32)], axis=0)
    for h in range(NHEADS):
        even = h % 2 == 0
        base = HEAD if even else 0
        q_bias = jnp.where(erow == 0, hi_t[h:h + 1, :], jnp.where(erow == 1, mid_t[h:h + 1, :],
                 jnp.where(erow == 2, lo_t[h:h + 1, :], q_ones)))
        q_rows = qn_t[h * HEAD:(h + 1) * HEAD, :]
        qt_ref[h] = jnp.concatenate([q_rows, q_bias] if even else [q_bias, q_rows], axis=0).astype(bf16)
        fh, fm, fl = c_hi[:, h:h + 1], c_mid[:, h:h + 1], c_lo[:, h:h + 1]
        k_ones = jnp.where((lane >= base) & (lane < base + 3), 1.0, 0.0)
        k_bias = jnp.where(lane == base + 3, -fh, jnp.where(lane == base + 4, -fm,
                 jnp.where(lane == base + 5, -fl, k_ones)))
        kc = kn[:, (h // 2) * LANES:(h // 2 + 1) * LANES]
        ka_ref[h] = jnp.where((lane < HEAD) == even, kc, k_bias).astype(bf16)
        vt_ref[h, 0] = jnp.concatenate([vt[h * HEAD:(h + 1) * HEAD, :], vpad], axis=0).astype(bf16)


def _fox_prep(z, gq, gk, bfp, *, tb):
    T = z.shape[0]
    wb = Z_FX // WIDTH
    return pl.pallas_call(
        functools.partial(_fox_prep_kernel, tb=tb),
        grid=(T // tb,),
        in_specs=[
            pl.BlockSpec((tb, WIDTH), lambda i: (i, wb)),
            pl.BlockSpec((tb, WIDTH), lambda i: (i, wb + 1)),
            pl.BlockSpec((tb, WIDTH), lambda i: (i, wb + 2)),
            pl.BlockSpec((tb, LANES), lambda i: (i, Z_FF // LANES)),
            pl.BlockSpec((1, WIDTH), lambda i: (0, 0)),
            pl.BlockSpec((1, WIDTH), lambda i: (0, 0)),
            pl.BlockSpec((1, LANES), lambda i: (0, 0)),
        ],
        out_specs=[
            pl.BlockSpec((NHEADS, LANES, tb), lambda i: (0, 0, i)),
            pl.BlockSpec((NHEADS, tb, LANES), lambda i: (0, i, 0)),
            pl.BlockSpec((NHEADS, 1, VT_ROWS, tb), lambda i: (0, i, 0, 0)),
            pl.BlockSpec((1, 8, LANES), lambda i: (i, 0, 0)),
            pl.BlockSpec((1, 8, WIDTH), lambda i: (i, 0, 0)),
        ],
        out_shape=[
            jax.ShapeDtypeStruct((NHEADS, LANES, T), bf16),
            jax.ShapeDtypeStruct((NHEADS, T, LANES), bf16),
            jax.ShapeDtypeStruct((NHEADS, T // tb, VT_ROWS, tb), bf16),
            jax.ShapeDtypeStruct((T // tb, 8, LANES), f32),
            jax.ShapeDtypeStruct((T // tb, 8, WIDTH), f32),
        ],
        scratch_shapes=[pltpu.VMEM((8, LANES), f32)],
        compiler_params=pltpu.CompilerParams(
            dimension_semantics=("arbitrary",), vmem_limit_bytes=VMEM_LIMIT),
        name="fox_prep",
    )(z, z, z, z, gq, gk, bfp)


def _attn_kernel(cnt_ref, blk_ref, qt_ref, ka_hbm, vt_hbm, fg_ref, o_ref, m_ref, acc_ref, kbuf, vbuf, sem, par_ref,
                 *, tq, tk):
    g, i = pl.program_id(0), pl.program_id(1)
    nq = pl.num_programs(1)
    ndiag = tq // tk
    max_full = ndiag * nq
    step = g * nq + i
    n_full = cnt_ref[step]

    def block_id(step_, t, n_full_, i_):
        return jnp.where(t < n_full_, blk_ref[step_ * max_full + jnp.minimum(t, max_full - 1)], ndiag * i_ + t - n_full_)

    def copies(g_, j, slot):
        return (pltpu.make_async_copy(ka_hbm.at[pl.ds(2 * g_, 2), pl.ds(j * tk, tk), :], kbuf.at[slot], sem.at[0, slot]),
                pltpu.make_async_copy(vt_hbm.at[pl.ds(2 * g_, 2), j], vbuf.at[slot], sem.at[1, slot]))

    def fetch(g_, j, slot):
        for c in copies(g_, j, slot):
            c.start()

    def wait(slot):
        for c in copies(g, 0, slot):
            c.wait()

    @pl.when(step == 0)
    def _():
        par_ref[0] = 0
        fetch(g, block_id(step, 0, n_full, i), 0)

    slot0 = par_ref[0]
    par_ref[0] = (slot0 + n_full + ndiag) % 2
    m_ref[...] = jnp.full(m_ref.shape, NEG_BIG, f32)
    acc_ref[...] = jnp.zeros_like(acc_ref)

    def update(d, slot):
        ka_ref, vt_ref = kbuf.at[slot], vbuf.at[slot]

        def n_keys(q_lo):
            return tk if d < 0 else max(0, min(tk, q_lo + Q_COL - d * tk))

        chains = [(hh, qs * Q_COL) for qs in range(tq // Q_COL) for hh in range(2) if n_keys(qs * Q_COL) > 0]

        def scores(hh, q_lo):
            nk = n_keys(q_lo)
            st = _dot(ka_ref[hh, 0:nk, :], qt_ref[hh, :, q_lo:q_lo + Q_COL])
            if d >= 0 and d * tk + nk - 1 > q_lo:
                kpos = d * tk + lax.broadcasted_iota(jnp.int32, st.shape, 0)
                qpos = q_lo + lax.broadcasted_iota(jnp.int32, st.shape, 1)
                st = jnp.where(kpos <= qpos, st, NEG_BIG)
            return st

        def consume(hh, q_lo, st):
            qsl = slice(q_lo, q_lo + Q_COL)
            m_old = m_ref[hh, :, qsl]
            m_new = jnp.maximum(m_old, jnp.max(st, axis=0, keepdims=True))
            m_ref[hh, :, qsl] = m_new
            p = jnp.exp2(st - m_new).astype(bf16)
            acc_ref[hh, :, qsl] = (jnp.exp2(m_old - m_new) * acc_ref[hh, :, qsl]
                                   + _dot(vt_ref[hh, :, 0:n_keys(q_lo)], p))

        ahead = 4
        pending = [scores(*c) for c in chains[:ahead]]
        for n, chain in enumerate(chains):
            st = pending.pop(0)
            if n + ahead < len(chains):
                pending.append(scores(*chains[n + ahead]))
            consume(*chain, st)

    def run_block(t, d):
        slot = (slot0 + t) % 2
        wait(slot)
        fetch(g, block_id(step, t + 1, n_full, i), 1 - slot)
        update(d, slot)

    def full_block(t, carry):
        run_block(t, -1)
        return carry

    lax.fori_loop(0, n_full, full_block, 0)
    for d in range(ndiag - 1):
        run_block(n_full + d, d)

    last_slot = (slot0 + n_full + ndiag - 1) % 2
    wait(last_slot)

    @pl.when(step + 1 < pl.num_programs(0) * nq)
    def _():
        nstep = step + 1
        fetch(nstep // nq, block_id(nstep, 0, cnt_ref[nstep], nstep % nq), 1 - last_slot)

    update(ndiag - 1, last_slot)

    outs = []
    for hh in range(2):
        acc = acc_ref[hh]
        outs.append(acc[0:HEAD, :] / acc[HEAD:HEAD + 1, :])
    y = jnp.concatenate(outs, axis=0).T
    gate = fg_ref[...]
    o_ref[...] = (y * gate * _sigmoid(gate)).astype(o_ref.dtype)


def _attention_work_list(fstat, ostat, qk_bound, *, T, tb, tq, tk):
    nq, nk = T // tq, T // tk
    fmax_q = fstat[:, 0, :NHEADS].reshape(nq, tq // tb, NHEADS).max(axis=1)
    fmin_k = fstat[:, 1, :NHEADS].reshape(nk, tk // tb, NHEADS).min(axis=1)
    own_q = ostat[:, 0, ::HEAD].reshape(nq, tq // tb, NHEADS).min(axis=1) - 0.05 * qk_bound
    need = (fmax_q[:, None, :] - fmin_k[None, :, :] + qk_bound - own_q[:, None, :]) > -SKIP_LOG2
    need = jnp.transpose(need[:, :, 0::2] | need[:, :, 1::2], (2, 0, 1))
    before = np.arange(nk)[None, :] < (tq // tk) * np.arange(nq)[:, None]
    need = need & jnp.asarray(before)[None]
    count = jnp.sum(need, axis=2).astype(jnp.int32)
    ids = jnp.argsort(jnp.logical_not(need), axis=2, stable=True).astype(jnp.int32)
    return count.reshape(-1), ids.reshape(-1)


def _attention(qt, ka, vt, z, fstat, ostat, qk_bound, *, tb, tq, tk):
    T = ka.shape[1]
    nq = T // tq
    assert vt.shape == (NHEADS, T // tk, VT_ROWS, tk), vt.shape
    count, ids = _attention_work_list(fstat, ostat, qk_bound, T=T, tb=tb, tq=tq, tk=tk)
    fg_col = (Z_FX + 3 * WIDTH) // LANES
    grid_spec = pltpu.PrefetchScalarGridSpec(
        num_scalar_prefetch=2,
        grid=(NPAIRS, nq),
        in_specs=[
            pl.BlockSpec((2, LANES, tq), lambda g, i, cnt, blk: (g, 0, i)),
            pl.BlockSpec(memory_space=pl.ANY),
            pl.BlockSpec(memory_space=pl.ANY),
            pl.BlockSpec((tq, LANES), lambda g, i, cnt, blk: (i, fg_col + g)),
        ],
        out_specs=pl.BlockSpec((tq, LANES), lambda g, i, cnt, blk: (i, g)),
        scratch_shapes=[
            pltpu.VMEM((2, 1, tq), f32), pltpu.VMEM((2, VT_ROWS, tq), f32),
            pltpu.VMEM((2, 2, tk, LANES), bf16), pltpu.VMEM((2, 2, VT_ROWS, tk), bf16),
            pltpu.SemaphoreType.DMA((2, 2)), pltpu.SMEM((1,), jnp.int32),
        ],
    )
    return pl.pallas_call(
        functools.partial(_attn_kernel, tq=tq, tk=tk),
        grid_spec=grid_spec,
        out_shape=jax.ShapeDtypeStruct((T, WIDTH), bf16),
        compiler_params=pltpu.CompilerParams(
            dimension_semantics=("arbitrary", "arbitrary"), vmem_limit_bytes=VMEM_LIMIT),
        name="fox_attention",
    )(count, ids, qt, ka, vt, z)


P_MU_R, P_MU_K, P_MU_V, P_W0, P_A0, P_KK, P_KA, P_RK, P_LNW, P_LNB = range(10)


def _rwkv_kernel(zr_ref, zk_ref, zv_ref, zg_ref, zl_ref, par_ref, mul_ref, w2_ref, a2_ref,
                 o_ref, s_ref, prev_ref, prevl_ref, *, tb, npair):
    width = npair * LANES

    @pl.when(pl.program_id(1) == 0)
    def _():
        s_ref[...] = jnp.zeros_like(s_ref)
        prev_ref[...] = jnp.zeros_like(prev_ref)
        prevl_ref[...] = jnp.zeros_like(prevl_ref)

    par = par_ref[...]
    prow = lambda k: par[k:k + 1, :]

    def shift(x, mu, last_ref):
        row = lax.broadcasted_iota(jnp.int32, x.shape, 0)
        xp = pltpu.roll(x, 1, axis=0)
        xp = jnp.where(row == 0, last_ref[7:8, :], xp)
        last_ref[...] = x[tb - 8:tb, :]
        return x + (xp - x) * mu

    r = shift(zr_ref[...], prow(P_MU_R), prev_ref.at[0])
    k = shift(zk_ref[...], prow(P_MU_K), prev_ref.at[1])
    v = shift(zv_ref[...], prow(P_MU_V), prev_ref.at[2])
    sl = shift(zl_ref[...], mul_ref[0:1, :], prevl_ref)

    bi = lax.broadcasted_iota(jnp.int32, (LANES, LANES), 0)
    bj = lax.broadcasted_iota(jnp.int32, (LANES, LANES), 1)
    same_head = jnp.where((bi >> 6) == (bj >> 6), 1.0, 0.0).astype(bf16)

    def seg(x):
        return jnp.concatenate(
            [_dot(x[:, pr * LANES:(pr + 1) * LANES].astype(bf16), same_head) for pr in range(npair)], axis=1)

    u = prow(P_W0) + _dot(jnp.tanh(sl).astype(bf16), w2_ref[...])
    ld = -EXP_NEG_HALF * _sigmoid(u)
    av = _sigmoid(prow(P_A0) + _dot(sl.astype(bf16), a2_ref[...]))

    kk = k * prow(P_KK)
    kk = kk * lax.rsqrt(jnp.maximum(seg(kk * kk), 1e-24))
    k2 = k * (1.0 + (av - 1.0) * prow(P_KA))

    ti = lax.broadcasted_iota(jnp.int32, (tb, tb), 0)
    tj = lax.broadcasted_iota(jnp.int32, (tb, tb), 1)
    tri = jnp.where(((ti >> 6) == (tj >> 6)) & (tj <= ti), 1.0, 0.0).astype(bf16)
    ld_hi, ld_lo = _split2(ld)
    cum = _dot(tri, ld_hi) + _dot(tri, ld_lo)

    e_pos = jnp.exp(cum)
    e_neg = jnp.exp(-cum)
    a_t = -kk * jnp.exp(cum - ld)
    b_t = kk * av * e_neg
    k_t = k2 * e_neg
    r_t = r * e_pos

    ri = lax.broadcasted_iota(jnp.int32, (LANES, LANES), 0)
    rj = lax.broadcasted_iota(jnp.int32, (LANES, LANES), 1)
    strict_lower = rj < ri
    eye = jnp.where(ri == rj, 1.0, 0.0)
    qi = lax.broadcasted_iota(jnp.int32, (CHUNK, LANES), 0)
    qj = lax.broadcasted_iota(jnp.int32, (CHUNK, LANES), 1)
    incl_lower = (qj & (CHUNK - 1)) <= qi
    chunk_head0 = qj < HEAD

    def stack(x):
        return jnp.concatenate([jnp.where(chunk_head0, x, 0.0), jnp.where(chunk_head0, 0.0, x)], axis=0)

    def level_mask(b):
        sh = b.bit_length()
        return ((ri >> sh) == (rj >> sh)) & ((ri & (2 * b - 1)) >= b) & ((rj & (2 * b - 1)) < b)

    nchunk = tb // CHUNK
    units = [(pr, c) for pr in range(npair) for c in range(nchunk)]
    rows = lambda c: slice(c * CHUNK, (c + 1) * CHUNK)
    lanes = lambda pr: slice(pr * LANES, (pr + 1) * LANES)
    cut = lambda x, u: x[rows(u[1]), lanes(u[0])]
    cend = {u: cum[(u[1] + 1) * CHUNK - 1:(u[1] + 1) * CHUNK, lanes(u[0])] for u in units}

    am = {u: stack(cut(a_t, u)) for u in units}
    vm = {u: stack(cut(v, u)) for u in units}
    x = {u: _dot_nt(jnp.concatenate([am[u], cut(r_t, u)], axis=0).astype(bf16),
                    jnp.concatenate([stack(cut(b_t, u)), stack(cut(k_t, u))], axis=0).astype(bf16))
         for u in units}
    l_ab = {u: jnp.where(strict_lower, x[u][0:LANES, 0:LANES], 0.0) for u in units}
    l_ak = {u: jnp.where(strict_lower, x[u][0:LANES, LANES:], 0.0) for u in units}
    l_r = {u: jnp.where(jnp.concatenate([incl_lower, incl_lower], axis=1), x[u][LANES:, :], 0.0).astype(bf16)
           for u in units}

    inv = {u: eye + jnp.where(level_mask(1), l_ab[u], 0.0) for u in units}
    b = 2
    while b < CHUNK:
        mask = level_mask(b)
        y_ = {u: _dot(jnp.where(mask, l_ab[u], 0.0).astype(bf16), inv[u].astype(bf16)) for u in units}
        inv = {u: inv[u] + _dot(inv[u].astype(bf16), y_[u].astype(bf16)) for u in units}
        b *= 2

    lv = {u: _dot(l_ak[u].astype(bf16), vm[u].astype(bf16)) for u in units}
    tg = {u: _dot(inv[u].astype(bf16), jnp.concatenate([am[u], lv[u]], axis=1).astype(bf16)) for u in units}
    zeros = jnp.zeros((LANES, LANES), f32)
    ly = {u: _dot(l_r[u], jnp.concatenate(
              [tg[u], jnp.concatenate([zeros, vm[u]], axis=1)], axis=0).astype(bf16)) for u in units}
    bh = {u: stack(cut(kk, u) * cut(av, u) * jnp.exp(cend[u] - cut(cum, u))).astype(bf16) for u in units}
    kh = {u: stack(cut(k2, u) * jnp.exp(cend[u] - cut(cum, u))).astype(bf16) for u in units}
    ag = {u: _dot_tn(tg[u].astype(bf16), bh[u]) for u in units}
    vk = {u: _dot_tn(vm[u].astype(bf16), kh[u]) for u in units}
    q_eff = {u: (cut(r_t, u) + ly[u][:, 0:LANES]).astype(bf16) for u in units}
    a_eff = {u: ag[u][0:LANES, :].astype(bf16) for u in units}
    g_eff = {u: ag[u][LANES:, :] + vk[u] for u in units}

    ys = [[None] * npair for _ in range(nchunk)]
    state = [s_ref[pr] for pr in range(npair)]
    for c in range(nchunk):
        for pr in range(npair):
            u = (pr, c)
            s_bf = state[pr].astype(bf16)
            ys[c][pr] = _dot_nt(q_eff[u], s_bf) + ly[u][:, LANES:]
            state[pr] = state[pr] * jnp.exp(cend[u]) + _dot(s_bf, a_eff[u]) + g_eff[u]
    for pr in range(npair):
        s_ref[pr] = state[pr]

    y = jnp.concatenate([jnp.concatenate(yc, axis=1) for yc in ys], axis=0)
    mean = seg(y) * (1.0 / HEAD)
    d = y - mean
    var = seg(d * d) * (1.0 / HEAD)
    yn = d * lax.rsqrt(var + GN_EPS) * prow(P_LNW) + prow(P_LNB)
    bonus = seg(r * k2 * prow(P_RK)) * v
    g = zg_ref[...]
    o_ref[...] = ((yn + bonus) * g * _sigmoid(g)).astype(o_ref.dtype)


def _rwkv(z, par, mul, w2p, a2p, *, tb, npair):
    T = z.shape[0]
    width = npair * LANES
    nb = WIDTH // width
    zspec = lambda off: pl.BlockSpec((tb, width), lambda g, t: (t, off + g))
    return pl.pallas_call(
        functools.partial(_rwkv_kernel, tb=tb, npair=npair),
        grid=(nb, T // tb),
        in_specs=[
            zspec(0), zspec(nb), zspec(2 * nb), zspec(3 * nb),
            pl.BlockSpec((tb, LANES), lambda g, t: (t, Z_LORA // LANES)),
            pl.BlockSpec((16, width), lambda g, t: (0, g)),
            pl.BlockSpec((8, LANES), lambda g, t: (0, 0)),
            pl.BlockSpec((LANES, width), lambda g, t: (0, g)),
            pl.BlockSpec((LANES, width), lambda g, t: (0, g)),
        ],
        out_specs=pl.BlockSpec((tb, width), lambda g, t: (t, g)),
        out_shape=jax.ShapeDtypeStruct((T, WIDTH), bf16),
        scratch_shapes=[pltpu.VMEM((npair, LANES, LANES), f32), pltpu.VMEM((3, 8, width), f32),
                        pltpu.VMEM((8, LANES), f32)],
        compiler_params=pltpu.CompilerParams(
            dimension_semantics=("parallel", "arbitrary"), vmem_limit_bytes=VMEM_LIMIT),
        name="rwkv7",
    )(z, z, z, z, z, par, mul, w2p, a2p)


def _out_kernel(perm_ref, x_ref, yr_ref, yf_ref, p_ref, wo_ref, wg_ref, wp_ref, g1_ref, g2_ref, o_ref, wfx_ref):
    @pl.when(pl.program_id(0) == 0)
    def _():
        for s in range(NHEADS):
            src = pl.multiple_of(WIDTH + perm_ref[s] * HEAD, HEAD)
            wfx_ref[s * HEAD:(s + 1) * HEAD, :] = wo_ref[pl.ds(src, HEAD), :]

    def rms(t, g):
        return t * lax.rsqrt(jnp.mean(t * t, axis=-1, keepdims=True) + RMS_EPS) * g

    m = _dot(yr_ref[...], wo_ref[0:WIDTH, :]) + _dot(yf_ref[...], wfx_ref[...])
    x1 = x_ref[...] + rms(m, g1_ref[...])
    gate = _sigmoid(_dot(rms(x1, g2_ref[...]).astype(bf16), wg_ref[...]))
    o_ref[...] = x1 + gate * _dot(p_ref[...].astype(bf16), wp_ref[...])


def _out_proj(perm, x, yr, yf, p, wo, wg, wp, g1, g2, *, tm):
    T = x.shape[0]
    const = lambda shape: pl.BlockSpec(shape, lambda i, perm: (0, 0), pipeline_mode=pl.Buffered(1))
    rows = lambda width: pl.BlockSpec((tm, width), lambda i, perm: (i, 0))
    grid_spec = pltpu.PrefetchScalarGridSpec(
        num_scalar_prefetch=1,
        grid=(T // tm,),
        in_specs=[
            rows(D_MODEL), rows(WIDTH), rows(WIDTH), rows(D_PLE),
            const((D_MODEL, D_MODEL)), const((D_MODEL, D_MODEL)), const((D_PLE, D_MODEL)),
            const((1, D_MODEL)), const((1, D_MODEL)),
        ],
        out_specs=rows(D_MODEL),
        scratch_shapes=[pltpu.VMEM((WIDTH, D_MODEL), bf16)],
    )
    return pl.pallas_call(
        _out_kernel,
        grid_spec=grid_spec,
        out_shape=jax.ShapeDtypeStruct((T, D_MODEL), f32),
        compiler_params=pltpu.CompilerParams(
            dimension_semantics=("arbitrary",), vmem_limit_bytes=VMEM_LIMIT),
        name="out_proj",
    )(perm, x, yr, yf, p, wo, wg, wp, g1, g2)


def _layer(x, p, pre_g, w_in, mu_r, mu_k, mu_v, mu_w, mu_a, w0, w2, a0, a2, k_k, k_a, r_k, ln_w, ln_b,
           b_f, q_g, k_g, w_out, post_g, ple_g, w_gate, w_ple):
    T = x.shape[0]
    rw_end = 4 * WIDTH
    lora_end = rw_end + 2 * LORA
    fx_end = lora_end + 4 * WIDTH
    assert x.shape == (T, D_MODEL) and w_in.shape == (D_MODEL, fx_end + NHEADS), (x.shape, w_in.shape)
    assert all(T % min(t, T) == 0 for t in (PROJ_ROWS, PREP_ROWS, ATTN_Q, ATTN_K, RWKV_ROWS, OUT_ROWS)), T
    perm = jnp.argsort(b_f).astype(jnp.int32)
    fx_half = lora_end // HEAD
    src_half = jnp.concatenate([
        jnp.arange(rw_end // HEAD, dtype=jnp.int32),
        (fx_half + NHEADS * jnp.arange(4, dtype=jnp.int32)[:, None] + perm[None, :]).reshape(-1),
        jnp.arange(rw_end // HEAD, lora_end // HEAD, dtype=jnp.int32),
        jnp.zeros((2,), jnp.int32)])
    w_in_t = w_in.T
    pick = (jnp.pad(perm, (0, LANES - NHEADS), constant_values=-1)[:, None] == jnp.arange(NHEADS, dtype=jnp.int32)[None, :])
    w_ff_t = jnp.dot(pick.astype(f32), w_in_t[fx_end:], precision=lax.Precision.HIGHEST)
    w = _relayout_w_in(src_half, w_in_t, w_ff_t)
    b_f = jnp.take(b_f, perm)
    z = _in_proj(x, pre_g.reshape(1, D_MODEL), w, tm=min(PROJ_ROWS, T), tn=PROJ_COLS)

    gq = jnp.tile(q_g, NHEADS).reshape(1, WIDTH)
    gk = jnp.tile(k_g, NHEADS).reshape(1, WIDTH)
    bfp = jnp.pad(b_f, (0, LANES - NHEADS)).reshape(1, LANES)
    tb = min(PREP_ROWS, T)
    qt, ka, vt, fstat, ostat = _fox_prep(z, gq, gk, bfp, tb=tb)
    qk_bound = 1.02 * HEAD ** 0.5 * LOG2E * jnp.max(jnp.abs(q_g)) * jnp.max(jnp.abs(k_g))
    y_fx = _attention(qt, ka, vt, z, fstat, ostat, qk_bound, tb=tb, tq=min(ATTN_Q, T), tk=min(ATTN_K, T))

    par = jnp.stack([mu_r, mu_k, mu_v, w0, a0, k_k, k_a, r_k.reshape(WIDTH), ln_w, ln_b])
    par = jnp.pad(par, ((0, 16 - par.shape[0]), (0, 0)))
    mul = jnp.broadcast_to(jnp.concatenate([mu_w, mu_a]).reshape(1, LANES), (8, LANES))
    zeros = jnp.zeros((LORA, WIDTH), f32)
    w2p = jnp.concatenate([w2, zeros], axis=0).astype(bf16)
    a2p = jnp.concatenate([zeros, a2], axis=0).astype(bf16)
    y_rw = _rwkv(z, par, mul, w2p, a2p, tb=min(RWKV_ROWS, T), npair=RWKV_PAIRS)

    return _out_proj(perm, x, y_rw, y_fx, p, w_out.astype(bf16), w_gate.astype(bf16), w_ple.astype(bf16),
                     post_g.reshape(1, D_MODEL), ple_g.reshape(1, D_MODEL), tm=min(OUT_ROWS, T))


def kernel(x, p, pre_norm_g, w_in, rw_mu_r, rw_mu_k, rw_mu_v, rw_mu_w, rw_mu_a, rw_w0, rw_w2, rw_a0, rw_a2,
           rw_k_k, rw_k_a, rw_r_k, rw_ln_w, rw_ln_b, fx_b_f, fx_q_g, fx_k_g, w_out, post_norm_g, ple_norm_g,
           w_ple_gate, w_ple):
    B = x.shape[0]
    outs = []
    for b in range(B):
        xb = x[b]
        for i in range(p.shape[0]):
            xb = _layer(xb, p[i, b], pre_norm_g[i], w_in[i], rw_mu_r[i], rw_mu_k[i], rw_mu_v[i], rw_mu_w[i],
                        rw_mu_a[i], rw_w0[i], rw_w2[i], rw_a0[i], rw_a2[i], rw_k_k[i], rw_k_a[i], rw_r_k[i],
                        rw_ln_w[i], rw_ln_b[i], fx_b_f[i], fx_q_g[i], fx_k_g[i], w_out[i], post_norm_g[i],
                        ple_norm_g[i], w_ple_gate[i], w_ple[i])
        outs.append(xb)
    return jnp.stack(outs)
```

```python
import functools

import numpy as np
import jax
import jax.numpy as jnp
from jax import lax
from jax.experimental import pallas as pl
from jax.experimental.pallas import tpu as pltpu

f32 = jnp.float32
bf16 = jnp.bfloat16

D_MODEL = 2048
D_PLE = 256
WIDTH = 1024
HEAD = 64
NHEADS = 16
NPAIRS = NHEADS // 2
LORA = 64
LANES = 128
RMS_EPS = 1e-6
GN_EPS = 64e-5
CHUNK = 64
LOG2E = 1.4426950408889634
EXP_NEG_HALF = 0.6065306597126334
NEG_BIG = -1e30

Z_FX = 4096
Z_LORA = 8192
Z_FF = 8320
Z_COLS = 8448
VT_ROWS = 80
Q_COL = 128
SKIP_LOG2 = 40.0


VMEM_LIMIT = 56 * 1024 * 1024
PROJ_ROWS, PROJ_COLS = 1024, 1408
PREP_ROWS = 512
ATTN_Q, ATTN_K = 1024, 512
RWKV_ROWS, RWKV_PAIRS = 256, 8
OUT_ROWS = 512


def _dot(a, b):
    return jnp.dot(a, b, preferred_element_type=f32)


def _dot_nt(a, b):
    return lax.dot_general(a, b, (((1,), (1,)), ((), ())), preferred_element_type=f32)


def _dot_tn(a, b):
    return lax.dot_general(a, b, (((0,), (0,)), ((), ())), preferred_element_type=f32)


def _split2(x):
    hi = x.astype(bf16)
    lo = (x - hi.astype(f32)).astype(bf16)
    return hi, lo


def _split3(x):
    hi = x.astype(bf16)
    r = x - hi.astype(f32)
    mid = r.astype(bf16)
    lo = (r - mid.astype(f32)).astype(bf16)
    return hi, mid, lo


def _sel_dot(sel, x):
    hi, mid, lo = _split3(x)
    return _dot(sel, hi) + _dot(sel, mid) + _dot(sel, lo)


def _sigmoid(x):
    return 1.0 / (1.0 + jnp.exp(-x))


RELAYOUT_SLABS = 4


def _relayout_kernel(src_ref, *refs):
    slab_refs, ff_ref, o_ref = refs[:RELAYOUT_SLABS], refs[RELAYOUT_SLABS], refs[RELAYOUT_SLABS + 1]
    slabs = [r[...] for r in slab_refs]
    is_last = pl.program_id(0) == pl.num_programs(0) - 1
    slabs[-2] = jnp.where(is_last, ff_ref[0:HEAD, :], slabs[-2])
    slabs[-1] = jnp.where(is_last, ff_ref[HEAD:, :], slabs[-1])
    o_ref[...] = jnp.concatenate(slabs, axis=0).T.astype(bf16)


def _relayout_w_in(src_half, w_in_t, w_ff_t):
    n = RELAYOUT_SLABS
    slab = lambda k: pl.BlockSpec((HEAD, D_MODEL), lambda c, src: (src[n * c + k], 0))
    grid_spec = pltpu.PrefetchScalarGridSpec(
        num_scalar_prefetch=1,
        grid=(Z_COLS // (n * HEAD),),
        in_specs=[slab(k) for k in range(n)] + [pl.BlockSpec((LANES, D_MODEL), lambda c, src: (0, 0))],
        out_specs=pl.BlockSpec((D_MODEL, n * HEAD), lambda c, src: (0, c)),
    )
    return pl.pallas_call(
        _relayout_kernel,
        grid_spec=grid_spec,
        out_shape=jax.ShapeDtypeStruct((D_MODEL, Z_COLS), bf16),
        compiler_params=pltpu.CompilerParams(
            dimension_semantics=("parallel",), vmem_limit_bytes=VMEM_LIMIT),
        name="w_in_relayout",
    )(src_half, *([w_in_t] * n), w_ff_t)


def _in_proj_kernel(x_ref, g_ref, w_ref, z_ref, h_ref):
    @pl.when(pl.program_id(1) == 0)
    def _():
        x = x_ref[...]
        ms = jnp.mean(x * x, axis=-1, keepdims=True)
        h_ref[...] = (x * lax.rsqrt(ms + RMS_EPS) * g_ref[...]).astype(bf16)

    z_ref[...] = _dot(h_ref[...], w_ref[...])


def _in_proj(x, g, w, *, tm, tn):
    T = x.shape[0]
    return pl.pallas_call(
        _in_proj_kernel,
        grid=(T // tm, Z_COLS // tn),
        in_specs=[
            pl.BlockSpec((tm, D_MODEL), lambda i, j: (i, 0)),
            pl.BlockSpec((1, D_MODEL), lambda i, j: (0, 0)),
            pl.BlockSpec((D_MODEL, tn), lambda i, j: (0, j)),
        ],
        out_specs=pl.BlockSpec((tm, tn), lambda i, j: (i, j)),
        out_shape=jax.ShapeDtypeStruct((T, Z_COLS), f32),
        scratch_shapes=[pltpu.VMEM((tm, D_MODEL), bf16)],
        compiler_params=pltpu.CompilerParams(
            dimension_semantics=("parallel", "arbitrary"), vmem_limit_bytes=VMEM_LIMIT),
        name="in_proj",
    )(x, g, w)


def _fox_prep_kernel(fq_ref, fk_ref, fv_ref, ff_ref, gq_ref, gk_ref, bf_ref,
                     qt_ref, ka_ref, vt_ref, fs_ref, os_ref, carry_ref, *, tb):
    @pl.when(pl.program_id(0) == 0)
    def _():
        carry_ref[...] = jnp.zeros_like(carry_ref)

    bi = lax.broadcasted_iota(jnp.int32, (LANES, LANES), 0)
    bj = lax.broadcasted_iota(jnp.int32, (LANES, LANES), 1)
    same_head = jnp.where((bi >> 6) == (bj >> 6), 1.0, 0.0).astype(bf16)

    def head_rms(x, gain):
        cols = []
        for c in range(WIDTH // LANES):
            xc = x[:, c * LANES:(c + 1) * LANES]
            ssq = _dot((xc * xc).astype(bf16), same_head)
            cols.append(xc * lax.rsqrt(ssq * (1.0 / HEAD) + RMS_EPS))
        return jnp.concatenate(cols, axis=1) * gain

    qn = head_rms(fq_ref[...], gq_ref[...]) * (HEAD ** -0.5 * LOG2E)
    kn = head_rms(fk_ref[...], gk_ref[...])

    xf = ff_ref[...] + bf_ref[...]
    logf = jnp.minimum(xf, 0.0) - jnp.log(1.0 + jnp.exp(-jnp.abs(xf)))
    ri = lax.broadcasted_iota(jnp.int32, (tb, tb), 0)
    rj = lax.broadcasted_iota(jnp.int32, (tb, tb), 1)
    tri = jnp.where(rj <= ri, 1.0, 0.0).astype(bf16)
    cum = _sel_dot(tri, logf) + carry_ref[0:1, :]
    carry_ref[...] = jnp.broadcast_to(cum[tb - 1:tb, :], carry_ref.shape)
    cum2 = cum * LOG2E
    c_hi, c_mid, c_lo = (t.astype(f32) for t in _split3(cum2))
    srow = lax.broadcasted_iota(jnp.int32, (8, LANES), 0)
    fs_ref[0] = jnp.where(srow == 0, jnp.max(cum2, axis=0, keepdims=True), jnp.min(cum2, axis=0, keepdims=True))
    own = qn * kn
    own = jnp.concatenate([_dot(own[:, c * LANES:(c + 1) * LANES].astype(bf16), same_head)
                           for c in range(WIDTH // LANES)], axis=1)
    os_ref[0] = jnp.broadcast_to(jnp.min(own, axis=0, keepdims=True), (8, WIDTH))

    lane = lax.broadcasted_iota(jnp.int32, (tb, LANES), 1)
    erow = lax.broadcasted_iota(jnp.int32, (HEAD, tb), 0)
    q_ones = jnp.where(erow < 6, 1.0, 0.0)
    qn_t, vt = qn.T, fv_ref[...].T
    hi_t, mid_t, lo_t = c_hi.T, c_mid.T, c_lo.T
    vpad = jnp.concatenate(
        [jnp.ones((1, tb), f32), jnp.zeros((VT_ROWS - HEAD - 1, tb), f32)], axis=0)
    for h in range(NHEADS):
        even = h % 2 == 0
        base = HEAD if even else 0
        q_bias = jnp.where(erow == 0, hi_t[h:h + 1, :], jnp.where(erow == 1, mid_t[h:h + 1, :],
                 jnp.where(erow == 2, lo_t[h:h + 1, :], q_ones)))
        q_rows = qn_t[h * HEAD:(h + 1) * HEAD, :]
        qt_ref[h] = jnp.concatenate([q_rows, q_bias] if even else [q_bias, q_rows], axis=0).astype(bf16)
        fh, fm, fl = c_hi[:, h:h + 1], c_mid[:, h:h + 1], c_lo[:, h:h + 1]
        k_ones = jnp.where((lane >= base) & (lane < base + 3), 1.0, 0.0)
        k_bias = jnp.where(lane == base + 3, -fh, jnp.where(lane == base + 4, -fm,
                 jnp.where(lane == base + 5, -fl, k_ones)))
        kc = kn[:, (h // 2) * LANES:(h // 2 + 1) * LANES]
        ka_ref[h] = jnp.where((lane < HEAD) == even, kc, k_bias).astype(bf16)
        vt_ref[h, 0] = jnp.concatenate([vt[h * HEAD:(h + 1) * HEAD, :], vpad], axis=0).astype(bf16)


def _fox_prep(z, gq, gk, bfp, *, tb):
    T = z.shape[0]
    wb = Z_FX // WIDTH
    return pl.pallas_call(
        functools.partial(_fox_prep_kernel, tb=tb),
        grid=(T // tb,),
        in_specs=[
            pl.BlockSpec((tb, WIDTH), lambda i: (i, wb)),
            pl.BlockSpec((tb, WIDTH), lambda i: (i, wb + 1)),
            pl.BlockSpec((tb, WIDTH), lambda i: (i, wb + 2)),
            pl.BlockSpec((tb, LANES), lambda i: (i, Z_FF // LANES)),
            pl.BlockSpec((1, WIDTH), lambda i: (0, 0)),
            pl.BlockSpec((1, WIDTH), lambda i: (0, 0)),
            pl.BlockSpec((1, LANES), lambda i: (0, 0)),
        ],
        out_specs=[
            pl.BlockSpec((NHEADS, LANES, tb), lambda i: (0, 0, i)),
            pl.BlockSpec((NHEADS, tb, LANES), lambda i: (0, i, 0)),
            pl.BlockSpec((NHEADS, 1, VT_ROWS, tb), lambda i: (0, i, 0, 0)),
            pl.BlockSpec((1, 8, LANES), lambda i: (i, 0, 0)),
            pl.BlockSpec((1, 8, WIDTH), lambda i: (i, 0, 0)),
        ],
        out_shape=[
            jax.ShapeDtypeStruct((NHEADS, LANES, T), bf16),
            jax.ShapeDtypeStruct((NHEADS, T, LANES), bf16),
            jax.ShapeDtypeStruct((NHEADS, T // tb, VT_ROWS, tb), bf16),
            jax.ShapeDtypeStruct((T // tb, 8, LANES), f32),
            jax.ShapeDtypeStruct((T // tb, 8, WIDTH), f32),
        ],
        scratch_shapes=[pltpu.VMEM((8, LANES), f32)],
        compiler_params=pltpu.CompilerParams(
            dimension_semantics=("arbitrary",), vmem_limit_bytes=VMEM_LIMIT),
        name="fox_prep",
    )(z, z, z, z, gq, gk, bfp)


def _attn_kernel(cnt_ref, blk_ref, qt_ref, ka_hbm, vt_hbm, fg_ref, o_ref, m_ref, acc_ref, kbuf, vbuf, sem, par_ref,
                 *, tq, tk):
    g, i = pl.program_id(0), pl.program_id(1)
    nq = pl.num_programs(1)
    ndiag = tq // tk
    max_full = ndiag * nq
    step = g * nq + i
    n_full = cnt_ref[step]

    def block_id(step_, t, n_full_, i_):
        return jnp.where(t < n_full_, blk_ref[step_ * max_full + jnp.minimum(t, max_full - 1)], ndiag * i_ + t - n_full_)

    def copies(g_, j, slot):
        return (pltpu.make_async_copy(ka_hbm.at[pl.ds(2 * g_, 2), pl.ds(j * tk, tk), :], kbuf.at[slot], sem.at[0, slot]),
                pltpu.make_async_copy(vt_hbm.at[pl.ds(2 * g_, 2), j], vbuf.at[slot], sem.at[1, slot]))

    def fetch(g_, j, slot):
        for c in copies(g_, j, slot):
            c.start()

    def wait(slot):
        for c in copies(g, 0, slot):
            c.wait()

    @pl.when(step == 0)
    def _():
        par_ref[0] = 0
        fetch(g, block_id(step, 0, n_full, i), 0)

    slot0 = par_ref[0]
    par_ref[0] = (slot0 + n_full + ndiag) % 2
    m_ref[...] = jnp.full(m_ref.shape, NEG_BIG, f32)
    acc_ref[...] = jnp.zeros_like(acc_ref)

    def update(d, slot):
        ka_ref, vt_ref = kbuf.at[slot], vbuf.at[slot]

        def n_keys(q_lo):
            return tk if d < 0 else max(0, min(tk, q_lo + Q_COL - d * tk))

        chains = [(hh, qs * Q_COL) for qs in range(tq // Q_COL) for hh in range(2) if n_keys(qs * Q_COL) > 0]

        def scores(hh, q_lo):
            nk = n_keys(q_lo)
            st = _dot(ka_ref[hh, 0:nk, :], qt_ref[hh, :, q_lo:q_lo + Q_COL])
            if d >= 0 and d * tk + nk - 1 > q_lo:
                kpos = d * tk + lax.broadcasted_iota(jnp.int32, st.shape, 0)
                qpos = q_lo + lax.broadcasted_iota(jnp.int32, st.shape, 1)
                st = jnp.where(kpos <= qpos, st, NEG_BIG)
            return st

        def consume(hh, q_lo, st):
            qsl = slice(q_lo, q_lo + Q_COL)
            m_old = m_ref[hh, :, qsl]
            m_new = jnp.maximum(m_old, jnp.max(st, axis=0, keepdims=True))
            m_ref[hh, :, qsl] = m_new
            p = jnp.exp2(st - m_new).astype(bf16)
            acc_ref[hh, :, qsl] = (jnp.exp2(m_old - m_new) * acc_ref[hh, :, qsl]
                                   + _dot(vt_ref[hh, :, 0:n_keys(q_lo)], p))

        ahead = 8
        pending = [scores(*c) for c in chains[:ahead]]
        for n, chain in enumerate(chains):
            st = pending.pop(0)
            if n + ahead < len(chains):
                pending.append(scores(*chains[n + ahead]))
            consume(*chain, st)

    def run_block(t, d):
        slot = (slot0 + t) % 2
        wait(slot)
        fetch(g, block_id(step, t + 1, n_full, i), 1 - slot)
        update(d, slot)

    def full_block(t, carry):
        run_block(t, -1)
        return carry

    lax.fori_loop(0, n_full, full_block, 0)
    for d in range(ndiag - 1):
        run_block(n_full + d, d)

    last_slot = (slot0 + n_full + ndiag - 1) % 2
    wait(last_slot)

    @pl.when(step + 1 < pl.num_programs(0) * nq)
    def _():
        nstep = step + 1
        fetch(nstep // nq, block_id(nstep, 0, cnt_ref[nstep], nstep % nq), 1 - last_slot)

    update(ndiag - 1, last_slot)

    outs = []
    for hh in range(2):
        acc = acc_ref[hh]
        outs.append(acc[0:HEAD, :] / acc[HEAD:HEAD + 1, :])
    y = jnp.concatenate(outs, axis=0).T
    gate = fg_ref[...]
    o_ref[...] = (y * gate * _sigmoid(gate)).astype(o_ref.dtype)


def _attention_work_list(fstat, ostat, qk_bound, *, T, tb, tq, tk):
    nq, nk = T // tq, T // tk
    fmax_q = fstat[:, 0, :NHEADS].reshape(nq, tq // tb, NHEADS).max(axis=1)
    fmin_k = fstat[:, 1, :NHEADS].reshape(nk, tk // tb, NHEADS).min(axis=1)
    own_q = ostat[:, 0, ::HEAD].reshape(nq, tq // tb, NHEADS).min(axis=1) - 0.05 * qk_bound
    need = (fmax_q[:, None, :] - fmin_k[None, :, :] + qk_bound - own_q[:, None, :]) > -SKIP_LOG2
    need = jnp.transpose(need[:, :, 0::2] | need[:, :, 1::2], (2, 0, 1))
    before = np.arange(nk)[None, :] < (tq // tk) * np.arange(nq)[:, None]
    need = need & jnp.asarray(before)[None]
    count = jnp.sum(need, axis=2).astype(jnp.int32)
    ids = jnp.argsort(jnp.logical_not(need), axis=2, stable=True).astype(jnp.int32)
    return count.reshape(-1), ids.reshape(-1)


def _attention(qt, ka, vt, z, fstat, ostat, qk_bound, *, tb, tq, tk):
    T = ka.shape[1]
    nq = T // tq
    assert vt.shape == (NHEADS, T // tk, VT_ROWS, tk), vt.shape
    count, ids = _attention_work_list(fstat, ostat, qk_bound, T=T, tb=tb, tq=tq, tk=tk)
    fg_col = (Z_FX + 3 * WIDTH) // LANES
    grid_spec = pltpu.PrefetchScalarGridSpec(
        num_scalar_prefetch=2,
        grid=(NPAIRS, nq),
        in_specs=[
            pl.BlockSpec((2, LANES, tq), lambda g, i, cnt, blk: (g, 0, i)),
            pl.BlockSpec(memory_space=pl.ANY),
            pl.BlockSpec(memory_space=pl.ANY),
            pl.BlockSpec((tq, LANES), lambda g, i, cnt, blk: (i, fg_col + g)),
        ],
        out_specs=pl.BlockSpec((tq, LANES), lambda g, i, cnt, blk: (i, g)),
        scratch_shapes=[
            pltpu.VMEM((2, 1, tq), f32), pltpu.VMEM((2, VT_ROWS, tq), f32),
            pltpu.VMEM((2, 2, tk, LANES), bf16), pltpu.VMEM((2, 2, VT_ROWS, tk), bf16),
            pltpu.SemaphoreType.DMA((2, 2)), pltpu.SMEM((1,), jnp.int32),
        ],
    )
    return pl.pallas_call(
        functools.partial(_attn_kernel, tq=tq, tk=tk),
        grid_spec=grid_spec,
        out_shape=jax.ShapeDtypeStruct((T, WIDTH), bf16),
        compiler_params=pltpu.CompilerParams(
            dimension_semantics=("arbitrary", "arbitrary"), vmem_limit_bytes=VMEM_LIMIT),
        name="fox_attention",
    )(count, ids, qt, ka, vt, z)


P_MU_R, P_MU_K, P_MU_V, P_W0, P_A0, P_KK, P_KA, P_RK, P_LNW, P_LNB = range(10)


def _rwkv_kernel(zr_ref, zk_ref, zv_ref, zg_ref, zl_ref, par_ref, mul_ref, w2_ref, a2_ref,
                 o_ref, s_ref, prev_ref, prevl_ref, *, tb, npair):
    width = npair * LANES

    @pl.when(pl.program_id(1) == 0)
    def _():
        s_ref[...] = jnp.zeros_like(s_ref)
        prev_ref[...] = jnp.zeros_like(prev_ref)
        prevl_ref[...] = jnp.zeros_like(prevl_ref)

    par = par_ref[...]
    prow = lambda k: par[k:k + 1, :]

    def shift(x, mu, last_ref):
        row = lax.broadcasted_iota(jnp.int32, x.shape, 0)
        xp = pltpu.roll(x, 1, axis=0)
        xp = jnp.where(row == 0, last_ref[7:8, :], xp)
        last_ref[...] = x[tb - 8:tb, :]
        return x + (xp - x) * mu

    r = shift(zr_ref[...], prow(P_MU_R), prev_ref.at[0])
    k = shift(zk_ref[...], prow(P_MU_K), prev_ref.at[1])
    v = shift(zv_ref[...], prow(P_MU_V), prev_ref.at[2])
    sl = shift(zl_ref[...], mul_ref[0:1, :], prevl_ref)

    bi = lax.broadcasted_iota(jnp.int32, (LANES, LANES), 0)
    bj = lax.broadcasted_iota(jnp.int32, (LANES, LANES), 1)
    same_head = jnp.where((bi >> 6) == (bj >> 6), 1.0, 0.0).astype(bf16)

    def seg(x):
        return jnp.concatenate(
            [_dot(x[:, pr * LANES:(pr + 1) * LANES].astype(bf16), same_head) for pr in range(npair)], axis=1)

    u = prow(P_W0) + _dot(jnp.tanh(sl).astype(bf16), w2_ref[...])
    ld = -EXP_NEG_HALF * _sigmoid(u)
    av = _sigmoid(prow(P_A0) + _dot(sl.astype(bf16), a2_ref[...]))

    kk = k * prow(P_KK)
    kk = kk * lax.rsqrt(jnp.maximum(seg(kk * kk), 1e-24))
    k2 = k * (1.0 + (av - 1.0) * prow(P_KA))

    ti = lax.broadcasted_iota(jnp.int32, (tb, tb), 0)
    tj = lax.broadcasted_iota(jnp.int32, (tb, tb), 1)
    tri = jnp.where(((ti >> 6) == (tj >> 6)) & (tj <= ti), 1.0, 0.0).astype(bf16)
    ld_hi, ld_lo = _split2(ld)
    cum = _dot(tri, ld_hi) + _dot(tri, ld_lo)

    e_pos = jnp.exp(cum)
    e_neg = jnp.exp(-cum)
    a_t = -kk * jnp.exp(cum - ld)
    b_t = kk * av * e_neg
    k_t = k2 * e_neg
    r_t = r * e_pos

    ri = lax.broadcasted_iota(jnp.int32, (LANES, LANES), 0)
    rj = lax.broadcasted_iota(jnp.int32, (LANES, LANES), 1)
    strict_lower = rj < ri
    eye = jnp.where(ri == rj, 1.0, 0.0)
    qi = lax.broadcasted_iota(jnp.int32, (CHUNK, LANES), 0)
    qj = lax.broadcasted_iota(jnp.int32, (CHUNK, LANES), 1)
    incl_lower = (qj & (CHUNK - 1)) <= qi
    chunk_head0 = qj < HEAD

    def stack(x):
        return jnp.concatenate([jnp.where(chunk_head0, x, 0.0), jnp.where(chunk_head0, 0.0, x)], axis=0)

    def level_mask(b):
        sh = b.bit_length()
        return ((ri >> sh) == (rj >> sh)) & ((ri & (2 * b - 1)) >= b) & ((rj & (2 * b - 1)) < b)

    nchunk = tb // CHUNK
    units = [(pr, c) for pr in range(npair) for c in range(nchunk)]
    rows = lambda c: slice(c * CHUNK, (c + 1) * CHUNK)
    lanes = lambda pr: slice(pr * LANES, (pr + 1) * LANES)
    cut = lambda x, u: x[rows(u[1]), lanes(u[0])]
    cend = {u: cum[(u[1] + 1) * CHUNK - 1:(u[1] + 1) * CHUNK, lanes(u[0])] for u in units}

    am = {u: stack(cut(a_t, u)) for u in units}
    vm = {u: stack(cut(v, u)) for u in units}
    x = {u: _dot_nt(jnp.concatenate([am[u], cut(r_t, u)], axis=0).astype(bf16),
                    jnp.concatenate([stack(cut(b_t, u)), stack(cut(k_t, u))], axis=0).astype(bf16))
         for u in units}
    l_ab = {u: jnp.where(strict_lower, x[u][0:LANES, 0:LANES], 0.0) for u in units}
    l_ak = {u: jnp.where(strict_lower, x[u][0:LANES, LANES:], 0.0) for u in units}
    l_r = {u: jnp.where(jnp.concatenate([incl_lower, incl_lower], axis=1), x[u][LANES:, :], 0.0).astype(bf16)
           for u in units}

    inv = {u: eye + jnp.where(level_mask(1), l_ab[u], 0.0) for u in units}
    b = 2
    while b < CHUNK:
        mask = level_mask(b)
        y_ = {u: _dot(jnp.where(mask, l_ab[u], 0.0).astype(bf16), inv[u].astype(bf16)) for u in units}
        inv = {u: inv[u] + _dot(inv[u].astype(bf16), y_[u].astype(bf16)) for u in units}
        b *= 2

    lv = {u: _dot(l_ak[u].astype(bf16), vm[u].astype(bf16)) for u in units}
    tg = {u: _dot(inv[u].astype(bf16), jnp.concatenate([am[u], lv[u]], axis=1).astype(bf16)) for u in units}
    zeros = jnp.zeros((LANES, LANES), f32)
    ly = {u: _dot(l_r[u], jnp.concatenate(
              [tg[u], jnp.concatenate([zeros, vm[u]], axis=1)], axis=0).astype(bf16)) for u in units}
    bh = {u: stack(cut(kk, u) * cut(av, u) * jnp.exp(cend[u] - cut(cum, u))).astype(bf16) for u in units}
    kh = {u: stack(cut(k2, u) * jnp.exp(cend[u] - cut(cum, u))).astype(bf16) for u in units}
    ag = {u: _dot_tn(tg[u].astype(bf16), bh[u]) for u in units}
    vk = {u: _dot_tn(vm[u].astype(bf16), kh[u]) for u in units}
    q_eff = {u: (cut(r_t, u) + ly[u][:, 0:LANES]).astype(bf16) for u in units}
    a_eff = {u: ag[u][0:LANES, :].astype(bf16) for u in units}
    g_eff = {u: ag[u][LANES:, :] + vk[u] for u in units}

    ys = [[None] * npair for _ in range(nchunk)]
    state = [s_ref[pr] for pr in range(npair)]
    for c in range(nchunk):
        for pr in range(npair):
            u = (pr, c)
            s_bf = state[pr].astype(bf16)
            ys[c][pr] = _dot_nt(q_eff[u], s_bf) + ly[u][:, LANES:]
            state[pr] = state[pr] * jnp.exp(cend[u]) + _dot(s_bf, a_eff[u]) + g_eff[u]
    for pr in range(npair):
        s_ref[pr] = state[pr]

    y = jnp.concatenate([jnp.concatenate(yc, axis=1) for yc in ys], axis=0)
    mean = seg(y) * (1.0 / HEAD)
    d = y - mean
    var = seg(d * d) * (1.0 / HEAD)
    yn = d * lax.rsqrt(var + GN_EPS) * prow(P_LNW) + prow(P_LNB)
    bonus = seg(r * k2 * prow(P_RK)) * v
    g = zg_ref[...]
    o_ref[...] = ((yn + bonus) * g * _sigmoid(g)).astype(o_ref.dtype)


def _rwkv(z, par, mul, w2p, a2p, *, tb, npair):
    T = z.shape[0]
    width = npair * LANES
    nb = WIDTH // width
    zspec = lambda off: pl.BlockSpec((tb, width), lambda g, t: (t, off + g))
    return pl.pallas_call(
        functools.partial(_rwkv_kernel, tb=tb, npair=npair),
        grid=(nb, T // tb),
        in_specs=[
            zspec(0), zspec(nb), zspec(2 * nb), zspec(3 * nb),
            pl.BlockSpec((tb, LANES), lambda g, t: (t, Z_LORA // LANES)),
            pl.BlockSpec((16, width), lambda g, t: (0, g)),
            pl.BlockSpec((8, LANES), lambda g, t: (0, 0)),
            pl.BlockSpec((LANES, width), lambda g, t: (0, g)),
            pl.BlockSpec((LANES, width), lambda g, t: (0, g)),
        ],
        out_specs=pl.BlockSpec((tb, width), lambda g, t: (t, g)),
        out_shape=jax.ShapeDtypeStruct((T, WIDTH), bf16),
        scratch_shapes=[pltpu.VMEM((npair, LANES, LANES), f32), pltpu.VMEM((3, 8, width), f32),
                        pltpu.VMEM((8, LANES), f32)],
        compiler_params=pltpu.CompilerParams(
            dimension_semantics=("parallel", "arbitrary"), vmem_limit_bytes=VMEM_LIMIT),
        name="rwkv7",
    )(z, z, z, z, z, par, mul, w2p, a2p)


def _out_kernel(perm_ref, x_ref, yr_ref, yf_ref, p_ref, wo_ref, wg_ref, wp_ref, g1_ref, g2_ref, o_ref, wfx_ref):
    @pl.when(pl.program_id(0) == 0)
    def _():
        for s in range(NHEADS):
            src = pl.multiple_of(WIDTH + perm_ref[s] * HEAD, HEAD)
            wfx_ref[s * HEAD:(s + 1) * HEAD, :] = wo_ref[pl.ds(src, HEAD), :]

    def rms(t, g):
        return t * lax.rsqrt(jnp.mean(t * t, axis=-1, keepdims=True) + RMS_EPS) * g

    m = _dot(yr_ref[...], wo_ref[0:WIDTH, :]) + _dot(yf_ref[...], wfx_ref[...])
    x1 = x_ref[...] + rms(m, g1_ref[...])
    gate = _sigmoid(_dot(rms(x1, g2_ref[...]).astype(bf16), wg_ref[...]))
    o_ref[...] = x1 + gate * _dot(p_ref[...].astype(bf16), wp_ref[...])


def _out_proj(perm, x, yr, yf, p, wo, wg, wp, g1, g2, *, tm):
    T = x.shape[0]
    const = lambda shape: pl.BlockSpec(shape, lambda i, perm: (0, 0), pipeline_mode=pl.Buffered(1))
    rows = lambda width: pl.BlockSpec((tm, width), lambda i, perm: (i, 0))
    grid_spec = pltpu.PrefetchScalarGridSpec(
        num_scalar_prefetch=1,
        grid=(T // tm,),
        in_specs=[
            rows(D_MODEL), rows(WIDTH), rows(WIDTH), rows(D_PLE),
            const((D_MODEL, D_MODEL)), const((D_MODEL, D_MODEL)), const((D_PLE, D_MODEL)),
            const((1, D_MODEL)), const((1, D_MODEL)),
        ],
        out_specs=rows(D_MODEL),
        scratch_shapes=[pltpu.VMEM((WIDTH, D_MODEL), bf16)],
    )
    return pl.pallas_call(
        _out_kernel,
        grid_spec=grid_spec,
        out_shape=jax.ShapeDtypeStruct((T, D_MODEL), f32),
        compiler_params=pltpu.CompilerParams(
            dimension_semantics=("arbitrary",), vmem_limit_bytes=VMEM_LIMIT),
        name="out_proj",
    )(perm, x, yr, yf, p, wo, wg, wp, g1, g2)


def _layer(x, p, pre_g, w_in, mu_r, mu_k, mu_v, mu_w, mu_a, w0, w2, a0, a2, k_k, k_a, r_k, ln_w, ln_b,
           b_f, q_g, k_g, w_out, post_g, ple_g, w_gate, w_ple):
    T = x.shape[0]
    rw_end = 4 * WIDTH
    lora_end = rw_end + 2 * LORA
    fx_end = lora_end + 4 * WIDTH
    assert x.shape == (T, D_MODEL) and w_in.shape == (D_MODEL, fx_end + NHEADS), (x.shape, w_in.shape)
    assert all(T % min(t, T) == 0 for t in (PROJ_ROWS, PREP_ROWS, ATTN_Q, ATTN_K, RWKV_ROWS, OUT_ROWS)), T
    perm = jnp.argsort(b_f).astype(jnp.int32)
    fx_half = lora_end // HEAD
    src_half = jnp.concatenate([
        jnp.arange(rw_end // HEAD, dtype=jnp.int32),
        (fx_half + NHEADS * jnp.arange(4, dtype=jnp.int32)[:, None] + perm[None, :]).reshape(-1),
        jnp.arange(rw_end // HEAD, lora_end // HEAD, dtype=jnp.int32),
        jnp.zeros((2,), jnp.int32)])
    w_in_t = w_in.T
    pick = (jnp.pad(perm, (0, LANES - NHEADS), constant_values=-1)[:, None] == jnp.arange(NHEADS, dtype=jnp.int32)[None, :])
    w_ff_t = jnp.dot(pick.astype(f32), w_in_t[fx_end:], precision=lax.Precision.HIGHEST)
    w = _relayout_w_in(src_half, w_in_t, w_ff_t)
    b_f = jnp.take(b_f, perm)
    z = _in_proj(x, pre_g.reshape(1, D_MODEL), w, tm=min(PROJ_ROWS, T), tn=PROJ_COLS)

    gq = jnp.tile(q_g, NHEADS).reshape(1, WIDTH)
    gk = jnp.tile(k_g, NHEADS).reshape(1, WIDTH)
    bfp = jnp.pad(b_f, (0, LANES - NHEADS)).reshape(1, LANES)
    tb = min(PREP_ROWS, T)
    qt, ka, vt, fstat, ostat = _fox_prep(z, gq, gk, bfp, tb=tb)
    qk_bound = 1.02 * HEAD ** 0.5 * LOG2E * jnp.max(jnp.abs(q_g)) * jnp.max(jnp.abs(k_g))
    y_fx = _attention(qt, ka, vt, z, fstat, ostat, qk_bound, tb=tb, tq=min(ATTN_Q, T), tk=min(ATTN_K, T))

    par = jnp.stack([mu_r, mu_k, mu_v, w0, a0, k_k, k_a, r_k.reshape(WIDTH), ln_w, ln_b])
    par = jnp.pad(par, ((0, 16 - par.shape[0]), (0, 0)))
    mul = jnp.broadcast_to(jnp.concatenate([mu_w, mu_a]).reshape(1, LANES), (8, LANES))
    zeros = jnp.zeros((LORA, WIDTH), f32)
    w2p = jnp.concatenate([w2, zeros], axis=0).astype(bf16)
    a2p = jnp.concatenate([zeros, a2], axis=0).astype(bf16)
    y_rw = _rwkv(z, par, mul, w2p, a2p, tb=min(RWKV_ROWS, T), npair=RWKV_PAIRS)

    return _out_proj(perm, x, y_rw, y_fx, p, w_out.astype(bf16), w_gate.astype(bf16), w_ple.astype(bf16),
                     post_g.reshape(1, D_MODEL), ple_g.reshape(1, D_MODEL), tm=min(OUT_ROWS, T))


def kernel(x, p, pre_norm_g, w_in, rw_mu_r, rw_mu_k, rw_mu_v, rw_mu_w, rw_mu_a, rw_w0, rw_w2, rw_a0, rw_a2,
           rw_k_k, rw_k_a, rw_r_k, rw_ln_w, rw_ln_b, fx_b_f, fx_q_g, fx_k_g, w_out, post_norm_g, ple_norm_g,
           w_ple_gate, w_ple):
    B = x.shape[0]
    outs = []
    for b in range(B):
        xb = x[b]
        for i in range(p.shape[0]):
            xb = _layer(xb, p[i, b], pre_norm_g[i], w_in[i], rw_mu_r[i], rw_mu_k[i], rw_mu_v[i], rw_mu_w[i],
                        rw_mu_a[i], rw_w0[i], rw_w2[i], rw_a0[i], rw_a2[i], rw_k_k[i], rw_k_a[i], rw_r_k[i],
                        rw_ln_w[i], rw_ln_b[i], fx_b_f[i], fx_q_g[i], fx_k_g[i], w_out[i], post_norm_g[i],
                        ple_norm_g[i], w_ple_gate[i], w_ple[i])
        outs.append(xb)
    return jnp.stack(outs)
```

```python
import functools

import numpy as np
import jax
import jax.numpy as jnp
from jax import lax
from jax.experimental import pallas as pl
from jax.experimental.pallas import tpu as pltpu

f32 = jnp.float32
bf16 = jnp.bfloat16

D_MODEL = 2048
D_PLE = 256
WIDTH = 1024
HEAD = 64
NHEADS = 16
NPAIRS = NHEADS // 2
LORA = 64
LANES = 128
RMS_EPS = 1e-6
GN_EPS = 64e-5
CHUNK = 64
LOG2E = 1.4426950408889634
EXP_NEG_HALF = 0.6065306597126334
NEG_BIG = -1e30

Z_FX = 4096
Z_LORA = 8192
Z_FF = 8320
Z_COLS = 8448
VT_ROWS = 80
Q_COL = 128
SKIP_LOG2 = 40.0


VMEM_LIMIT = 56 * 1024 * 1024
PROJ_ROWS, PROJ_COLS = 1024, 1408
PREP_ROWS = 512
ATTN_Q, ATTN_K = 512, 512
RWKV_ROWS, RWKV_PAIRS = 256, 8
OUT_ROWS = 512


def _dot(a, b):
    return jnp.dot(a, b, preferred_element_type=f32)


def _dot_nt(a, b):
    return lax.dot_general(a, b, (((1,), (1,)), ((), ())), preferred_element_type=f32)


def _dot_tn(a, b):
    return lax.dot_general(a, b, (((0,), (0,)), ((), ())), preferred_element_type=f32)


def _split2(x):
    hi = x.astype(bf16)
    lo = (x - hi.astype(f32)).astype(bf16)
    return hi, lo


def _split3(x):
    hi = x.astype(bf16)
    r = x - hi.astype(f32)
    mid = r.astype(bf16)
    lo = (r - mid.astype(f32)).astype(bf16)
    return hi, mid, lo


def _sel_dot(sel, x):
    hi, mid, lo = _split3(x)
    return _dot(sel, hi) + _dot(sel, mid) + _dot(sel, lo)


def _sigmoid(x):
    return 1.0 / (1.0 + jnp.exp(-x))


RELAYOUT_SLABS = 4


def _relayout_kernel(src_ref, *refs):
    slab_refs, ff_ref, o_ref = refs[:RELAYOUT_SLABS], refs[RELAYOUT_SLABS], refs[RELAYOUT_SLABS + 1]
    slabs = [r[...] for r in slab_refs]
    is_last = pl.program_id(0) == pl.num_programs(0) - 1
    slabs[-2] = jnp.where(is_last, ff_ref[0:HEAD, :], slabs[-2])
    slabs[-1] = jnp.where(is_last, ff_ref[HEAD:, :], slabs[-1])
    o_ref[...] = jnp.concatenate(slabs, axis=0).T.astype(bf16)


def _relayout_w_in(src_half, w_in_t, w_ff_t):
    n = RELAYOUT_SLABS
    slab = lambda k: pl.BlockSpec((HEAD, D_MODEL), lambda c, src: (src[n * c + k], 0))
    grid_spec = pltpu.PrefetchScalarGridSpec(
        num_scalar_prefetch=1,
        grid=(Z_COLS // (n * HEAD),),
        in_specs=[slab(k) for k in range(n)] + [pl.BlockSpec((LANES, D_MODEL), lambda c, src: (0, 0))],
        out_specs=pl.BlockSpec((D_MODEL, n * HEAD), lambda c, src: (0, c)),
    )
    return pl.pallas_call(
        _relayout_kernel,
        grid_spec=grid_spec,
        out_shape=jax.ShapeDtypeStruct((D_MODEL, Z_COLS), bf16),
        compiler_params=pltpu.CompilerParams(
            dimension_semantics=("parallel",), vmem_limit_bytes=VMEM_LIMIT),
        name="w_in_relayout",
    )(src_half, *([w_in_t] * n), w_ff_t)


def _in_proj_kernel(x_ref, g_ref, w_ref, z_ref, h_ref):
    @pl.when(pl.program_id(1) == 0)
    def _():
        x = x_ref[...]
        ms = jnp.mean(x * x, axis=-1, keepdims=True)
        h_ref[...] = (x * lax.rsqrt(ms + RMS_EPS) * g_ref[...]).astype(bf16)

    z_ref[...] = _dot(h_ref[...], w_ref[...])


def _in_proj(x, g, w, *, tm, tn):
    T = x.shape[0]
    return pl.pallas_call(
        _in_proj_kernel,
        grid=(T // tm, Z_COLS // tn),
        in_specs=[
            pl.BlockSpec((tm, D_MODEL), lambda i, j: (i, 0)),
            pl.BlockSpec((1, D_MODEL), lambda i, j: (0, 0)),
            pl.BlockSpec((D_MODEL, tn), lambda i, j: (0, j)),
        ],
        out_specs=pl.BlockSpec((tm, tn), lambda i, j: (i, j)),
        out_shape=jax.ShapeDtypeStruct((T, Z_COLS), f32),
        scratch_shapes=[pltpu.VMEM((tm, D_MODEL), bf16)],
        compiler_params=pltpu.CompilerParams(
            dimension_semantics=("parallel", "arbitrary"), vmem_limit_bytes=VMEM_LIMIT),
        name="in_proj",
    )(x, g, w)


def _fox_prep_kernel(fq_ref, fk_ref, fv_ref, ff_ref, gq_ref, gk_ref, bf_ref,
                     qt_ref, ka_ref, vt_ref, fs_ref, os_ref, carry_ref, *, tb):
    @pl.when(pl.program_id(0) == 0)
    def _():
        carry_ref[...] = jnp.zeros_like(carry_ref)

    bi = lax.broadcasted_iota(jnp.int32, (LANES, LANES), 0)
    bj = lax.broadcasted_iota(jnp.int32, (LANES, LANES), 1)
    same_head = jnp.where((bi >> 6) == (bj >> 6), 1.0, 0.0).astype(bf16)

    def head_rms(x, gain):
        cols = []
        for c in range(WIDTH // LANES):
            xc = x[:, c * LANES:(c + 1) * LANES]
            ssq = _dot((xc * xc).astype(bf16), same_head)
            cols.append(xc * lax.rsqrt(ssq * (1.0 / HEAD) + RMS_EPS))
        return jnp.concatenate(cols, axis=1) * gain

    qn = head_rms(fq_ref[...], gq_ref[...]) * (HEAD ** -0.5 * LOG2E)
    kn = head_rms(fk_ref[...], gk_ref[...])

    xf = ff_ref[...] + bf_ref[...]
    logf = jnp.minimum(xf, 0.0) - jnp.log(1.0 + jnp.exp(-jnp.abs(xf)))
    ri = lax.broadcasted_iota(jnp.int32, (tb, tb), 0)
    rj = lax.broadcasted_iota(jnp.int32, (tb, tb), 1)
    tri = jnp.where(rj <= ri, 1.0, 0.0).astype(bf16)
    cum = _sel_dot(tri, logf) + carry_ref[0:1, :]
    carry_ref[...] = jnp.broadcast_to(cum[tb - 1:tb, :], carry_ref.shape)
    cum2 = cum * LOG2E
    c_hi, c_mid, c_lo = (t.astype(f32) for t in _split3(cum2))
    srow = lax.broadcasted_iota(jnp.int32, (8, LANES), 0)
    fs_ref[0] = jnp.where(srow == 0, jnp.max(cum2, axis=0, keepdims=True), jnp.min(cum2, axis=0, keepdims=True))
    own = qn * kn
    own = jnp.concatenate([_dot(own[:, c * LANES:(c + 1) * LANES].astype(bf16), same_head)
                           for c in range(WIDTH // LANES)], axis=1)
    os_ref[0] = jnp.broadcast_to(jnp.min(own, axis=0, keepdims=True), (8, WIDTH))

    lane = lax.broadcasted_iota(jnp.int32, (tb, LANES), 1)
    erow = lax.broadcasted_iota(jnp.int32, (HEAD, tb), 0)
    q_ones = jnp.where(erow < 6, 1.0, 0.0)
    qn_t, vt = qn.T, fv_ref[...].T
    hi_t, mid_t, lo_t = c_hi.T, c_mid.T, c_lo.T
    vpad = jnp.concatenate(
        [jnp.ones((1, tb), f32), jnp.zeros((VT_ROWS - HEAD - 1, tb), f32)], axis=0)
    for h in range(NHEADS):
        even = h % 2 == 0
        base = HEAD if even else 0
        q_bias = jnp.where(erow == 0, hi_t[h:h + 1, :], jnp.where(erow == 1, mid_t[h:h + 1, :],
                 jnp.where(erow == 2, lo_t[h:h + 1, :], q_ones)))
        q_rows = qn_t[h * HEAD:(h + 1) * HEAD, :]
        qt_ref[h] = jnp.concatenate([q_rows, q_bias] if even else [q_bias, q_rows], axis=0).astype(bf16)
        fh, fm, fl = c_hi[:, h:h + 1], c_mid[:, h:h + 1], c_lo[:, h:h + 1]
        k_ones = jnp.where((lane >= base) & (lane < base + 3), 1.0, 0.0)
        k_bias = jnp.where(lane == base + 3, -fh, jnp.where(lane == base + 4, -fm,
                 jnp.where(lane == base + 5, -fl, k_ones)))
        kc = kn[:, (h // 2) * LANES:(h // 2 + 1) * LANES]
        ka_ref[h] = jnp.where((lane < HEAD) == even, kc, k_bias).astype(bf16)
        vt_ref[h, 0] = jnp.concatenate([vt[h * HEAD:(h + 1) * HEAD, :], vpad], axis=0).astype(bf16)


def _fox_prep(z, gq, gk, bfp, *, tb):
    T = z.shape[0]
    wb = Z_FX // WIDTH
    return pl.pallas_call(
        functools.partial(_fox_prep_kernel, tb=tb),
        grid=(T // tb,),
        in_specs=[
            pl.BlockSpec((tb, WIDTH), lambda i: (i, wb)),
            pl.BlockSpec((tb, WIDTH), lambda i: (i, wb + 1)),
            pl.BlockSpec((tb, WIDTH), lambda i: (i, wb + 2)),
            pl.BlockSpec((tb, LANES), lambda i: (i, Z_FF // LANES)),
            pl.BlockSpec((1, WIDTH), lambda i: (0, 0)),
            pl.BlockSpec((1, WIDTH), lambda i: (0, 0)),
            pl.BlockSpec((1, LANES), lambda i: (0, 0)),
        ],
        out_specs=[
            pl.BlockSpec((NHEADS, LANES, tb), lambda i: (0, 0, i)),
            pl.BlockSpec((NHEADS, tb, LANES), lambda i: (0, i, 0)),
            pl.BlockSpec((NHEADS, 1, VT_ROWS, tb), lambda i: (0, i, 0, 0)),
            pl.BlockSpec((1, 8, LANES), lambda i: (i, 0, 0)),
            pl.BlockSpec((1, 8, WIDTH), lambda i: (i, 0, 0)),
        ],
        out_shape=[
            jax.ShapeDtypeStruct((NHEADS, LANES, T), bf16),
            jax.ShapeDtypeStruct((NHEADS, T, LANES), bf16),
            jax.ShapeDtypeStruct((NHEADS, T // tb, VT_ROWS, tb), bf16),
            jax.ShapeDtypeStruct((T // tb, 8, LANES), f32),
            jax.ShapeDtypeStruct((T // tb, 8, WIDTH), f32),
        ],
        scratch_shapes=[pltpu.VMEM((8, LANES), f32)],
        compiler_params=pltpu.CompilerParams(
            dimension_semantics=("arbitrary",), vmem_limit_bytes=VMEM_LIMIT),
        name="fox_prep",
    )(z, z, z, z, gq, gk, bfp)


def _attn_kernel(cnt_ref, blk_ref, qt_ref, ka_hbm, vt_hbm, fg_ref, o_ref, m_ref, acc_ref, kbuf, vbuf, sem, par_ref,
                 *, tq, tk):
    g, i = pl.program_id(0), pl.program_id(1)
    nq = pl.num_programs(1)
    ndiag = tq // tk
    max_full = ndiag * nq
    step = g * nq + i
    n_full = cnt_ref[step]

    def block_id(step_, t, n_full_, i_):
        return jnp.where(t < n_full_, blk_ref[step_ * max_full + jnp.minimum(t, max_full - 1)], ndiag * i_ + t - n_full_)

    def copies(g_, j, slot):
        return (pltpu.make_async_copy(ka_hbm.at[pl.ds(2 * g_, 2), pl.ds(j * tk, tk), :], kbuf.at[slot], sem.at[0, slot]),
                pltpu.make_async_copy(vt_hbm.at[pl.ds(2 * g_, 2), j], vbuf.at[slot], sem.at[1, slot]))

    def fetch(g_, j, slot):
        for c in copies(g_, j, slot):
            c.start()

    def wait(slot):
        for c in copies(g, 0, slot):
            c.wait()

    @pl.when(step == 0)
    def _():
        par_ref[0] = 0
        fetch(g, block_id(step, 0, n_full, i), 0)

    slot0 = par_ref[0]
    par_ref[0] = (slot0 + n_full + ndiag) % 2
    m_ref[...] = jnp.full(m_ref.shape, NEG_BIG, f32)
    acc_ref[...] = jnp.zeros_like(acc_ref)

    def update(d, slot):
        ka_ref, vt_ref = kbuf.at[slot], vbuf.at[slot]

        def n_keys(q_lo):
            return tk if d < 0 else max(0, min(tk, q_lo + Q_COL - d * tk))

        chains = [(hh, qs * Q_COL) for qs in range(tq // Q_COL) for hh in range(2) if n_keys(qs * Q_COL) > 0]

        def scores(hh, q_lo):
            nk = n_keys(q_lo)
            st = _dot(ka_ref[hh, 0:nk, :], qt_ref[hh, :, q_lo:q_lo + Q_COL])
            if d >= 0 and d * tk + nk - 1 > q_lo:
                kpos = d * tk + lax.broadcasted_iota(jnp.int32, st.shape, 0)
                qpos = q_lo + lax.broadcasted_iota(jnp.int32, st.shape, 1)
                st = jnp.where(kpos <= qpos, st, NEG_BIG)
            return st

        def consume(hh, q_lo, st):
            qsl = slice(q_lo, q_lo + Q_COL)
            m_old = m_ref[hh, :, qsl]
            m_new = jnp.maximum(m_old, jnp.max(st, axis=0, keepdims=True))
            m_ref[hh, :, qsl] = m_new
            p = jnp.exp2(st - m_new).astype(bf16)
            acc_ref[hh, :, qsl] = (jnp.exp2(m_old - m_new) * acc_ref[hh, :, qsl]
                                   + _dot(vt_ref[hh, :, 0:n_keys(q_lo)], p))

        ahead = 8
        pending = [scores(*c) for c in chains[:ahead]]
        for n, chain in enumerate(chains):
            st = pending.pop(0)
            if n + ahead < len(chains):
                pending.append(scores(*chains[n + ahead]))
            consume(*chain, st)

    def run_block(t, d):
        slot = (slot0 + t) % 2
        wait(slot)
        fetch(g, block_id(step, t + 1, n_full, i), 1 - slot)
        update(d, slot)

    def full_block(t, carry):
        run_block(t, -1)
        return carry

    lax.fori_loop(0, n_full, full_block, 0)
    for d in range(ndiag - 1):
        run_block(n_full + d, d)

    last_slot = (slot0 + n_full + ndiag - 1) % 2
    wait(last_slot)

    @pl.when(step + 1 < pl.num_programs(0) * nq)
    def _():
        nstep = step + 1
        fetch(nstep // nq, block_id(nstep, 0, cnt_ref[nstep], nstep % nq), 1 - last_slot)

    update(ndiag - 1, last_slot)

    outs = []
    for hh in range(2):
        acc = acc_ref[hh]
        outs.append(acc[0:HEAD, :] / acc[HEAD:HEAD + 1, :])
    y = jnp.concatenate(outs, axis=0).T
    gate = fg_ref[...]
    o_ref[...] = (y * gate * _sigmoid(gate)).astype(o_ref.dtype)


def _attention_work_list(fstat, ostat, qk_bound, *, T, tb, tq, tk):
    nq, nk = T // tq, T // tk
    fmax_q = fstat[:, 0, :NHEADS].reshape(nq, tq // tb, NHEADS).max(axis=1)
    fmin_k = fstat[:, 1, :NHEADS].reshape(nk, tk // tb, NHEADS).min(axis=1)
    own_q = ostat[:, 0, ::HEAD].reshape(nq, tq // tb, NHEADS).min(axis=1) - 0.05 * qk_bound
    need = (fmax_q[:, None, :] - fmin_k[None, :, :] + qk_bound - own_q[:, None, :]) > -SKIP_LOG2
    need = jnp.transpose(need[:, :, 0::2] | need[:, :, 1::2], (2, 0, 1))
    before = np.arange(nk)[None, :] < (tq // tk) * np.arange(nq)[:, None]
    need = need & jnp.asarray(before)[None]
    count = jnp.sum(need, axis=2).astype(jnp.int32)
    ids = jnp.argsort(jnp.logical_not(need), axis=2, stable=True).astype(jnp.int32)
    return count.reshape(-1), ids.reshape(-1)


def _attention(qt, ka, vt, z, fstat, ostat, qk_bound, *, tb, tq, tk):
    T = ka.shape[1]
    nq = T // tq
    assert vt.shape == (NHEADS, T // tk, VT_ROWS, tk), vt.shape
    count, ids = _attention_work_list(fstat, ostat, qk_bound, T=T, tb=tb, tq=tq, tk=tk)
    fg_col = (Z_FX + 3 * WIDTH) // LANES
    grid_spec = pltpu.PrefetchScalarGridSpec(
        num_scalar_prefetch=2,
        grid=(NPAIRS, nq),
        in_specs=[
            pl.BlockSpec((2, LANES, tq), lambda g, i, cnt, blk: (g, 0, i)),
            pl.BlockSpec(memory_space=pl.ANY),
            pl.BlockSpec(memory_space=pl.ANY),
            pl.BlockSpec((tq, LANES), lambda g, i, cnt, blk: (i, fg_col + g)),
        ],
        out_specs=pl.BlockSpec((tq, LANES), lambda g, i, cnt, blk: (i, g)),
        scratch_shapes=[
            pltpu.VMEM((2, 1, tq), f32), pltpu.VMEM((2, VT_ROWS, tq), f32),
            pltpu.VMEM((2, 2, tk, LANES), bf16), pltpu.VMEM((2, 2, VT_ROWS, tk), bf16),
            pltpu.SemaphoreType.DMA((2, 2)), pltpu.SMEM((1,), jnp.int32),
        ],
    )
    return pl.pallas_call(
        functools.partial(_attn_kernel, tq=tq, tk=tk),
        grid_spec=grid_spec,
        out_shape=jax.ShapeDtypeStruct((T, WIDTH), bf16),
        compiler_params=pltpu.CompilerParams(
            dimension_semantics=("arbitrary", "arbitrary"), vmem_limit_bytes=VMEM_LIMIT),
        name="fox_attention",
    )(count, ids, qt, ka, vt, z)


P_MU_R, P_MU_K, P_MU_V, P_W0, P_A0, P_KK, P_KA, P_RK, P_LNW, P_LNB = range(10)


def _rwkv_kernel(zr_ref, zk_ref, zv_ref, zg_ref, zl_ref, par_ref, mul_ref, w2_ref, a2_ref,
                 o_ref, s_ref, prev_ref, prevl_ref, *, tb, npair):
    width = npair * LANES

    @pl.when(pl.program_id(1) == 0)
    def _():
        s_ref[...] = jnp.zeros_like(s_ref)
        prev_ref[...] = jnp.zeros_like(prev_ref)
        prevl_ref[...] = jnp.zeros_like(prevl_ref)

    par = par_ref[...]
    prow = lambda k: par[k:k + 1, :]

    def shift(x, mu, last_ref):
        row = lax.broadcasted_iota(jnp.int32, x.shape, 0)
        xp = pltpu.roll(x, 1, axis=0)
        xp = jnp.where(row == 0, last_ref[7:8, :], xp)
        last_ref[...] = x[tb - 8:tb, :]
        return x + (xp - x) * mu

    r = shift(zr_ref[...], prow(P_MU_R), prev_ref.at[0])
    k = shift(zk_ref[...], prow(P_MU_K), prev_ref.at[1])
    v = shift(zv_ref[...], prow(P_MU_V), prev_ref.at[2])
    sl = shift(zl_ref[...], mul_ref[0:1, :], prevl_ref)

    bi = lax.broadcasted_iota(jnp.int32, (LANES, LANES), 0)
    bj = lax.broadcasted_iota(jnp.int32, (LANES, LANES), 1)
    same_head = jnp.where((bi >> 6) == (bj >> 6), 1.0, 0.0).astype(bf16)

    def seg(x):
        return jnp.concatenate(
            [_dot(x[:, pr * LANES:(pr + 1) * LANES].astype(bf16), same_head) for pr in range(npair)], axis=1)

    u = prow(P_W0) + _dot(jnp.tanh(sl).astype(bf16), w2_ref[...])
    ld = -EXP_NEG_HALF * _sigmoid(u)
    av = _sigmoid(prow(P_A0) + _dot(sl.astype(bf16), a2_ref[...]))

    kk = k * prow(P_KK)
    kk = kk * lax.rsqrt(jnp.maximum(seg(kk * kk), 1e-24))
    k2 = k * (1.0 + (av - 1.0) * prow(P_KA))

    ti = lax.broadcasted_iota(jnp.int32, (tb, tb), 0)
    tj = lax.broadcasted_iota(jnp.int32, (tb, tb), 1)
    tri = jnp.where(((ti >> 6) == (tj >> 6)) & (tj <= ti), 1.0, 0.0).astype(bf16)
    ld_hi, ld_lo = _split2(ld)
    cum = _dot(tri, ld_hi) + _dot(tri, ld_lo)

    e_pos = jnp.exp(cum)
    e_neg = jnp.exp(-cum)
    a_t = -kk * jnp.exp(cum - ld)
    b_t = kk * av * e_neg
    k_t = k2 * e_neg
    r_t = r * e_pos

    ri = lax.broadcasted_iota(jnp.int32, (LANES, LANES), 0)
    rj = lax.broadcasted_iota(jnp.int32, (LANES, LANES), 1)
    strict_lower = rj < ri
    eye = jnp.where(ri == rj, 1.0, 0.0)
    qi = lax.broadcasted_iota(jnp.int32, (CHUNK, LANES), 0)
    qj = lax.broadcasted_iota(jnp.int32, (CHUNK, LANES), 1)
    incl_lower = (qj & (CHUNK - 1)) <= qi
    chunk_head0 = qj < HEAD

    def stack(x):
        return jnp.concatenate([jnp.where(chunk_head0, x, 0.0), jnp.where(chunk_head0, 0.0, x)], axis=0)

    def level_mask(b):
        sh = b.bit_length()
        return ((ri >> sh) == (rj >> sh)) & ((ri & (2 * b - 1)) >= b) & ((rj & (2 * b - 1)) < b)

    nchunk = tb // CHUNK
    units = [(pr, c) for pr in range(npair) for c in range(nchunk)]
    rows = lambda c: slice(c * CHUNK, (c + 1) * CHUNK)
    lanes = lambda pr: slice(pr * LANES, (pr + 1) * LANES)
    cut = lambda x, u: x[rows(u[1]), lanes(u[0])]
    cend = {u: cum[(u[1] + 1) * CHUNK - 1:(u[1] + 1) * CHUNK, lanes(u[0])] for u in units}

    am = {u: stack(cut(a_t, u)) for u in units}
    vm = {u: stack(cut(v, u)) for u in units}
    x = {u: _dot_nt(jnp.concatenate([am[u], cut(r_t, u)], axis=0).astype(bf16),
                    jnp.concatenate([stack(cut(b_t, u)), stack(cut(k_t, u))], axis=0).astype(bf16))
         for u in units}
    l_ab = {u: jnp.where(strict_lower, x[u][0:LANES, 0:LANES], 0.0) for u in units}
    l_ak = {u: jnp.where(strict_lower, x[u][0:LANES, LANES:], 0.0) for u in units}
    l_r = {u: jnp.where(jnp.concatenate([incl_lower, incl_lower], axis=1), x[u][LANES:, :], 0.0).astype(bf16)
           for u in units}

    inv = {u: eye + jnp.where(level_mask(1), l_ab[u], 0.0) for u in units}
    b = 2
    while b < CHUNK:
        mask = level_mask(b)
        y_ = {u: _dot(jnp.where(mask, l_ab[u], 0.0).astype(bf16), inv[u].astype(bf16)) for u in units}
        inv = {u: inv[u] + _dot(inv[u].astype(bf16), y_[u].astype(bf16)) for u in units}
        b *= 2

    lv = {u: _dot(l_ak[u].astype(bf16), vm[u].astype(bf16)) for u in units}
    tg = {u: _dot(inv[u].astype(bf16), jnp.concatenate([am[u], lv[u]], axis=1).astype(bf16)) for u in units}
    zeros = jnp.zeros((LANES, LANES), f32)
    ly = {u: _dot(l_r[u], jnp.concatenate(
              [tg[u], jnp.concatenate([zeros, vm[u]], axis=1)], axis=0).astype(bf16)) for u in units}
    bh = {u: stack(cut(kk, u) * cut(av, u) * jnp.exp(cend[u] - cut(cum, u))).astype(bf16) for u in units}
    kh = {u: stack(cut(k2, u) * jnp.exp(cend[u] - cut(cum, u))).astype(bf16) for u in units}
    ag = {u: _dot_tn(tg[u].astype(bf16), bh[u]) for u in units}
    vk = {u: _dot_tn(vm[u].astype(bf16), kh[u]) for u in units}
    q_eff = {u: (cut(r_t, u) + ly[u][:, 0:LANES]).astype(bf16) for u in units}
    a_eff = {u: ag[u][0:LANES, :].astype(bf16) for u in units}
    g_eff = {u: ag[u][LANES:, :] + vk[u] for u in units}

    ys = [[None] * npair for _ in range(nchunk)]
    state = [s_ref[pr] for pr in range(npair)]
    for c in range(nchunk):
        for pr in range(npair):
            u = (pr, c)
            s_bf = state[pr].astype(bf16)
            ys[c][pr] = _dot_nt(q_eff[u], s_bf) + ly[u][:, LANES:]
            state[pr] = state[pr] * jnp.exp(cend[u]) + _dot(s_bf, a_eff[u]) + g_eff[u]
    for pr in range(npair):
        s_ref[pr] = state[pr]

    y = jnp.concatenate([jnp.concatenate(yc, axis=1) for yc in ys], axis=0)
    mean = seg(y) * (1.0 / HEAD)
    d = y - mean
    var = seg(d * d) * (1.0 / HEAD)
    yn = d * lax.rsqrt(var + GN_EPS) * prow(P_LNW) + prow(P_LNB)
    bonus = seg(r * k2 * prow(P_RK)) * v
    g = zg_ref[...]
    o_ref[...] = ((yn + bonus) * g * _sigmoid(g)).astype(o_ref.dtype)


def _rwkv(z, par, mul, w2p, a2p, *, tb, npair):
    T = z.shape[0]
    width = npair * LANES
    nb = WIDTH // width
    zspec = lambda off: pl.BlockSpec((tb, width), lambda g, t: (t, off + g))
    return pl.pallas_call(
        functools.partial(_rwkv_kernel, tb=tb, npair=npair),
        grid=(nb, T // tb),
        in_specs=[
            zspec(0), zspec(nb), zspec(2 * nb), zspec(3 * nb),
            pl.BlockSpec((tb, LANES), lambda g, t: (t, Z_LORA // LANES)),
            pl.BlockSpec((16, width), lambda g, t: (0, g)),
            pl.BlockSpec((8, LANES), lambda g, t: (0, 0)),
            pl.BlockSpec((LANES, width), lambda g, t: (0, g)),
            pl.BlockSpec((LANES, width), lambda g, t: (0, g)),
        ],
        out_specs=pl.BlockSpec((tb, width), lambda g, t: (t, g)),
        out_shape=jax.ShapeDtypeStruct((T, WIDTH), bf16),
        scratch_shapes=[pltpu.VMEM((npair, LANES, LANES), f32), pltpu.VMEM((3, 8, width), f32),
                        pltpu.VMEM((8, LANES), f32)],
        compiler_params=pltpu.CompilerParams(
            dimension_semantics=("parallel", "arbitrary"), vmem_limit_bytes=VMEM_LIMIT),
        name="rwkv7",
    )(z, z, z, z, z, par, mul, w2p, a2p)


def _out_kernel(perm_ref, x_ref, yr_ref, yf_ref, p_ref, wo_ref, wg_ref, wp_ref, g1_ref, g2_ref, o_ref, wfx_ref):
    @pl.when(pl.program_id(0) == 0)
    def _():
        for s in range(NHEADS):
            src = pl.multiple_of(WIDTH + perm_ref[s] * HEAD, HEAD)
            wfx_ref[s * HEAD:(s + 1) * HEAD, :] = wo_ref[pl.ds(src, HEAD), :]

    def rms(t, g):
        return t * lax.rsqrt(jnp.mean(t * t, axis=-1, keepdims=True) + RMS_EPS) * g

    m = _dot(yr_ref[...], wo_ref[0:WIDTH, :]) + _dot(yf_ref[...], wfx_ref[...])
    x1 = x_ref[...] + rms(m, g1_ref[...])
    gate = _sigmoid(_dot(rms(x1, g2_ref[...]).astype(bf16), wg_ref[...]))
    o_ref[...] = x1 + gate * _dot(p_ref[...].astype(bf16), wp_ref[...])


def _out_proj(perm, x, yr, yf, p, wo, wg, wp, g1, g2, *, tm):
    T = x.shape[0]
    const = lambda shape: pl.BlockSpec(shape, lambda i, perm: (0, 0), pipeline_mode=pl.Buffered(1))
    rows = lambda width: pl.BlockSpec((tm, width), lambda i, perm: (i, 0))
    grid_spec = pltpu.PrefetchScalarGridSpec(
        num_scalar_prefetch=1,
        grid=(T // tm,),
        in_specs=[
            rows(D_MODEL), rows(WIDTH), rows(WIDTH), rows(D_PLE),
            const((D_MODEL, D_MODEL)), const((D_MODEL, D_MODEL)), const((D_PLE, D_MODEL)),
            const((1, D_MODEL)), const((1, D_MODEL)),
        ],
        out_specs=rows(D_MODEL),
        scratch_shapes=[pltpu.VMEM((WIDTH, D_MODEL), bf16)],
    )
    return pl.pallas_call(
        _out_kernel,
        grid_spec=grid_spec,
        out_shape=jax.ShapeDtypeStruct((T, D_MODEL), f32),
        compiler_params=pltpu.CompilerParams(
            dimension_semantics=("arbitrary",), vmem_limit_bytes=VMEM_LIMIT),
        name="out_proj",
    )(perm, x, yr, yf, p, wo, wg, wp, g1, g2)


def _layer(x, p, pre_g, w_in, mu_r, mu_k, mu_v, mu_w, mu_a, w0, w2, a0, a2, k_k, k_a, r_k, ln_w, ln_b,
           b_f, q_g, k_g, w_out, post_g, ple_g, w_gate, w_ple):
    T = x.shape[0]
    rw_end = 4 * WIDTH
    lora_end = rw_end + 2 * LORA
    fx_end = lora_end + 4 * WIDTH
    assert x.shape == (T, D_MODEL) and w_in.shape == (D_MODEL, fx_end + NHEADS), (x.shape, w_in.shape)
    assert all(T % min(t, T) == 0 for t in (PROJ_ROWS, PREP_ROWS, ATTN_Q, ATTN_K, RWKV_ROWS, OUT_ROWS)), T
    perm = jnp.argsort(b_f).astype(jnp.int32)
    fx_half = lora_end // HEAD
    src_half = jnp.concatenate([
        jnp.arange(rw_end // HEAD, dtype=jnp.int32),
        (fx_half + NHEADS * jnp.arange(4, dtype=jnp.int32)[:, None] + perm[None, :]).reshape(-1),
        jnp.arange(rw_end // HEAD, lora_end // HEAD, dtype=jnp.int32),
        jnp.zeros((2,), jnp.int32)])
    w_in_t = w_in.T
    pick = (jnp.pad(perm, (0, LANES - NHEADS), constant_values=-1)[:, None] == jnp.arange(NHEADS, dtype=jnp.int32)[None, :])
    w_ff_t = jnp.dot(pick.astype(f32), w_in_t[fx_end:], precision=lax.Precision.HIGHEST)
    w = _relayout_w_in(src_half, w_in_t, w_ff_t)
    b_f = jnp.take(b_f, perm)
    z = _in_proj(x, pre_g.reshape(1, D_MODEL), w, tm=min(PROJ_ROWS, T), tn=PROJ_COLS)

    gq = jnp.tile(q_g, NHEADS).reshape(1, WIDTH)
    gk = jnp.tile(k_g, NHEADS).reshape(1, WIDTH)
    bfp = jnp.pad(b_f, (0, LANES - NHEADS)).reshape(1, LANES)
    tb = min(PREP_ROWS, T)
    qt, ka, vt, fstat, ostat = _fox_prep(z, gq, gk, bfp, tb=tb)
    qk_bound = 1.02 * HEAD ** 0.5 * LOG2E * jnp.max(jnp.abs(q_g)) * jnp.max(jnp.abs(k_g))
    y_fx = _attention(qt, ka, vt, z, fstat, ostat, qk_bound, tb=tb, tq=min(ATTN_Q, T), tk=min(ATTN_K, T))

    par = jnp.stack([mu_r, mu_k, mu_v, w0, a0, k_k, k_a, r_k.reshape(WIDTH), ln_w, ln_b])
    par = jnp.pad(par, ((0, 16 - par.shape[0]), (0, 0)))
    mul = jnp.broadcast_to(jnp.concatenate([mu_w, mu_a]).reshape(1, LANES), (8, LANES))
    zeros = jnp.zeros((LORA, WIDTH), f32)
    w2p = jnp.concatenate([w2, zeros], axis=0).astype(bf16)
    a2p = jnp.concatenate([zeros, a2], axis=0).astype(bf16)
    y_rw = _rwkv(z, par, mul, w2p, a2p, tb=min(RWKV_ROWS, T), npair=RWKV_PAIRS)

    return _out_proj(perm, x, y_rw, y_fx, p, w_out.astype(bf16), w_gate.astype(bf16), w_ple.astype(bf16),
                     post_g.reshape(1, D_MODEL), ple_g.reshape(1, D_MODEL), tm=min(OUT_ROWS, T))


def kernel(x, p, pre_norm_g, w_in, rw_mu_r, rw_mu_k, rw_mu_v, rw_mu_w, rw_mu_a, rw_w0, rw_w2, rw_a0, rw_a2,
           rw_k_k, rw_k_a, rw_r_k, rw_ln_w, rw_ln_b, fx_b_f, fx_q_g, fx_k_g, w_out, post_norm_g, ple_norm_g,
           w_ple_gate, w_ple):
    B = x.shape[0]
    outs = []
    for b in range(B):
        xb = x[b]
        for i in range(p.shape[0]):
            xb = _layer(xb, p[i, b], pre_norm_g[i], w_in[i], rw_mu_r[i], rw_mu_k[i], rw_mu_v[i], rw_mu_w[i],
                        rw_mu_a[i], rw_w0[i], rw_w2[i], rw_a0[i], rw_a2[i], rw_k_k[i], rw_k_a[i], rw_r_k[i],
                        rw_ln_w[i], rw_ln_b[i], fx_b_f[i], fx_q_g[i], fx_k_g[i], w_out[i], post_norm_g[i],
                        ple_norm_g[i], w_ple_gate[i], w_ple[i])
        outs.append(xb)
    return jnp.stack(outs)
```
